```python
import jax, jax.numpy as jnp
from jax import lax
import numpy as np

D_MODEL = 4096
BATCH = 8
SEQ = 4096
DEPTH = 1

MIX_WIDTH = D_MODEL
HEAD_DIM = 128
A_WIDTH = MIX_WIDTH // 2
A_HEADS = A_WIDTH // HEAD_DIM
A_KEY_DIM = 128
A_KEY_WIDTH = A_HEADS * A_KEY_DIM
B_WIDTH = MIX_WIDTH - A_WIDTH
B_GROUP_DIM = 128
B_GROUPS = B_WIDTH // B_GROUP_DIM
GMLP_CHUNK = 128
GLA_CHUNK = 64
IN_COLS = 2 * A_KEY_WIDTH + 2 * A_WIDTH + 2 * B_WIDTH
D_FF = ((8 * D_MODEL + 3 * 256 - 1) // (3 * 256)) * 256
PLE_DIM = 256
EPS = 1e-6

kernel_name = "hymba_hgrn2_gmlp_hybrid"


def _rmsnorm(x, w):
    xf = x.astype(jnp.float32)
    y = xf * lax.rsqrt(jnp.mean(xf * xf, axis=-1, keepdims=True) + EPS)
    return (y * w.astype(jnp.float32)).astype(x.dtype)


def _hgrn2(q, f_pre, i_in, g, lb, norm_w):
    bsz, t, _ = q.shape
    n = t // GLA_CHUNK
    f32 = jnp.float32
    lbf = lb.astype(f32)
    qf = jax.nn.silu(q.astype(f32))
    f = lbf + (1.0 - lbf) * jax.nn.sigmoid(f_pre.astype(f32))
    kf = 1.0 - f
    logf = jnp.log(jnp.maximum(f, 1e-30))

    def to_chunks(a, d):
        return a.reshape(bsz, n, GLA_CHUNK, A_HEADS, d).transpose(1, 0, 3, 2, 4)

    qc = to_chunks(qf, A_KEY_DIM)
    kc = to_chunks(kf, A_KEY_DIM)
    vc = to_chunks(i_in.astype(f32), HEAD_DIM)
    bc = jnp.cumsum(to_chunks(logf, A_KEY_DIM), axis=3)
    causal = jnp.tril(jnp.ones((GLA_CHUNK, GLA_CHUNK), dtype=bool))[:, :, None]

    def step(state, inp):
        q_c, k_c, v_c, b_c = inp
        inter = jnp.einsum('bhtk,bhkv->bhtv', q_c * jnp.exp(b_c), state)
        diff = b_c[:, :, :, None, :] - b_c[:, :, None, :, :]
        decay = jnp.exp(jnp.where(causal, diff, -jnp.inf))
        scores = jnp.einsum('bhtk,bhsk,bhtsk->bhts', q_c, k_c, decay)
        intra = jnp.einsum('bhts,bhsv->bhtv', scores, v_c)
        b_end = b_c[:, :, -1:, :]
        new_state = (jnp.exp(b_end[:, :, 0, :])[..., None] * state
                     + jnp.einsum('bhsk,bhsv->bhkv', k_c * jnp.exp(b_end - b_c), v_c))
        return new_state, inter + intra

    s0 = jnp.zeros((bsz, A_HEADS, A_KEY_DIM, HEAD_DIM), f32)
    _, o = lax.scan(step, s0, (qc, kc, vc, bc))
    o = o.transpose(1, 0, 3, 2, 4).reshape(bsz, t, A_HEADS, HEAD_DIM)
    o = o * lax.rsqrt(jnp.mean(o * o, axis=-1, keepdims=True) + EPS)
    o = o.reshape(bsz, t, A_WIDTH) * norm_w.astype(f32) * jax.nn.silu(g.astype(f32))
    return o.astype(q.dtype)


def _gmlp(u, v, ln_w, ln_b, w_s, b_s):
    bsz, t, _ = u.shape
    n = t // GMLP_CHUNK
    f32 = jnp.float32
    uf = jax.nn.gelu(u.astype(f32), approximate=False)
    vf = jax.nn.gelu(v.astype(f32), approximate=False)
    mu = jnp.mean(vf, axis=-1, keepdims=True)
    var = jnp.mean(jnp.square(vf - mu), axis=-1, keepdims=True)
    vf = (vf - mu) * lax.rsqrt(var + EPS) * ln_w.astype(f32) + ln_b.astype(f32)
    vc = vf.reshape(bsz, n, GMLP_CHUNK, B_GROUPS, B_GROUP_DIM)
    tril = jnp.tril(jnp.ones((GMLP_CHUNK, GMLP_CHUNK), f32))
    w = w_s.astype(f32) * tril
    z = jnp.einsum('gts,bnsgd->bntgd', w, vc) + b_s.astype(f32).T[None, None, :, :, None]
    return (uf * z.reshape(bsz, t, B_WIDTH)).astype(u.dtype)


def _fwd_setup_inputs(seed: int = 0) -> dict:
    key = jax.random.key(seed)
    ks = jax.random.split(key, 20)
    nrm = jax.random.normal
    f32 = jnp.float32

    def gain(k, shape):
        return 1.0 + 0.01 * nrm(k, shape, f32)

    return {
        "x": nrm(ks[0], (BATCH, SEQ, D_MODEL), f32),
        "p": nrm(ks[1], (DEPTH, BATCH, SEQ, PLE_DIM), f32),
        "pre_mix_w": gain(ks[2], (DEPTH, D_MODEL)),
        "w_in": nrm(ks[3], (DEPTH, D_MODEL, IN_COLS), f32) * D_MODEL ** -0.5,
        "lb_param": nrm(ks[4], (DEPTH + 1, A_KEY_WIDTH), f32) * 0.5,
        "a_norm_w": gain(ks[5], (DEPTH, A_WIDTH)),
        "gmlp_ln_w": gain(ks[6], (DEPTH, B_WIDTH)),
        "gmlp_ln_b": 0.01 * nrm(ks[7], (DEPTH, B_WIDTH), f32),
        "w_spatial": nrm(ks[8], (DEPTH, B_GROUPS, GMLP_CHUNK, GMLP_CHUNK), f32) * GMLP_CHUNK ** -0.5,
        "b_spatial": gain(ks[9], (DEPTH, B_GROUPS, GMLP_CHUNK)),
        "w_out": nrm(ks[10], (DEPTH, MIX_WIDTH, D_MODEL), f32) * MIX_WIDTH ** -0.5,
        "post_mix_w": gain(ks[11], (DEPTH, D_MODEL)),
        "pre_ffn_w": gain(ks[12], (DEPTH, D_MODEL)),
        "w_gate": nrm(ks[13], (DEPTH, D_MODEL, D_FF), f32) * D_MODEL ** -0.5,
        "w_up": nrm(ks[14], (DEPTH, D_MODEL, D_FF), f32) * D_MODEL ** -0.5,
        "w_down": nrm(ks[15], (DEPTH, D_FF, D_MODEL), f32) * D_FF ** -0.5,
        "post_ffn_w": gain(ks[16], (DEPTH, D_MODEL)),
        "w_ple": nrm(ks[17], (DEPTH, PLE_DIM, D_MODEL), f32) * PLE_DIM ** -0.5,
        "w_ple_gate": nrm(ks[18], (DEPTH, D_MODEL, D_MODEL), f32) * D_MODEL ** -0.5,
        "post_ple_w": gain(ks[19], (DEPTH, D_MODEL)),
    }


def _fwd_reference(x, p, pre_mix_w, w_in, lb_param, a_norm_w, gmlp_ln_w, gmlp_ln_b, w_spatial, b_spatial,
              w_out, post_mix_w, pre_ffn_w, w_gate, w_up, w_down, post_ffn_w, w_ple, w_ple_gate, post_ple_w):
    lower_bounds = jnp.cumsum(jax.nn.softmax(lb_param.astype(jnp.float32), axis=0), axis=0)
    splits = [A_KEY_WIDTH,
              2 * A_KEY_WIDTH,
              2 * A_KEY_WIDTH + A_WIDTH,
              2 * A_KEY_WIDTH + 2 * A_WIDTH,
              2 * A_KEY_WIDTH + 2 * A_WIDTH + B_WIDTH]
    for l in range(DEPTH):
        h = _rmsnorm(x, pre_mix_w[l])
        proj = h @ w_in[l]
        q, f_pre, i_in, g, u, v = jnp.split(proj, splits, axis=-1)
        a_out = _hgrn2(q, f_pre, i_in, g, lower_bounds[l], a_norm_w[l])
        b_out = _gmlp(u, v, gmlp_ln_w[l], gmlp_ln_b[l], w_spatial[l], b_spatial[l])
        mix = jnp.concatenate([a_out, b_out], axis=-1) @ w_out[l]
        x = x + _rmsnorm(mix, post_mix_w[l])
        h = _rmsnorm(x, pre_ffn_w[l])
        ff = (jax.nn.silu(h @ w_gate[l]) * (h @ w_up[l])) @ w_down[l]
        x = x + _rmsnorm(ff, post_ffn_w[l])
        gate = jax.nn.sigmoid(x @ w_ple_gate[l])
        x = x + _rmsnorm((p[l] @ w_ple[l]) * gate, post_ple_w[l])
    return x


import jax as _jax
import jax.numpy as _jnp

TWIN_FORMAT = 'train_step'
FWD_PARAMS = ['x', 'p', 'pre_mix_w', 'w_in', 'lb_param', 'a_norm_w', 'gmlp_ln_w', 'gmlp_ln_b', 'w_spatial', 'b_spatial', 'w_out', 'post_mix_w', 'pre_ffn_w', 'w_gate', 'w_up', 'w_down', 'post_ffn_w', 'w_ple', 'w_ple_gate', 'post_ple_w']
TWIN_WEIGHTS = ['pre_mix_w', 'w_in', 'lb_param', 'a_norm_w', 'gmlp_ln_w', 'gmlp_ln_b', 'w_spatial', 'b_spatial', 'w_out', 'post_mix_w', 'pre_ffn_w', 'w_gate', 'w_up', 'w_down', 'post_ffn_w', 'w_ple', 'w_ple_gate', 'post_ple_w']
TWIN_DIFF_INPUT = 'x'
TWIN_INPUTS = ['x', 'p', 'pre_mix_w', 'w_in', 'lb_param', 'a_norm_w', 'gmlp_ln_w', 'gmlp_ln_b', 'w_spatial', 'b_spatial', 'w_out', 'post_mix_w', 'pre_ffn_w', 'w_gate', 'w_up', 'w_down', 'post_ffn_w', 'w_ple', 'w_ple_gate', 'post_ple_w', 'loss_target', 'm_pre_mix_w', 'm_w_in', 'm_lb_param', 'm_a_norm_w', 'm_gmlp_ln_w', 'm_gmlp_ln_b', 'm_w_spatial', 'm_b_spatial', 'm_w_out', 'm_post_mix_w', 'm_pre_ffn_w', 'm_w_gate', 'm_w_up', 'm_w_down', 'm_post_ffn_w', 'm_w_ple', 'm_w_ple_gate', 'm_post_ple_w', 'v_pre_mix_w', 'v_w_in', 'v_lb_param', 'v_a_norm_w', 'v_gmlp_ln_w', 'v_gmlp_ln_b', 'v_w_spatial', 'v_b_spatial', 'v_w_out', 'v_post_mix_w', 'v_pre_ffn_w', 'v_w_gate', 'v_w_up', 'v_w_down', 'v_post_ffn_w', 'v_w_ple', 'v_w_ple_gate', 'v_post_ple_w']
TWIN_OUTPUTS = ['loss', 'grad_x', 'grad_pre_mix_w', 'grad_w_in', 'grad_lb_param', 'grad_a_norm_w', 'grad_gmlp_ln_w', 'grad_gmlp_ln_b', 'grad_w_spatial', 'grad_b_spatial', 'grad_w_out', 'grad_post_mix_w', 'grad_pre_ffn_w', 'grad_w_gate', 'grad_w_up', 'grad_w_down', 'grad_post_ffn_w', 'grad_w_ple', 'grad_w_ple_gate', 'grad_post_ple_w', 'delta_pre_mix_w', 'delta_w_in', 'delta_lb_param', 'delta_a_norm_w', 'delta_gmlp_ln_w', 'delta_gmlp_ln_b', 'delta_w_spatial', 'delta_b_spatial', 'delta_w_out', 'delta_post_mix_w', 'delta_pre_ffn_w', 'delta_w_gate', 'delta_w_up', 'delta_w_down', 'delta_post_ffn_w', 'delta_w_ple', 'delta_w_ple_gate', 'delta_post_ple_w', 'new_m_pre_mix_w', 'new_m_w_in', 'new_m_lb_param', 'new_m_a_norm_w', 'new_m_gmlp_ln_w', 'new_m_gmlp_ln_b', 'new_m_w_spatial', 'new_m_b_spatial', 'new_m_w_out', 'new_m_post_mix_w', 'new_m_pre_ffn_w', 'new_m_w_gate', 'new_m_w_up', 'new_m_w_down', 'new_m_post_ffn_w', 'new_m_w_ple', 'new_m_w_ple_gate', 'new_m_post_ple_w', 'new_v_pre_mix_w', 'new_v_w_in', 'new_v_lb_param', 'new_v_a_norm_w', 'new_v_gmlp_ln_w', 'new_v_gmlp_ln_b', 'new_v_w_spatial', 'new_v_b_spatial', 'new_v_w_out', 'new_v_post_mix_w', 'new_v_pre_ffn_w', 'new_v_w_gate', 'new_v_w_up', 'new_v_w_down', 'new_v_post_ffn_w', 'new_v_w_ple', 'new_v_w_ple_gate', 'new_v_post_ple_w']
TWIN_LEAF_KINDS = {'loss': 'loss', 'grad_x': 'grad_x', 'grad_pre_mix_w': 'grad_w', 'grad_w_in': 'grad_w', 'grad_lb_param': 'grad_w', 'grad_a_norm_w': 'grad_w', 'grad_gmlp_ln_w': 'grad_w', 'grad_gmlp_ln_b': 'grad_w', 'grad_w_spatial': 'grad_w', 'grad_b_spatial': 'grad_w', 'grad_w_out': 'grad_w', 'grad_post_mix_w': 'grad_w', 'grad_pre_ffn_w': 'grad_w', 'grad_w_gate': 'grad_w', 'grad_w_up': 'grad_w', 'grad_w_down': 'grad_w', 'grad_post_ffn_w': 'grad_w', 'grad_w_ple': 'grad_w', 'grad_w_ple_gate': 'grad_w', 'grad_post_ple_w': 'grad_w', 'delta_pre_mix_w': 'delta_w', 'delta_w_in': 'delta_w', 'delta_lb_param': 'delta_w', 'delta_a_norm_w': 'delta_w', 'delta_gmlp_ln_w': 'delta_w', 'delta_gmlp_ln_b': 'delta_w', 'delta_w_spatial': 'delta_w', 'delta_b_spatial': 'delta_w', 'delta_w_out': 'delta_w', 'delta_post_mix_w': 'delta_w', 'delta_pre_ffn_w': 'delta_w', 'delta_w_gate': 'delta_w', 'delta_w_up': 'delta_w', 'delta_w_down': 'delta_w', 'delta_post_ffn_w': 'delta_w', 'delta_w_ple': 'delta_w', 'delta_w_ple_gate': 'delta_w', 'delta_post_ple_w': 'delta_w', 'new_m_pre_mix_w': 'new_m', 'new_m_w_in': 'new_m', 'new_m_lb_param': 'new_m', 'new_m_a_norm_w': 'new_m', 'new_m_gmlp_ln_w': 'new_m', 'new_m_gmlp_ln_b': 'new_m', 'new_m_w_spatial': 'new_m', 'new_m_b_spatial': 'new_m', 'new_m_w_out': 'new_m', 'new_m_post_mix_w': 'new_m', 'new_m_pre_ffn_w': 'new_m', 'new_m_w_gate': 'new_m', 'new_m_w_up': 'new_m', 'new_m_w_down': 'new_m', 'new_m_post_ffn_w': 'new_m', 'new_m_w_ple': 'new_m', 'new_m_w_ple_gate': 'new_m', 'new_m_post_ple_w': 'new_m', 'new_v_pre_mix_w': 'new_v', 'new_v_w_in': 'new_v', 'new_v_lb_param': 'new_v', 'new_v_a_norm_w': 'new_v', 'new_v_gmlp_ln_w': 'new_v', 'new_v_gmlp_ln_b': 'new_v', 'new_v_w_spatial': 'new_v', 'new_v_b_spatial': 'new_v', 'new_v_w_out': 'new_v', 'new_v_post_mix_w': 'new_v', 'new_v_pre_ffn_w': 'new_v', 'new_v_w_gate': 'new_v', 'new_v_w_up': 'new_v', 'new_v_w_down': 'new_v', 'new_v_post_ffn_w': 'new_v', 'new_v_w_ple': 'new_v', 'new_v_w_ple_gate': 'new_v', 'new_v_post_ple_w': 'new_v'}


def _forward(args):
    return _fwd_reference(*[args[k] for k in FWD_PARAMS])


def _output_shape():
    out = _jax.eval_shape(lambda: _forward(_fwd_setup_inputs(0)))
    return out.shape, out.dtype

N_MICROBATCH = 1
ADAM_LR = 0.001
ADAM_B1 = 0.9
ADAM_B2 = 0.999
ADAM_EPS = 1e-08
ADAM_WD = 0.01
ADAM_STEP = 10
PER_EXAMPLE_BATCH_AXIS = {'x': 0, 'p': 1, 'loss_target': 0}
SHARED_INPUTS = []
_WEIGHT_DTYPES = {'pre_mix_w': _jnp.float32, 'w_in': _jnp.float32, 'lb_param': _jnp.float32, 'a_norm_w': _jnp.float32, 'gmlp_ln_w': _jnp.float32, 'gmlp_ln_b': _jnp.float32, 'w_spatial': _jnp.float32, 'b_spatial': _jnp.float32, 'w_out': _jnp.float32, 'post_mix_w': _jnp.float32, 'pre_ffn_w': _jnp.float32, 'w_gate': _jnp.float32, 'w_up': _jnp.float32, 'w_down': _jnp.float32, 'post_ffn_w': _jnp.float32, 'w_ple': _jnp.float32, 'w_ple_gate': _jnp.float32, 'post_ple_w': _jnp.float32}
MOMENT_SCALE = {'pre_mix_w': 1.762974e-01, 'w_in': 1.023335e-01, 'lb_param': 1.037771e-02, 'a_norm_w': 1.233151e-01, 'gmlp_ln_w': 8.514783e-02, 'gmlp_ln_b': 8.473430e-02, 'w_spatial': 8.432201e-02, 'b_spatial': 1.239549e-01, 'w_out': 2.121810e-01, 'post_mix_w': 8.047315e+00, 'pre_ffn_w': 1.768917e-01, 'w_gate': 6.663029e-02, 'w_up': 9.080710e-02, 'w_down': 1.476755e-01, 'post_ffn_w': 7.977495e+00, 'w_ple': 8.848665e-02, 'w_ple_gate': 4.778077e-02, 'post_ple_w': 8.091838e+00}


def _to_microbatches(a, axis):
    t = _jnp.moveaxis(a, axis, 0)
    t = t.reshape((N_MICROBATCH, t.shape[0] // N_MICROBATCH) + t.shape[1:])
    return _jnp.moveaxis(t, 1, axis + 1)


def setup_inputs(seed: int = 0) -> dict:
    inp = _fwd_setup_inputs(seed)
    key = _jax.random.fold_in(_jax.random.key(seed), 7919)
    shape, _ = _output_shape()
    out = dict(inp)
    out["loss_target"] = _jax.random.normal(_jax.random.fold_in(key, 0), shape, _jnp.float32)
    for i, name in enumerate(TWIN_WEIGHTS):
        w = inp[name].astype(_jnp.float32)
        if MOMENT_SCALE is None:
            s = _jnp.sqrt(_jnp.mean(_jnp.square(w)) + 1e-30)
        else:
            s = MOMENT_SCALE[name]
        km, kv = _jax.random.split(_jax.random.fold_in(key, i + 1))
        out[name] = w
        out["m_" + name] = s * _jax.random.normal(km, w.shape, _jnp.float32)
        out["v_" + name] = (s * s) * _jax.random.uniform(kv, w.shape, _jnp.float32, 0.5, 1.5)
    if N_MICROBATCH > 1:
        for name, axis in PER_EXAMPLE_BATCH_AXIS.items():
            out[name] = _to_microbatches(out[name], axis)
    return {'x': out['x'], 'p': out['p'], 'pre_mix_w': out['pre_mix_w'], 'w_in': out['w_in'], 'lb_param': out['lb_param'], 'a_norm_w': out['a_norm_w'], 'gmlp_ln_w': out['gmlp_ln_w'], 'gmlp_ln_b': out['gmlp_ln_b'], 'w_spatial': out['w_spatial'], 'b_spatial': out['b_spatial'], 'w_out': out['w_out'], 'post_mix_w': out['post_mix_w'], 'pre_ffn_w': out['pre_ffn_w'], 'w_gate': out['w_gate'], 'w_up': out['w_up'], 'w_down': out['w_down'], 'post_ffn_w': out['post_ffn_w'], 'w_ple': out['w_ple'], 'w_ple_gate': out['w_ple_gate'], 'post_ple_w': out['post_ple_w'], 'loss_target': out['loss_target'], 'm_pre_mix_w': out['m_pre_mix_w'], 'm_w_in': out['m_w_in'], 'm_lb_param': out['m_lb_param'], 'm_a_norm_w': out['m_a_norm_w'], 'm_gmlp_ln_w': out['m_gmlp_ln_w'], 'm_gmlp_ln_b': out['m_gmlp_ln_b'], 'm_w_spatial': out['m_w_spatial'], 'm_b_spatial': out['m_b_spatial'], 'm_w_out': out['m_w_out'], 'm_post_mix_w': out['m_post_mix_w'], 'm_pre_ffn_w': out['m_pre_ffn_w'], 'm_w_gate': out['m_w_gate'], 'm_w_up': out['m_w_up'], 'm_w_down': out['m_w_down'], 'm_post_ffn_w': out['m_post_ffn_w'], 'm_w_ple': out['m_w_ple'], 'm_w_ple_gate': out['m_w_ple_gate'], 'm_post_ple_w': out['m_post_ple_w'], 'v_pre_mix_w': out['v_pre_mix_w'], 'v_w_in': out['v_w_in'], 'v_lb_param': out['v_lb_param'], 'v_a_norm_w': out['v_a_norm_w'], 'v_gmlp_ln_w': out['v_gmlp_ln_w'], 'v_gmlp_ln_b': out['v_gmlp_ln_b'], 'v_w_spatial': out['v_w_spatial'], 'v_b_spatial': out['v_b_spatial'], 'v_w_out': out['v_w_out'], 'v_post_mix_w': out['v_post_mix_w'], 'v_pre_ffn_w': out['v_pre_ffn_w'], 'v_w_gate': out['v_w_gate'], 'v_w_up': out['v_w_up'], 'v_w_down': out['v_w_down'], 'v_post_ffn_w': out['v_post_ffn_w'], 'v_w_ple': out['v_w_ple'], 'v_w_ple_gate': out['v_w_ple_gate'], 'v_post_ple_w': out['v_post_ple_w']}


def _loss(weights, diff, rest, loss_target):
    with _jax.named_scope("forward"):
        args = {**rest, TWIN_DIFF_INPUT: diff, **{k: w.astype(_WEIGHT_DTYPES[k]) for k, w in weights.items()}}
        y = _forward(args)
    with _jax.named_scope("loss_head"):
        err = _jnp.square(y.astype(_jnp.float32) - loss_target)
        return 0.5 * _jnp.sum(_jnp.mean(err, axis=-1)) if err.ndim else 0.5 * err


def _adamw(w, g, m, v):
    m = ADAM_B1 * m + (1.0 - ADAM_B1) * g
    v = ADAM_B2 * v + (1.0 - ADAM_B2) * _jnp.square(g)
    m_hat = m / (1.0 - ADAM_B1 ** ADAM_STEP)
    v_hat = v / (1.0 - ADAM_B2 ** ADAM_STEP)
    delta = -ADAM_LR * (m_hat / (_jnp.sqrt(v_hat) + ADAM_EPS) + ADAM_WD * w)
    return delta, m, v


def reference(x, p, pre_mix_w, w_in, lb_param, a_norm_w, gmlp_ln_w, gmlp_ln_b, w_spatial, b_spatial, w_out, post_mix_w, pre_ffn_w, w_gate, w_up, w_down, post_ffn_w, w_ple, w_ple_gate, post_ple_w, loss_target, m_pre_mix_w, m_w_in, m_lb_param, m_a_norm_w, m_gmlp_ln_w, m_gmlp_ln_b, m_w_spatial, m_b_spatial, m_w_out, m_post_mix_w, m_pre_ffn_w, m_w_gate, m_w_up, m_w_down, m_post_ffn_w, m_w_ple, m_w_ple_gate, m_post_ple_w, v_pre_mix_w, v_w_in, v_lb_param, v_a_norm_w, v_gmlp_ln_w, v_gmlp_ln_b, v_w_spatial, v_b_spatial, v_w_out, v_post_mix_w, v_pre_ffn_w, v_w_gate, v_w_up, v_w_down, v_post_ffn_w, v_w_ple, v_w_ple_gate, v_post_ple_w):
    given = dict(x=x, p=p, pre_mix_w=pre_mix_w, w_in=w_in, lb_param=lb_param, a_norm_w=a_norm_w, gmlp_ln_w=gmlp_ln_w, gmlp_ln_b=gmlp_ln_b, w_spatial=w_spatial, b_spatial=b_spatial, w_out=w_out, post_mix_w=post_mix_w, pre_ffn_w=pre_ffn_w, w_gate=w_gate, w_up=w_up, w_down=w_down, post_ffn_w=post_ffn_w, w_ple=w_ple, w_ple_gate=w_ple_gate, post_ple_w=post_ple_w, loss_target=loss_target, m_pre_mix_w=m_pre_mix_w, m_w_in=m_w_in, m_lb_param=m_lb_param, m_a_norm_w=m_a_norm_w, m_gmlp_ln_w=m_gmlp_ln_w, m_gmlp_ln_b=m_gmlp_ln_b, m_w_spatial=m_w_spatial, m_b_spatial=m_b_spatial, m_w_out=m_w_out, m_post_mix_w=m_post_mix_w, m_pre_ffn_w=m_pre_ffn_w, m_w_gate=m_w_gate, m_w_up=m_w_up, m_w_down=m_w_down, m_post_ffn_w=m_post_ffn_w, m_w_ple=m_w_ple, m_w_ple_gate=m_w_ple_gate, m_post_ple_w=m_post_ple_w, v_pre_mix_w=v_pre_mix_w, v_w_in=v_w_in, v_lb_param=v_lb_param, v_a_norm_w=v_a_norm_w, v_gmlp_ln_w=v_gmlp_ln_w, v_gmlp_ln_b=v_gmlp_ln_b, v_w_spatial=v_w_spatial, v_b_spatial=v_b_spatial, v_w_out=v_w_out, v_post_mix_w=v_post_mix_w, v_pre_ffn_w=v_pre_ffn_w, v_w_gate=v_w_gate, v_w_up=v_w_up, v_w_down=v_w_down, v_post_ffn_w=v_post_ffn_w, v_w_ple=v_w_ple, v_w_ple_gate=v_w_ple_gate, v_post_ple_w=v_post_ple_w)
    weights = {n: given[n] for n in TWIN_WEIGHTS}
    shared = {n: given[n] for n in SHARED_INPUTS}
    per_example = {n: given[n] for n in ['x', 'p']}
    grad_fn = _jax.value_and_grad(_loss, argnums=(0, 1))

    def one_microbatch(ex, loss_target):
        ex = dict(ex)
        diff = ex.pop(TWIN_DIFF_INPUT)
        return grad_fn(weights, diff, {**shared, **ex}, loss_target)

    if N_MICROBATCH == 1:
        loss, (grad_w, grad_x) = one_microbatch(per_example, given["loss_target"])
    else:
        def body(carry, xs):
            loss_sum, grad_sum = carry
            l_k, (gw_k, gx_k) = one_microbatch(xs[0], xs[1])
            with _jax.named_scope("update"):
                return (loss_sum + l_k, _jax.tree.map(_jnp.add, grad_sum, gw_k)), gx_k

        init = (_jnp.zeros((), _jnp.float32), _jax.tree.map(_jnp.zeros_like, weights))
        (loss, grad_w), grad_x = _jax.lax.scan(body, init, (per_example, given["loss_target"]))
    with _jax.named_scope("update"):
        delta_w, new_m, new_v = {}, {}, {}
        for n in TWIN_WEIGHTS:
            delta_w[n], new_m[n], new_v[n] = _adamw(weights[n], grad_w[n], given["m_" + n], given["v_" + n])
    return (loss, grad_x, *[grad_w[n] for n in TWIN_WEIGHTS], *[delta_w[n] for n in TWIN_WEIGHTS],
            *[new_m[n] for n in TWIN_WEIGHTS], *[new_v[n] for n in TWIN_WEIGHTS])
```

```python
import functools
import math

import jax
import jax.numpy as jnp
from jax import lax
from jax.experimental import pallas as pl
from jax.experimental.pallas import tpu as pltpu

F32 = jnp.float32
BF16 = jnp.bfloat16
EPS = 1e-6
HEAD = 128
GLA_CHUNK = 64
GMLP_CHUNK = 128
LANES = 128
VMEM_LIMIT = 48 * 1024 * 1024
ADAM_LR, ADAM_B1, ADAM_B2, ADAM_EPS, ADAM_WD, ADAM_STEP = 0.001, 0.9, 0.999, 1e-08, 0.01, 10
MESH = pl.DeviceIdType.MESH
HBM_SPEC = pl.BlockSpec(memory_space=pltpu.HBM)
VMEM_SPEC = pl.BlockSpec(memory_space=pltpu.VMEM)

NN = ((1,), (0,))
NT = ((1,), (1,))
TN = ((0,), (0,))


def _tile(n, pref, mult=LANES):
    if n <= pref:
        return n
    t = (pref // mult) * mult
    while t >= mult:
        if n % t == 0:
            return t
        t -= mult
    return n


def _params(sem):
    return pltpu.CompilerParams(dimension_semantics=sem, vmem_limit_bytes=VMEM_LIMIT)


def _dot(a, b, dims, precision=None):
    return lax.dot_general(a, b, (dims, ((), ())), preferred_element_type=F32, precision=precision)


def _bdot(a, b, dims):
    return _dot(a.astype(BF16), b.astype(BF16), dims)


def _fdot(a, b, dims):
    return _dot(a, b, dims, precision=lax.Precision.HIGHEST)


def _sigmoid(v):
    return 1.0 / (1.0 + jnp.exp(-v))


def _mm(name, operands, specs, out_shape, out_spec, acc_shape, grid, dims, add=None, add_spec=None):
    n_pairs = len(operands) // 2
    nk = grid[2]

    def body(*refs):
        ins = refs[:2 * n_pairs]
        r_ref = refs[2 * n_pairs] if add is not None else None
        o_ref, acc = refs[-2], refs[-1]
        k = pl.program_id(2)

        @pl.when(k == 0)
        def _():
            acc[...] = jnp.zeros_like(acc)

        part = _bdot(ins[0][...], ins[1][...], dims)
        for q in range(1, n_pairs):
            part = part + _bdot(ins[2 * q][...], ins[2 * q + 1][...], dims)
        acc[...] += part

        @pl.when(k == nk - 1)
        def _():
            r = acc[...]
            if r_ref is not None:
                r = r + r_ref[...].astype(F32)
            o_ref[...] = r.astype(o_ref.dtype)

    in_specs = list(specs) + ([add_spec] if add is not None else [])
    args = list(operands) + ([add] if add is not None else [])
    return pl.pallas_call(
        body, name=name, grid=grid, in_specs=in_specs, out_specs=out_spec, out_shape=out_shape,
        scratch_shapes=[pltpu.VMEM(acc_shape, F32)],
        compiler_params=_params(("parallel", "parallel", "arbitrary")),
    )(*args)


def _rms(v, w):
    r = lax.rsqrt(jnp.mean(v * v, axis=-1, keepdims=True) + EPS)
    return v * r * w


def _rms_bwd(v, w, dy):
    r = lax.rsqrt(jnp.mean(v * v, axis=-1, keepdims=True) + EPS)
    vh = v * r
    gy = dy * w
    dv = r * (gy - vh * jnp.mean(gy * vh, axis=-1, keepdims=True))
    return dv, jnp.sum(dy * vh, axis=0, keepdims=True)


def _rowwise(name, body, ins, in_kinds, outs, out_kinds, T, D):
    tr = _tile(T, 128, 16)
    row = pl.BlockSpec((tr, D), lambda i: (i, 0))
    vec = pl.BlockSpec((1, D), lambda i: (0, 0))
    pick = {"row": row, "vec": vec}
    return pl.pallas_call(
        body, name=name, grid=(T // tr,),
        in_specs=[pick[k] for k in in_kinds], out_specs=[pick[k] for k in out_kinds], out_shape=outs,
        compiler_params=_params(("arbitrary",)),
    )(*ins)


def _acc_vec(ref, val):
    @pl.when(pl.program_id(0) == 0)
    def _():
        ref[...] = jnp.zeros_like(ref)
    ref[...] += val


def _norm_in(x, w):
    T, D = x.shape

    def body(x_ref, w_ref, h_ref):
        h_ref[...] = _rms(x_ref[...], w_ref[...]).astype(BF16)

    return _rowwise("norm_in", body, [x, w], ["row", "vec"], [jax.ShapeDtypeStruct((T, D), BF16)], ["row"], T, D)[0]


def _mid1(x, mix, w_pm, w_pf):
    T, D = x.shape

    def body(x_ref, mix_ref, wpm_ref, wpf_ref, x1_ref, h2_ref):
        x1 = x_ref[...] + _rms(mix_ref[...], wpm_ref[...])
        x1_ref[...] = x1
        h2_ref[...] = _rms(x1, wpf_ref[...]).astype(BF16)

    return _rowwise("mid1", body, [x, mix, w_pm, w_pf], ["row", "row", "vec", "vec"],
                    [jax.ShapeDtypeStruct((T, D), F32), jax.ShapeDtypeStruct((T, D), BF16)], ["row", "row"], T, D)


def _mid2(x1, ff, w_pff):
    T, D = x1.shape

    def body(x1_ref, ff_ref, w_ref, x2_ref, x2b_ref):
        x2 = x1_ref[...] + _rms(ff_ref[...], w_ref[...])
        x2_ref[...] = x2
        x2b_ref[...] = x2.astype(BF16)

    return _rowwise("mid2", body, [x1, ff, w_pff], ["row", "row", "vec"],
                    [jax.ShapeDtypeStruct((T, D), F32), jax.ShapeDtypeStruct((T, D), BF16)], ["row", "row"], T, D)


def _head(x2, gl, e, w_pp, tgt):
    T, D = x2.shape

    def body(x2_ref, gl_ref, e_ref, w_ref, t_ref, d3_ref, de_ref, dgl_ref, dw_ref, loss_ref):
        ev = e_ref[...]
        gate = _sigmoid(gl_ref[...])
        pe = ev * gate
        w = w_ref[...]
        err = x2_ref[...] + _rms(pe, w) - t_ref[...]
        d3 = err * (1.0 / D)
        d3_ref[...] = d3
        dpe, dw = _rms_bwd(pe, w, d3)
        de_ref[...] = (dpe * gate).astype(BF16)
        dgl_ref[...] = (dpe * ev * gate * (1.0 - gate)).astype(BF16)
        _acc_vec(dw_ref, dw)
        _acc_vec(loss_ref, jnp.sum(err * err, axis=0, keepdims=True) * (0.5 / D))

    return _rowwise("head", body, [x2, gl, e, w_pp, tgt], ["row", "row", "row", "vec", "row"],
                    [jax.ShapeDtypeStruct((T, D), F32), jax.ShapeDtypeStruct((T, D), BF16),
                     jax.ShapeDtypeStruct((T, D), BF16), jax.ShapeDtypeStruct((1, D), F32),
                     jax.ShapeDtypeStruct((1, D), F32)], ["row", "row", "row", "vec", "vec"], T, D)


def _bwd_ffn_norm(dx2, ff, w_pff):
    T, D = dx2.shape

    def body(d_ref, ff_ref, w_ref, dff_ref, dw_ref):
        dff, dw = _rms_bwd(ff_ref[...], w_ref[...], d_ref[...])
        dff_ref[...] = dff.astype(BF16)
        _acc_vec(dw_ref, dw)

    return _rowwise("bwd_ffn_norm", body, [dx2, ff, w_pff], ["row", "row", "vec"],
                    [jax.ShapeDtypeStruct((T, D), BF16), jax.ShapeDtypeStruct((1, D), F32)], ["row", "vec"], T, D)


def _bwd_mid(dx2, dh2, x1, w_pf, mix, w_pm):
    T, D = dx2.shape

    def body(dx2_ref, dh2_ref, x1_ref, wpf_ref, mix_ref, wpm_ref, dx1_ref, dmix_ref, dwpf_ref, dwpm_ref):
        d1, dwpf = _rms_bwd(x1_ref[...], wpf_ref[...], dh2_ref[...])
        dx1 = dx2_ref[...] + d1
        dx1_ref[...] = dx1
        dmix, dwpm = _rms_bwd(mix_ref[...], wpm_ref[...], dx1)
        dmix_ref[...] = dmix.astype(BF16)
        _acc_vec(dwpf_ref, dwpf)
        _acc_vec(dwpm_ref, dwpm)

    return _rowwise("bwd_mid", body, [dx2, dh2, x1, w_pf, mix, w_pm], ["row", "row", "row", "vec", "row", "vec"],
                    [jax.ShapeDtypeStruct((T, D), F32), jax.ShapeDtypeStruct((T, D), BF16),
                     jax.ShapeDtypeStruct((1, D), F32), jax.ShapeDtypeStruct((1, D), F32)],
                    ["row", "row", "vec", "vec"], T, D)


def _bwd_in(dx1, dh1, x, w_pre):
    T, D = dx1.shape

    def body(dx1_ref, dh1_ref, x_ref, w_ref, gx_ref, dw_ref):
        d0, dw = _rms_bwd(x_ref[...], w_ref[...], dh1_ref[...])
        gx_ref[...] = dx1_ref[...] + d0
        _acc_vec(dw_ref, dw)

    return _rowwise("bwd_in", body, [dx1, dh1, x, w_pre], ["row", "row", "row", "vec"],
                    [jax.ShapeDtypeStruct((T, D), F32), jax.ShapeDtypeStruct((1, D), F32)], ["row", "vec"], T, D)


def _ffn_up(h2, wg, wu):
    T, D = h2.shape
    S, _, Fs = wg.shape
    tm, tk = _tile(T, 256), _tile(D, 512)
    nk = D // tk

    def body(h_ref, wg_ref, wu_ref, gt_ref, up_ref, act_ref, accg, accu):
        k = pl.program_id(2)

        @pl.when(k == 0)
        def _():
            accg[...] = jnp.zeros_like(accg)
            accu[...] = jnp.zeros_like(accu)

        h = h_ref[...]
        accg[...] += _dot(h, wg_ref[...], NN)
        accu[...] += _dot(h, wu_ref[...], NN)

        @pl.when(k == nk - 1)
        def _():
            g, u = accg[...], accu[...]
            gt_ref[...] = g
            up_ref[...] = u
            act_ref[...] = (g * _sigmoid(g) * u).astype(BF16)

    w_spec = pl.BlockSpec((None, tk, Fs), lambda i, j, k: (j, k, 0))
    o_spec = pl.BlockSpec((None, tm, Fs), lambda i, j, k: (j, i, 0))
    return pl.pallas_call(
        body, name="ffn_up", grid=(T // tm, S, nk),
        in_specs=[pl.BlockSpec((tm, tk), lambda i, j, k: (i, k)), w_spec, w_spec],
        out_specs=[o_spec, o_spec, o_spec],
        out_shape=[jax.ShapeDtypeStruct((S, T, Fs), F32), jax.ShapeDtypeStruct((S, T, Fs), F32),
                   jax.ShapeDtypeStruct((S, T, Fs), BF16)],
        scratch_shapes=[pltpu.VMEM((tm, Fs), F32), pltpu.VMEM((tm, Fs), F32)],
        compiler_params=_params(("parallel", "parallel", "arbitrary")),
    )(h2, wg, wu)


def _ffn_dact(dff, wd, gt, up):
    T, D = dff.shape
    S, Fs, _ = wd.shape
    tm, tk = _tile(T, 256), _tile(D, 512)
    nk = D // tk

    def body(d_ref, w_ref, gt_ref, up_ref, dgt_ref, dup_ref, acc):
        k = pl.program_id(2)

        @pl.when(k == 0)
        def _():
            acc[...] = jnp.zeros_like(acc)

        acc[...] += _dot(d_ref[...], w_ref[...], NT)

        @pl.when(k == nk - 1)
        def _():
            da, g, u = acc[...], gt_ref[...], up_ref[...]
            sg = _sigmoid(g)
            dgt_ref[...] = (da * u * sg * (1.0 + g * (1.0 - sg))).astype(BF16)
            dup_ref[...] = (da * g * sg).astype(BF16)

    o_spec = pl.BlockSpec((None, tm, Fs), lambda i, j, k: (j, i, 0))
    return pl.pallas_call(
        body, name="ffn_dact", grid=(T // tm, S, nk),
        in_specs=[pl.BlockSpec((tm, tk), lambda i, j, k: (i, k)),
                  pl.BlockSpec((None, Fs, tk), lambda i, j, k: (j, 0, k)), o_spec, o_spec],
        out_specs=[o_spec, o_spec],
        out_shape=[jax.ShapeDtypeStruct((S, T, Fs), BF16), jax.ShapeDtypeStruct((S, T, Fs), BF16)],
        scratch_shapes=[pltpu.VMEM((tm, Fs), F32)],
        compiler_params=_params(("parallel", "parallel", "arbitrary")),
    )(dff, wd, gt, up)


def _hgrn2_gates(q, f, lb):
    C = GLA_CHUNK
    sq = _sigmoid(q)
    qs = q * sq
    sf = _sigmoid(f)
    fg = lb + (1.0 - lb) * sf
    lf = jnp.log(jnp.maximum(fg, 1e-30))
    row = lax.broadcasted_iota(jnp.int32, (C, C), 0)
    col = lax.broadcasted_iota(jnp.int32, (C, C), 1)
    tril = row >= col
    b = _dot(jnp.where(tril, 1.0, 0.0), lf, NN, precision=lax.Precision.HIGHEST)
    bmid = b[C // 2 - 1:C // 2, :]
    bend = b[C - 1:C, :]
    ea = jnp.exp(b - bmid)
    ena = jnp.exp(bmid - b)
    eb = ea * jnp.exp(bmid)
    ee = ena * jnp.exp(bend - bmid)
    return sq, qs, sf, fg, tril, ea, ena, eb, ee, jnp.exp(bend)


def _lower_bound(lbp):
    return _sigmoid(lbp[0:1, :] - lbp[1:2, :])


def _hgrn2_fwd(proj6, lb_param, a_norm_w):
    _, T, Wd = proj6.shape
    H, C = Wd // HEAD, GLA_CHUNK
    TB = _tile(T, 512, C)
    NB, NCB = T // TB, TB // C

    def body(p_ref, lbp_ref, w_ref, cat_ref, o_ref, st_ref, S):
        @pl.when(pl.program_id(1) == 0)
        def _():
            S[...] = jnp.zeros_like(S)

        lb = _lower_bound(lbp_ref[...])
        w = w_ref[...]

        def chunk(c, carry):
            rows = pl.ds(pl.multiple_of(c * C, C), C)
            q, f, v, g = p_ref[0, rows, :], p_ref[1, rows, :], p_ref[2, rows, :], p_ref[3, rows, :]
            _, qs, _, fg, tril, ea, ena, eb, ee, ebend = _hgrn2_gates(q, f, lb)
            k = 1.0 - fg
            st = S[...]
            st_ref[0, c] = st
            p = jnp.where(tril, _bdot(qs * ea, k * ena, NT), 0.0)
            o = _bdot(qs * eb, st, NT) + _bdot(p, v, NN)
            S[...] = st * ebend + _bdot(v, k * ee, TN)
            o_ref[rows, :] = o
            r = lax.rsqrt(jnp.mean(o * o, axis=-1, keepdims=True) + EPS)
            cat_ref[rows, :] = (o * r * w * (g * _sigmoid(g))).astype(BF16)
            return carry

        lax.fori_loop(0, NCB, chunk, 0)

    return pl.pallas_call(
        body, name="hgrn2_fwd", grid=(H, NB),
        in_specs=[pl.BlockSpec((4, TB, HEAD), lambda h, t: (0, t, h)),
                  pl.BlockSpec((2, HEAD), lambda h, t: (0, h)),
                  pl.BlockSpec((1, HEAD), lambda h, t: (0, h))],
        out_specs=[pl.BlockSpec((None, TB, HEAD), lambda h, t: (0, t, h)),
                   pl.BlockSpec((TB, HEAD), lambda h, t: (t, h)),
                   pl.BlockSpec((1, NCB, HEAD, HEAD), lambda h, t: (h, t, 0, 0))],
        out_shape=[jax.ShapeDtypeStruct((2, T, Wd), BF16), jax.ShapeDtypeStruct((T, Wd), F32),
                   jax.ShapeDtypeStruct((H, T // C, HEAD, HEAD), F32)],
        scratch_shapes=[pltpu.VMEM((HEAD, HEAD), F32)],
        compiler_params=_params(("parallel", "arbitrary")),
    )(proj6, lb_param, a_norm_w)


def _hgrn2_bwd(proj6, o, states, dcat, lb_param, a_norm_w):
    _, T, Wd = proj6.shape
    H, C = Wd // HEAD, GLA_CHUNK
    TB = _tile(T, 512, C)
    NB, NCB = T // TB, TB // C

    def body(p_ref, o_ref, st_ref, da_ref, lbp_ref, w_ref, dp_ref, dw_ref, dlb_ref, dS):
        @pl.when(pl.program_id(1) == 0)
        def _():
            dS[...] = jnp.zeros_like(dS)
            dw_ref[...] = jnp.zeros_like(dw_ref)
            dlb_ref[...] = jnp.zeros_like(dlb_ref)

        lb = _lower_bound(lbp_ref[...])
        w = w_ref[...]

        def chunk(i, carry):
            c = NCB - 1 - i
            rows = pl.ds(pl.multiple_of(c * C, C), C)
            q, f, v, g = p_ref[0, rows, :], p_ref[1, rows, :], p_ref[2, rows, :], p_ref[3, rows, :]
            sq, qs, sf, fg, tril, ea, ena, eb, ee, ebend = _hgrn2_gates(q, f, lb)
            k = 1.0 - fg
            qt, kt, qb, ke = qs * ea, k * ena, qs * eb, k * ee
            st = st_ref[0, c]
            dst = dS[...]
            p = jnp.where(tril, _bdot(qt, kt, NT), 0.0)
            ov, da = o_ref[rows, :], da_ref[rows, :]
            r = lax.rsqrt(jnp.mean(ov * ov, axis=-1, keepdims=True) + EPS)
            oh = ov * r
            sg = _sigmoid(g)
            gs = g * sg
            dw_ref[...] += jnp.sum(da * oh * gs, axis=0, keepdims=True)
            dg = da * oh * w * sg * (1.0 + g * (1.0 - sg))
            doh = da * w * gs
            do = r * (doh - oh * jnp.mean(doh * oh, axis=-1, keepdims=True))
            dqb = _fdot(do, st, NN)
            dp = jnp.where(tril, _bdot(do, v, NT), 0.0)
            dv = _bdot(p, do, TN) + _bdot(ke, dst, NT)
            dqt = _fdot(dp, kt, NN)
            dkt = _fdot(dp, qt, TN)
            dke = _fdot(v, dst, NN)
            dqs = dqt * ea + dqb * eb
            dk = dkt * ena + dke * ee
            db = dqt * qt + dqb * qb - dkt * kt - dke * ke
            db_end = (jnp.sum(dke * ke, axis=0, keepdims=True)
                      + ebend * jnp.sum(dst * st, axis=0, keepdims=True))
            triu = lax.broadcasted_iota(jnp.int32, (C, C), 0) <= lax.broadcasted_iota(jnp.int32, (C, C), 1)
            dlf = _dot(jnp.where(triu, 1.0, 0.0), db, NN, precision=lax.Precision.HIGHEST) + db_end
            dfg = jnp.where(fg > 1e-30, dlf / fg, 0.0) - dk
            dlb_ref[...] += jnp.sum(dfg * (1.0 - sf), axis=0, keepdims=True)
            dp_ref[0, rows, :] = (dqs * sq * (1.0 + q * (1.0 - sq))).astype(BF16)
            dp_ref[1, rows, :] = (dfg * (1.0 - lb) * sf * (1.0 - sf)).astype(BF16)
            dp_ref[2, rows, :] = dv.astype(BF16)
            dp_ref[3, rows, :] = dg.astype(BF16)
            dS[...] = dst * ebend + _bdot(do, qb, TN)
            return carry

        lax.fori_loop(0, NCB, chunk, 0)

    rev = lambda t: NB - 1 - t
    return pl.pallas_call(
        body, name="hgrn2_bwd", grid=(H, NB),
        in_specs=[pl.BlockSpec((4, TB, HEAD), lambda h, t: (0, rev(t), h)),
                  pl.BlockSpec((TB, HEAD), lambda h, t: (rev(t), h)),
                  pl.BlockSpec((1, NCB, HEAD, HEAD), lambda h, t: (h, rev(t), 0, 0)),
                  pl.BlockSpec((None, TB, HEAD), lambda h, t: (0, rev(t), h)),
                  pl.BlockSpec((2, HEAD), lambda h, t: (0, h)),
                  pl.BlockSpec((1, HEAD), lambda h, t: (0, h))],
        out_specs=[pl.BlockSpec((4, TB, HEAD), lambda h, t: (0, rev(t), h)),
                   pl.BlockSpec((1, HEAD), lambda h, t: (0, h)),
                   pl.BlockSpec((1, HEAD), lambda h, t: (0, h))],
        out_shape=[jax.ShapeDtypeStruct((6, T, Wd), BF16), jax.ShapeDtypeStruct((1, Wd), F32),
                   jax.ShapeDtypeStruct((1, Wd), F32)],
        scratch_shapes=[pltpu.VMEM((HEAD, HEAD), F32)],
        compiler_params=_params(("parallel", "arbitrary")),
    )(proj6, o, states, dcat, lb_param, a_norm_w)


_INV_SQRT2 = 1.0 / math.sqrt(2.0)
_INV_SQRT2PI = 1.0 / math.sqrt(2.0 * math.pi)


def _gelu(v):
    return 0.5 * v * (1.0 + lax.erf(v * _INV_SQRT2))


def _gelu_grad(v):
    return 0.5 * (1.0 + lax.erf(v * _INV_SQRT2)) + v * jnp.exp(-0.5 * v * v) * _INV_SQRT2PI


def _gmlp_norm(v, ln_w, ln_b):
    vf = _gelu(v)
    mu = jnp.mean(vf, axis=-1, keepdims=True)
    cen = vf - mu
    rstd = lax.rsqrt(jnp.mean(cen * cen, axis=-1, keepdims=True) + EPS)
    vh = cen * rstd
    return vh, rstd, vh * ln_w + ln_b


def _tril_mask():
    n = GMLP_CHUNK
    return lax.broadcasted_iota(jnp.int32, (n, n), 0) >= lax.broadcasted_iota(jnp.int32, (n, n), 1)


def _gmlp_fwd(proj6, cat, ln_w, ln_b, w_s, b_st):
    _, T, Wd = proj6.shape
    G, N = Wd // HEAD, GMLP_CHUNK

    def body(p_ref, cat_in, lnw_ref, lnb_ref, ws_ref, bst_ref, out_ref):
        del cat_in
        _, _, vn = _gmlp_norm(p_ref[1], lnw_ref[...], lnb_ref[...])
        tril = _tril_mask()
        for g in range(G):
            cols = slice(g * HEAD, (g + 1) * HEAD)
            z = _bdot(jnp.where(tril, ws_ref[g], 0.0), vn[:, cols], NN) + bst_ref[:, g:g + 1]
            out_ref[:, cols] = (_gelu(p_ref[0, :, cols]) * z).astype(BF16)

    return pl.pallas_call(
        body, name="gmlp_fwd", grid=(T // N,),
        in_specs=[pl.BlockSpec((2, N, Wd), lambda t: (2, t, 0)), pl.BlockSpec(memory_space=pl.ANY),
                  pl.BlockSpec((1, Wd), lambda t: (0, 0)), pl.BlockSpec((1, Wd), lambda t: (0, 0)),
                  pl.BlockSpec((G, N, N), lambda t: (0, 0, 0)), pl.BlockSpec((N, LANES), lambda t: (0, 0))],
        out_specs=pl.BlockSpec((None, N, Wd), lambda t: (1, t, 0)),
        out_shape=jax.ShapeDtypeStruct(cat.shape, cat.dtype),
        input_output_aliases={1: 0},
        compiler_params=_params(("arbitrary",)),
    )(proj6, cat, ln_w, ln_b, w_s, b_st)


def _gmlp_bwd(proj6, dproj6, dcat, ln_w, ln_b, w_s, b_st):
    _, T, Wd = proj6.shape
    G, N = Wd // HEAD, GMLP_CHUNK

    def body(p_ref, dp_in, db_ref, lnw_ref, lnb_ref, ws_ref, bst_ref, dp_ref, dlnw_ref, dlnb_ref, dws_ref, dbst_ref, dvn):
        del dp_in

        @pl.when(pl.program_id(0) == 0)
        def _():
            dlnw_ref[...] = jnp.zeros_like(dlnw_ref)
            dlnb_ref[...] = jnp.zeros_like(dlnb_ref)
            dws_ref[...] = jnp.zeros_like(dws_ref)
            dbst_ref[...] = jnp.zeros_like(dbst_ref)

        lnw = lnw_ref[...]
        v = p_ref[1]
        vh, rstd, vn = _gmlp_norm(v, lnw, lnb_ref[...])
        tril = _tril_mask()
        lane = lax.broadcasted_iota(jnp.int32, (N, LANES), 1)
        dbst = jnp.zeros((N, LANES), F32)
        for g in range(G):
            cols = slice(g * HEAD, (g + 1) * HEAD)
            wt = jnp.where(tril, ws_ref[g], 0.0)
            vn_g = vn[:, cols]
            z = _bdot(wt, vn_g, NN) + bst_ref[:, g:g + 1]
            u = p_ref[0, :, cols]
            db = db_ref[:, cols]
            dp_ref[0, :, cols] = (db * z * _gelu_grad(u)).astype(BF16)
            dz = db * _gelu(u)
            dbst = dbst + jnp.where(lane == g, jnp.sum(dz, axis=1, keepdims=True), 0.0)
            dws_ref[g] += jnp.where(tril, _bdot(dz, vn_g, NT), 0.0)
            dvn[:, cols] = _bdot(wt, dz, TN)
        dbst_ref[...] += dbst
        dv = dvn[...]
        dlnw_ref[...] += jnp.sum(dv * vh, axis=0, keepdims=True)
        dlnb_ref[...] += jnp.sum(dv, axis=0, keepdims=True)
        dvh = dv * lnw
        dvf = rstd * (dvh - jnp.mean(dvh, axis=-1, keepdims=True) - vh * jnp.mean(dvh * vh, axis=-1, keepdims=True))
        dp_ref[1] = (dvf * _gelu_grad(v)).astype(BF16)

    vec = pl.BlockSpec((1, Wd), lambda t: (0, 0))
    return pl.pallas_call(
        body, name="gmlp_bwd", grid=(T // N,),
        in_specs=[pl.BlockSpec((2, N, Wd), lambda t: (2, t, 0)), pl.BlockSpec(memory_space=pl.ANY),
                  pl.BlockSpec((None, N, Wd), lambda t: (1, t, 0)), vec, vec,
                  pl.BlockSpec((G, N, N), lambda t: (0, 0, 0)), pl.BlockSpec((N, LANES), lambda t: (0, 0))],
        out_specs=[pl.BlockSpec((2, N, Wd), lambda t: (2, t, 0)), vec, vec,
                   pl.BlockSpec((G, N, N), lambda t: (0, 0, 0)), pl.BlockSpec((N, LANES), lambda t: (0, 0))],
        out_shape=[jax.ShapeDtypeStruct(dproj6.shape, dproj6.dtype), jax.ShapeDtypeStruct((1, Wd), F32),
                   jax.ShapeDtypeStruct((1, Wd), F32), jax.ShapeDtypeStruct((G, N, N), F32),
                   jax.ShapeDtypeStruct((N, LANES), F32)],
        scratch_shapes=[pltpu.VMEM((N, Wd), F32)],
        input_output_aliases={1: 0},
        compiler_params=_params(("arbitrary",)),
    )(proj6, dproj6, dcat, ln_w, ln_b, w_s, b_st)


def _place():
    return lax.axis_index("x"), lax.axis_index("y"), lax.axis_index("c")


def _other_chips(x, y):
    return [(1 - x, y), (x, 1 - y), (1 - x, 1 - y)]


def _rcopy(src, dst, ssem, rsem, k, to):
    return pltpu.make_async_remote_copy(src_ref=src, dst_ref=dst, send_sem=ssem.at[k], recv_sem=rsem.at[k],
                                        device_id=to, device_id_type=MESH)


def _gather_w(name, w):
    R, C = w.shape
    Rh = R // 2

    def body(w_ref, out_ref, ssem, rsem, lsem):
        x, y, c = _place()
        chips = _other_chips(x, y)

        def half(px, py, pc):
            return out_ref.at[2 * px + py, pl.ds(pc * Rh, Rh), :]

        mine = pltpu.make_async_copy(w_ref, out_ref.at[2 * x + y], lsem)
        mine.start()
        src = w_ref.at[pl.ds(c * Rh, Rh), :]
        first = [_rcopy(src, half(x, y, c), ssem, rsem, j, (px, py, c)) for j, (px, py) in enumerate(chips)]
        for cp in first:
            cp.start()
        passed = [_rcopy(half(px, py, c), half(px, py, c), ssem, rsem, 3 + j, (x, y, 1 - c))
                  for j, (px, py) in enumerate(chips)]
        for j, (px, py) in enumerate(chips):
            _rcopy(src, half(px, py, c), ssem, rsem, j, (px, py, c)).wait_recv()
            passed[j].start()
        for j, (px, py) in enumerate(chips):
            _rcopy(src, half(px, py, 1 - c), ssem, rsem, 3 + j, (x, y, 1 - c)).wait_recv()
        for cp in first + passed:
            cp.wait_send()
        mine.wait()

    return pl.pallas_call(
        body, name=name, in_specs=[HBM_SPEC], out_specs=HBM_SPEC,
        out_shape=jax.ShapeDtypeStruct((4, R, C), w.dtype),
        scratch_shapes=[pltpu.SemaphoreType.DMA((6,)), pltpu.SemaphoreType.DMA((6,)), pltpu.SemaphoreType.DMA],
    )(w)


def _swap_halves(name, g4):
    S, R, C = g4.shape
    Rh = R // 2

    def body(g_ref, land_ref, ssem, rsem):
        x, y, c = _place()
        cp = _rcopy(g_ref.at[:, pl.ds((1 - c) * Rh, Rh), :], land_ref, ssem, rsem, 0, (x, y, 1 - c))
        cp.start()
        cp.wait()

    return pl.pallas_call(
        body, name=name, in_specs=[HBM_SPEC], out_specs=HBM_SPEC,
        out_shape=jax.ShapeDtypeStruct((S, Rh, C), g4.dtype),
        scratch_shapes=[pltpu.SemaphoreType.DMA((1,)), pltpu.SemaphoreType.DMA((1,))],
    )(g4)


def _chip_exchange(name, pb):
    S, Rh, C = pb.shape

    def body(p_ref, land_ref, ssem, rsem, lsem):
        x, y, c = _place()
        chips = _other_chips(x, y)
        mine = pltpu.make_async_copy(p_ref.at[2 * x + y], land_ref.at[3], lsem)
        mine.start()
        sends = [_rcopy(p_ref.at[2 * px + py], land_ref.at[j], ssem, rsem, j, (px, py, c))
                 for j, (px, py) in enumerate(chips)]
        for cp in sends:
            cp.start()
        for cp in sends:
            cp.wait()
        mine.wait()

    return pl.pallas_call(
        body, name=name, in_specs=[HBM_SPEC], out_specs=HBM_SPEC,
        out_shape=jax.ShapeDtypeStruct((S, Rh, C), pb.dtype),
        scratch_shapes=[pltpu.SemaphoreType.DMA((3,)), pltpu.SemaphoreType.DMA((3,)), pltpu.SemaphoreType.DMA],
    )(pb)


def _share_half(name, fh):
    Rh, C = fh.shape

    def body(f_ref, out_ref, ssem, rsem, lsem):
        x, y, c = _place()
        mine = pltpu.make_async_copy(f_ref, out_ref.at[pl.ds(c * Rh, Rh), :], lsem)
        mine.start()
        cp = _rcopy(f_ref, out_ref.at[pl.ds(c * Rh, Rh), :], ssem, rsem, 0, (x, y, 1 - c))
        cp.start()
        cp.wait()
        mine.wait()

    return pl.pallas_call(
        body, name=name, in_specs=[HBM_SPEC], out_specs=HBM_SPEC,
        out_shape=jax.ShapeDtypeStruct((2 * Rh, C), fh.dtype),
        scratch_shapes=[pltpu.SemaphoreType.DMA((1,)), pltpu.SemaphoreType.DMA((1,)), pltpu.SemaphoreType.DMA],
    )(fh)


def _sum_all_devices(v):
    R, L = v.shape

    def body(v_ref, out_ref, buf, ssem, rsem):
        x, y, c = _place()
        me = 4 * x + 2 * y + c
        buf[me] = v_ref[...]
        copies = []
        for m in range(1, 8):
            to = (x ^ (m >> 2), y ^ ((m >> 1) & 1), c ^ (m & 1))
            copies.append(_rcopy(v_ref, buf.at[me], ssem, rsem, m - 1, to))
        for cp in copies:
            cp.start()
        for cp in copies:
            cp.wait()
        acc = buf[0]
        for d in range(1, 8):
            acc = acc + buf[d]
        out_ref[...] = acc

    return pl.pallas_call(
        body, name="sum_all_devices", in_specs=[VMEM_SPEC], out_specs=VMEM_SPEC,
        out_shape=jax.ShapeDtypeStruct((R, L), F32),
        scratch_shapes=[pltpu.VMEM((8, R, L), F32), pltpu.SemaphoreType.DMA((7,)), pltpu.SemaphoreType.DMA((7,))],
        compiler_params=pltpu.CompilerParams(vmem_limit_bytes=VMEM_LIMIT),
    )(v)


def _elementwise(name, body, ins, in_specs, outs, out_specs, grid):
    return pl.pallas_call(body, name=name, grid=grid, in_specs=in_specs, out_specs=out_specs, out_shape=outs,
                          compiler_params=_params(("parallel",) * len(grid)))(*ins)


def _row_block(R, C, bytes_per_row_elem=4, target=1 << 20):
    return _tile(R, max(16, target // (C * bytes_per_row_elem)), 16)


def _add_halves(name, g4, land, c_idx):
    S, R, C = g4.shape
    Rh = R // 2
    tr = _row_block(Rh, C)
    nb = Rh // tr

    def body(c_ref, g_ref, l_ref, o_ref):
        del c_ref
        o_ref[...] = (g_ref[...] + l_ref[...]).astype(BF16)

    return pl.pallas_call(
        body, name=name,
        grid_spec=pltpu.PrefetchScalarGridSpec(
            num_scalar_prefetch=1, grid=(S, nb),
            in_specs=[pl.BlockSpec((None, tr, C), lambda s, i, c: (s, c[0] * nb + i, 0)),
                      pl.BlockSpec((None, tr, C), lambda s, i, c: (s, i, 0))],
            out_specs=pl.BlockSpec((None, tr, C), lambda s, i, c: (s, i, 0))),
        out_shape=jax.ShapeDtypeStruct((S, Rh, C), BF16),
        compiler_params=_params(("parallel", "parallel")),
    )(c_idx, g4, land)


def _sum_chips(name, land):
    S, Rh, C = land.shape
    tr = _row_block(Rh, C)

    def body(l_ref, o_ref):
        acc = l_ref[0].astype(F32)
        for s in range(1, S):
            acc = acc + l_ref[s].astype(F32)
        o_ref[...] = acc

    return _elementwise(name, body, [land], [pl.BlockSpec((S, tr, C), lambda i: (0, i, 0))],
                        jax.ShapeDtypeStruct((Rh, C), F32), pl.BlockSpec((tr, C), lambda i: (i, 0)), (Rh // tr,))


def _adamw(name, w, g, m, v):
    R, C = w.shape
    tr = _row_block(R, C, target=1 << 19)
    c1 = 1.0 - ADAM_B1 ** ADAM_STEP
    c2 = 1.0 - ADAM_B2 ** ADAM_STEP

    def body(w_ref, g_ref, m_ref, v_ref, d_ref, nm_ref, nv_ref):
        gv = g_ref[...]
        nm = ADAM_B1 * m_ref[...] + (1.0 - ADAM_B1) * gv
        nv = ADAM_B2 * v_ref[...] + (1.0 - ADAM_B2) * (gv * gv)
        nm_ref[...] = nm
        nv_ref[...] = nv
        d_ref[...] = -ADAM_LR * ((nm / c1) / (jnp.sqrt(nv / c2) + ADAM_EPS) + ADAM_WD * w_ref[...])

    spec = pl.BlockSpec((tr, C), lambda i: (i, 0))
    shp = jax.ShapeDtypeStruct((R, C), F32)
    return _elementwise(name, body, [w, g, m, v], [spec] * 4, [shp] * 3, [spec] * 3, (R // tr,))


def _lb_grad(dlb, lb_param):
    def body(d_ref, p_ref, o_ref):
        lb = _lower_bound(p_ref[...])
        t = d_ref[...] * lb * (1.0 - lb)
        o_ref[0:1, :] = t
        o_ref[1:2, :] = -t

    return pl.pallas_call(body, name="lb_grad", in_specs=[VMEM_SPEC, VMEM_SPEC], out_specs=VMEM_SPEC,
                          out_shape=jax.ShapeDtypeStruct(lb_param.shape, F32))(dlb, lb_param)


def _reduce_big(tag, g4, c_idx):
    land = _swap_halves("swap_" + tag, g4)
    pb = _add_halves("addh_" + tag, g4, land, c_idx)
    got = _chip_exchange("xchg_" + tag, pb)
    fh = _sum_chips("sumc_" + tag, got)
    return _share_half("share_" + tag, fh)


def kernel(x, p, pre_mix_w, w_in, lb_param, a_norm_w, gmlp_ln_w, gmlp_ln_b, w_spatial, b_spatial, w_out, post_mix_w, pre_ffn_w, w_gate, w_up, w_down, post_ffn_w, w_ple, w_ple_gate, post_ple_w, loss_target, m_pre_mix_w, m_w_in, m_lb_param, m_a_norm_w, m_gmlp_ln_w, m_gmlp_ln_b, m_w_spatial, m_b_spatial, m_w_out, m_post_mix_w, m_pre_ffn_w, m_w_gate, m_w_up, m_w_down, m_post_ffn_w, m_w_ple, m_w_ple_gate, m_post_ple_w, v_pre_mix_w, v_w_in, v_lb_param, v_a_norm_w, v_gmlp_ln_w, v_gmlp_ln_b, v_w_spatial, v_b_spatial, v_w_out, v_post_mix_w, v_pre_ffn_w, v_w_gate, v_w_up, v_w_down, v_post_ffn_w, v_w_ple, v_w_ple_gate, v_post_ple_w):
    T, D = x.shape[1], x.shape[2]
    Wd = D // 2
    G = Wd // HEAD
    NCH = 4
    Ci = w_in.shape[2]
    Fs = w_gate.shape[2]
    Dq = w_out.shape[1]
    Pd = w_ple.shape[1]
    xs, ps, tgt = x[0], p[0, 0], loss_target[0]
    c_idx = lax.axis_index("c").astype(jnp.int32).reshape((1,))

    gw_in = _gather_w("gather_w_in", w_in[0].astype(BF16))
    gw_out = _gather_w("gather_w_out", w_out[0].astype(BF16)).reshape(D, D)
    gw_gate = _gather_w("gather_w_gate", w_gate[0].astype(BF16))
    gw_up = _gather_w("gather_w_up", w_up[0].astype(BF16))
    gw_down = _gather_w("gather_w_down", w_down[0].astype(BF16))
    gw_ple = _gather_w("gather_w_ple", w_ple[0].astype(BF16))
    gw_pg = _gather_w("gather_w_ple_gate", w_ple_gate[0].astype(BF16)).reshape(D, D)

    tm, tn, tk = _tile(T, 1024), _tile(D, 1024), _tile(D, 512)
    ts = _tile(math.gcd(Wd, Ci), 1024)
    tkt = _tile(T, 512)
    n_sec, n_shd = Wd // ts, Ci // ts
    b_st = jnp.pad(b_spatial[0].T, ((0, 0), (0, LANES - G)))

    h1 = _norm_in(xs, pre_mix_w)
    proj6 = _mm("proj", [h1, gw_in],
                [pl.BlockSpec((tm, tk), lambda i, j, k: (i, k)),
                 pl.BlockSpec((None, tk, ts), lambda i, j, k: (j // n_shd, k, j % n_shd))],
                jax.ShapeDtypeStruct((6, T, Wd), F32),
                pl.BlockSpec((None, tm, ts), lambda i, j, k: (j // n_sec, i, j % n_sec)),
                (tm, ts), (T // tm, 6 * n_sec, D // tk), NN)
    cat, o_a, states = _hgrn2_fwd(proj6, lb_param, a_norm_w)
    cat = _gmlp_fwd(proj6, cat, gmlp_ln_w, gmlp_ln_b, w_spatial[0], b_st)
    nkc = Wd // tk
    mix = _mm("mix", [cat, gw_out],
              [pl.BlockSpec((None, tm, tk), lambda i, j, k: (k // nkc, i, k % nkc)),
               pl.BlockSpec((tk, tn), lambda i, j, k: (k, j))],
              jax.ShapeDtypeStruct((T, D), F32), pl.BlockSpec((tm, tn), lambda i, j, k: (i, j)),
              (tm, tn), (T // tm, D // tn, D // tk), NN)
    x1, h2 = _mid1(xs, mix, post_mix_w, pre_ffn_w)
    gt, up, act = _ffn_up(h2, gw_gate, gw_up)
    tnf = _tile(D, 512)
    ff = _mm("ffn_down", [act, gw_down],
             [pl.BlockSpec((None, tm, Fs), lambda i, j, k: (k, i, 0)),
              pl.BlockSpec((None, Fs, tnf), lambda i, j, k: (k, 0, j))],
             jax.ShapeDtypeStruct((T, D), F32), pl.BlockSpec((tm, tnf), lambda i, j, k: (i, j)),
             (tm, tnf), (T // tm, D // tnf, NCH), NN)
    x2, x2b = _mid2(x1, ff, post_ffn_w)
    gl = _mm("ple_gate", [x2b, gw_pg],
             [pl.BlockSpec((tm, tk), lambda i, j, k: (i, k)), pl.BlockSpec((tk, tn), lambda i, j, k: (k, j))],
             jax.ShapeDtypeStruct((T, D), F32), pl.BlockSpec((tm, tn), lambda i, j, k: (i, j)),
             (tm, tn), (T // tm, D // tn, D // tk), NN)
    tq = _tile(Dq, 1024)
    nq = Dq // tq
    e = _mm("ple_embed", [ps, gw_ple],
            [pl.BlockSpec((tm, Pd), lambda i, j, k: (i, 0)),
             pl.BlockSpec((None, Pd, tq), lambda i, j, k: (j // nq, 0, j % nq))],
            jax.ShapeDtypeStruct((T, D), F32), pl.BlockSpec((tm, tq), lambda i, j, k: (i, j)),
            (tm, tq), (T // tm, D // tq, 1), NN)

    d3, de, dgl, dw_pp, loss_vec = _head(x2, gl, e, post_ple_w, tgt)
    g_ple = _mm("d_w_ple", [ps, de],
                [pl.BlockSpec((tkt, Pd), lambda i, j, k: (k, 0)), pl.BlockSpec((tkt, tq), lambda i, j, k: (k, j))],
                jax.ShapeDtypeStruct((NCH, Pd, Dq), F32),
                pl.BlockSpec((None, Pd, tq), lambda i, j, k: (j // nq, 0, j % nq)),
                (Pd, tq), (1, D // tq, T // tkt), TN)
    tmw = _tile(D, 1024)
    g_pg = _mm("d_w_ple_gate", [x2b, dgl],
               [pl.BlockSpec((tkt, tmw), lambda i, j, k: (k, i)), pl.BlockSpec((tkt, tn), lambda i, j, k: (k, j))],
               jax.ShapeDtypeStruct((D, D), F32), pl.BlockSpec((tmw, tn), lambda i, j, k: (i, j)),
               (tmw, tn), (D // tmw, D // tn, T // tkt), TN)
    dx2 = _mm("d_x2", [dgl, gw_pg],
              [pl.BlockSpec((tm, tk), lambda i, j, k: (i, k)), pl.BlockSpec((tn, tk), lambda i, j, k: (j, k))],
              jax.ShapeDtypeStruct((T, D), F32), pl.BlockSpec((tm, tn), lambda i, j, k: (i, j)),
              (tm, tn), (T // tm, D // tn, D // tk), NT,
              add=d3, add_spec=pl.BlockSpec((tm, tn), lambda i, j, k: (i, j)))
    dff, dw_pff = _bwd_ffn_norm(dx2, ff, post_ffn_w)
    g_down = _mm("d_w_down", [act, dff],
                 [pl.BlockSpec((None, tkt, Fs), lambda i, j, k: (i, k, 0)),
                  pl.BlockSpec((tkt, tnf), lambda i, j, k: (k, j))],
                 jax.ShapeDtypeStruct((NCH, Fs, D), F32), pl.BlockSpec((None, Fs, tnf), lambda i, j, k: (i, 0, j)),
                 (Fs, tnf), (NCH, D // tnf, T // tkt), TN)
    dgt, dup = _ffn_dact(dff, gw_down, gt, up)
    tmf = _tile(D, 512)
    wgrad_specs = [pl.BlockSpec((tkt, tmf), lambda i, j, k: (k, i)),
                   pl.BlockSpec((None, tkt, Fs), lambda i, j, k: (j, k, 0))]
    wgrad_out = pl.BlockSpec((None, tmf, Fs), lambda i, j, k: (j, i, 0))
    g_gate = _mm("d_w_gate", [h2, dgt], wgrad_specs, jax.ShapeDtypeStruct((NCH, D, Fs), F32), wgrad_out,
                 (tmf, Fs), (D // tmf, NCH, T // tkt), TN)
    g_up = _mm("d_w_up", [h2, dup], wgrad_specs, jax.ShapeDtypeStruct((NCH, D, Fs), F32), wgrad_out,
               (tmf, Fs), (D // tmf, NCH, T // tkt), TN)
    tmh = _tile(T, 512)
    a_sp = pl.BlockSpec((None, tmh, Fs), lambda i, j, k: (k, i, 0))
    b_sp = pl.BlockSpec((None, tnf, Fs), lambda i, j, k: (k, j, 0))
    dh2 = _mm("d_h2", [dgt, gw_gate, dup, gw_up], [a_sp, b_sp, a_sp, b_sp],
              jax.ShapeDtypeStruct((T, D), F32), pl.BlockSpec((tmh, tnf), lambda i, j, k: (i, j)),
              (tmh, tnf), (T // tmh, D // tnf, NCH), NT)
    dx1, dmix, dw_pf, dw_pm = _bwd_mid(dx2, dh2, x1, pre_ffn_w, mix, post_mix_w)
    tw = _tile(Wd, 1024)
    nw = Wd // tw
    g_out = _mm("d_w_out", [cat, dmix],
                [pl.BlockSpec((None, tkt, tw), lambda i, j, k: (i // nw, k, i % nw)),
                 pl.BlockSpec((tkt, tn), lambda i, j, k: (k, j))],
                jax.ShapeDtypeStruct((D, D), F32), pl.BlockSpec((tw, tn), lambda i, j, k: (i, j)),
                (tw, tn), (D // tw, D // tn, T // tkt), TN)
    dcat = _mm("d_cat", [dmix, gw_out],
               [pl.BlockSpec((tm, tk), lambda i, j, k: (i, k)), pl.BlockSpec((tw, tk), lambda i, j, k: (j, k))],
               jax.ShapeDtypeStruct((2, T, Wd), F32), pl.BlockSpec((None, tm, tw), lambda i, j, k: (j // nw, i, j % nw)),
               (tm, tw), (T // tm, D // tw, D // tk), NT)
    dproj6, dw_an, dlb = _hgrn2_bwd(proj6, o_a, states, dcat, lb_param, a_norm_w)
    dproj6, dw_lnw, dw_lnb, dw_sp, dw_bst = _gmlp_bwd(proj6, dproj6, dcat, gmlp_ln_w, gmlp_ln_b, w_spatial[0], b_st)
    g_in = _mm("d_w_in", [h1, dproj6],
               [pl.BlockSpec((tkt, tmw), lambda i, j, k: (k, i)),
                pl.BlockSpec((None, tkt, ts), lambda i, j, k: (j // n_sec, k, j % n_sec))],
               jax.ShapeDtypeStruct((NCH, D, Ci), F32),
               pl.BlockSpec((None, tmw, ts), lambda i, j, k: (j // n_shd, i, j % n_shd)),
               (tmw, ts), (D // tmw, 6 * n_sec, T // tkt), TN)
    dh1 = _mm("d_h1", [dproj6, gw_in],
              [pl.BlockSpec((None, tm, ts), lambda i, j, k: (k // n_sec, i, k % n_sec)),
               pl.BlockSpec((None, tn, ts), lambda i, j, k: (k // n_shd, j, k % n_shd))],
              jax.ShapeDtypeStruct((T, D), F32), pl.BlockSpec((tm, tn), lambda i, j, k: (i, j)),
              (tm, tn), (T // tm, D // tn, 6 * n_sec), NT)
    grad_x, dw_pre = _bwd_in(dx1, dh1, xs, pre_mix_w)

    small = [("pre_mix_w", dw_pre), ("lb", dlb), ("a_norm_w", dw_an), ("gmlp_ln_w", dw_lnw), ("gmlp_ln_b", dw_lnb),
             ("w_spatial", dw_sp), ("b_spatial_t", dw_bst), ("post_mix_w", dw_pm), ("pre_ffn_w", dw_pf),
             ("post_ffn_w", dw_pff), ("post_ple_w", dw_pp), ("loss", loss_vec)]
    rows = [a.size // LANES for _, a in small]
    total = sum(rows)
    pad = (-total) % 8
    packed = jnp.concatenate([a.reshape(-1, LANES) for _, a in small] + [jnp.zeros((pad, LANES), F32)], axis=0)
    summed = _sum_all_devices(packed)
    off, piece = 0, {}
    for (nm, a), r in zip(small, rows):
        piece[nm] = summed[off:off + r].reshape(a.shape)
        off += r
    loss = jnp.sum(piece["loss"])
    g_small = {
        "pre_mix_w": piece["pre_mix_w"], "lb_param": _lb_grad(piece["lb"], lb_param), "a_norm_w": piece["a_norm_w"],
        "gmlp_ln_w": piece["gmlp_ln_w"], "gmlp_ln_b": piece["gmlp_ln_b"], "w_spatial": piece["w_spatial"][None],
        "b_spatial": piece["b_spatial_t"][:, :G].T[None], "post_mix_w": piece["post_mix_w"],
        "pre_ffn_w": piece["pre_ffn_w"], "post_ffn_w": piece["post_ffn_w"], "post_ple_w": piece["post_ple_w"],
    }
    w_small = dict(pre_mix_w=(pre_mix_w, m_pre_mix_w, v_pre_mix_w), lb_param=(lb_param, m_lb_param, v_lb_param),
                   a_norm_w=(a_norm_w, m_a_norm_w, v_a_norm_w), gmlp_ln_w=(gmlp_ln_w, m_gmlp_ln_w, v_gmlp_ln_w),
                   gmlp_ln_b=(gmlp_ln_b, m_gmlp_ln_b, v_gmlp_ln_b), w_spatial=(w_spatial, m_w_spatial, v_w_spatial),
                   b_spatial=(b_spatial, m_b_spatial, v_b_spatial), post_mix_w=(post_mix_w, m_post_mix_w, v_post_mix_w),
                   pre_ffn_w=(pre_ffn_w, m_pre_ffn_w, v_pre_ffn_w), post_ffn_w=(post_ffn_w, m_post_ffn_w, v_post_ffn_w),
                   post_ple_w=(post_ple_w, m_post_ple_w, v_post_ple_w))
    names_small = list(w_small)

    def pack(arrs):
        flat = jnp.concatenate([a.reshape(-1, LANES) for a in arrs], axis=0)
        return jnp.pad(flat, ((0, (-flat.shape[0]) % 16), (0, 0)))

    pk = [pack([w_small[n][i] for n in names_small]) for i in range(3)]
    pg = pack([g_small[n] for n in names_small])
    sd, sm, sv = _adamw("adamw_small", pk[0], pg, pk[1], pk[2])
    res = {}
    off = 0
    for n in names_small:
        shp = w_small[n][0].shape
        r = w_small[n][0].size // LANES
        res[n] = (g_small[n].reshape(shp), sd[off:off + r].reshape(shp), sm[off:off + r].reshape(shp),
                  sv[off:off + r].reshape(shp))
        off += r

    big = dict(w_in=(g_in, w_in, m_w_in, v_w_in), w_out=(g_out.reshape(NCH, Dq, D), w_out, m_w_out, v_w_out),
               w_gate=(g_gate, w_gate, m_w_gate, v_w_gate), w_up=(g_up, w_up, m_w_up, v_w_up),
               w_down=(g_down, w_down, m_w_down, v_w_down), w_ple=(g_ple, w_ple, m_w_ple, v_w_ple),
               w_ple_gate=(g_pg.reshape(NCH, Dq, D), w_ple_gate, m_w_ple_gate, v_w_ple_gate))
    for n, (g4, w, m, v) in big.items():
        g = _reduce_big(n, g4, c_idx)
        dl, nm_, nv_ = _adamw("adamw_" + n, w[0], g, m[0], v[0])
        res[n] = (g[None], dl[None], nm_[None], nv_[None])

    order = ["pre_mix_w", "w_in", "lb_param", "a_norm_w", "gmlp_ln_w", "gmlp_ln_b", "w_spatial", "b_spatial", "w_out",
             "post_mix_w", "pre_ffn_w", "w_gate", "w_up", "w_down", "post_ffn_w", "w_ple", "w_ple_gate", "post_ple_w"]
    return (loss, grad_x[None], *[res[n][0] for n in order], *[res[n][1] for n in order],
            *[res[n][2] for n in order], *[res[n][3] for n in order])
```

```python
import functools
import math

import jax
import jax.numpy as jnp
from jax import lax
from jax.experimental import pallas as pl
from jax.experimental.pallas import tpu as pltpu

F32 = jnp.float32
BF16 = jnp.bfloat16
EPS = 1e-6
HEAD = 128
GLA_CHUNK = 64
GMLP_CHUNK = 128
LANES = 128
VMEM_LIMIT = 48 * 1024 * 1024
ADAM_LR, ADAM_B1, ADAM_B2, ADAM_EPS, ADAM_WD, ADAM_STEP = 0.001, 0.9, 0.999, 1e-08, 0.01, 10
MESH = pl.DeviceIdType.MESH
HBM_SPEC = pl.BlockSpec(memory_space=pltpu.HBM)
VMEM_SPEC = pl.BlockSpec(memory_space=pltpu.VMEM)
SEM_SPEC = pl.BlockSpec(memory_space=pltpu.SEMAPHORE)
SIDE_EFFECT = pltpu.SideEffectType.DATAFLOW_SIDE_EFFECTING

NN = ((1,), (0,))
NT = ((1,), (1,))
TN = ((0,), (0,))


def _tile(n, pref, mult=LANES):
    if n <= pref:
        return n
    t = (pref // mult) * mult
    while t >= mult:
        if n % t == 0:
            return t
        t -= mult
    return n


def _params(sem):
    return pltpu.CompilerParams(dimension_semantics=sem, vmem_limit_bytes=VMEM_LIMIT)


class _Order:
    last = None


def _pcall(body, args, *, token=0, prefetch=0, grid=(), in_specs, out_specs, scratch_shapes=(), same=(), **kw):
    args, in_specs = list(args), list(in_specs)
    n = len(args)
    run = body
    if _Order.last is not None and not any(_Order.last is a for a in [*args, *same]):
        def run(*refs):
            return body(*refs[:n], *refs[n + 1:])
        args.append(_Order.last)
        in_specs.append(pl.BlockSpec(memory_space=pl.ANY))
    if prefetch:
        kw["grid_spec"] = pltpu.PrefetchScalarGridSpec(num_scalar_prefetch=prefetch, grid=grid, in_specs=in_specs,
                                                       out_specs=out_specs, scratch_shapes=list(scratch_shapes))
    else:
        kw.update(grid=grid, in_specs=in_specs, out_specs=out_specs, scratch_shapes=list(scratch_shapes))
    outs = pl.pallas_call(run, **kw)(*args)
    _Order.last = outs[token] if isinstance(outs, (tuple, list)) else outs
    return outs


def _dot(a, b, dims, precision=None):
    return lax.dot_general(a, b, (dims, ((), ())), preferred_element_type=F32, precision=precision)


def _bdot(a, b, dims):
    return _dot(a.astype(BF16), b.astype(BF16), dims)


def _fdot(a, b, dims):
    return _dot(a, b, dims, precision=lax.Precision.HIGHEST)


def _sigmoid(v):
    return 1.0 / (1.0 + jnp.exp(-v))


def _mm(name, operands, specs, out_shape, out_spec, acc_shape, grid, dims, add=None, add_spec=None):
    n_pairs = len(operands) // 2
    nk = grid[2]

    def body(*refs):
        ins = refs[:2 * n_pairs]
        r_ref = refs[2 * n_pairs] if add is not None else None
        o_ref, acc = refs[-2], refs[-1]
        k = pl.program_id(2)

        @pl.when(k == 0)
        def _():
            acc[...] = jnp.zeros_like(acc)

        part = _bdot(ins[0][...], ins[1][...], dims)
        for q in range(1, n_pairs):
            part = part + _bdot(ins[2 * q][...], ins[2 * q + 1][...], dims)
        acc[...] += part

        @pl.when(k == nk - 1)
        def _():
            r = acc[...]
            if r_ref is not None:
                r = r + r_ref[...].astype(F32)
            o_ref[...] = r.astype(o_ref.dtype)

    in_specs = list(specs) + ([add_spec] if add is not None else [])
    args = list(operands) + ([add] if add is not None else [])
    return _pcall(
        body, args, name=name, grid=grid, in_specs=in_specs, out_specs=out_spec, out_shape=out_shape,
        scratch_shapes=[pltpu.VMEM(acc_shape, F32)],
        compiler_params=_params(("parallel", "parallel", "arbitrary")),
    )


def _rms(v, w):
    r = lax.rsqrt(jnp.mean(v * v, axis=-1, keepdims=True) + EPS)
    return v * r * w


def _rms_bwd(v, w, dy):
    r = lax.rsqrt(jnp.mean(v * v, axis=-1, keepdims=True) + EPS)
    vh = v * r
    gy = dy * w
    dv = r * (gy - vh * jnp.mean(gy * vh, axis=-1, keepdims=True))
    return dv, jnp.sum(dy * vh, axis=0, keepdims=True)


def _rowwise(name, body, ins, in_kinds, outs, out_kinds, T, D):
    tr = _tile(T, 128, 16)
    row = pl.BlockSpec((tr, D), lambda i: (i, 0))
    vec = pl.BlockSpec((1, D), lambda i: (0, 0))
    pick = {"row": row, "vec": vec}
    return _pcall(
        body, ins, name=name, grid=(T // tr,),
        in_specs=[pick[k] for k in in_kinds], out_specs=[pick[k] for k in out_kinds], out_shape=outs,
        compiler_params=_params(("arbitrary",)),
    )


def _acc_vec(ref, val):
    @pl.when(pl.program_id(0) == 0)
    def _():
        ref[...] = jnp.zeros_like(ref)
    ref[...] += val


def _norm_in(x, w):
    T, D = x.shape

    def body(x_ref, w_ref, h_ref):
        h_ref[...] = _rms(x_ref[...], w_ref[...]).astype(BF16)

    return _rowwise("norm_in", body, [x, w], ["row", "vec"], [jax.ShapeDtypeStruct((T, D), BF16)], ["row"], T, D)[0]


def _mid1(x, mix, w_pm, w_pf):
    T, D = x.shape

    def body(x_ref, mix_ref, wpm_ref, wpf_ref, x1_ref, h2_ref):
        x1 = x_ref[...] + _rms(mix_ref[...], wpm_ref[...])
        x1_ref[...] = x1
        h2_ref[...] = _rms(x1, wpf_ref[...]).astype(BF16)

    return _rowwise("mid1", body, [x, mix, w_pm, w_pf], ["row", "row", "vec", "vec"],
                    [jax.ShapeDtypeStruct((T, D), F32), jax.ShapeDtypeStruct((T, D), BF16)], ["row", "row"], T, D)


def _mid2(x1, ff, w_pff):
    T, D = x1.shape

    def body(x1_ref, ff_ref, w_ref, x2_ref, x2b_ref):
        x2 = x1_ref[...] + _rms(ff_ref[...], w_ref[...])
        x2_ref[...] = x2
        x2b_ref[...] = x2.astype(BF16)

    return _rowwise("mid2", body, [x1, ff, w_pff], ["row", "row", "vec"],
                    [jax.ShapeDtypeStruct((T, D), F32), jax.ShapeDtypeStruct((T, D), BF16)], ["row", "row"], T, D)


def _head(x2, gl, e, w_pp, tgt):
    T, D = x2.shape

    def body(x2_ref, gl_ref, e_ref, w_ref, t_ref, d3_ref, de_ref, dgl_ref, dw_ref, loss_ref):
        ev = e_ref[...]
        gate = _sigmoid(gl_ref[...])
        pe = ev * gate
        w = w_ref[...]
        err = x2_ref[...] + _rms(pe, w) - t_ref[...]
        d3 = err * (1.0 / D)
        d3_ref[...] = d3
        dpe, dw = _rms_bwd(pe, w, d3)
        de_ref[...] = (dpe * gate).astype(BF16)
        dgl_ref[...] = (dpe * ev * gate * (1.0 - gate)).astype(BF16)
        _acc_vec(dw_ref, dw)
        _acc_vec(loss_ref, jnp.sum(err * err, axis=0, keepdims=True) * (0.5 / D))

    return _rowwise("head", body, [x2, gl, e, w_pp, tgt], ["row", "row", "row", "vec", "row"],
                    [jax.ShapeDtypeStruct((T, D), F32), jax.ShapeDtypeStruct((T, D), BF16),
                     jax.ShapeDtypeStruct((T, D), BF16), jax.ShapeDtypeStruct((1, D), F32),
                     jax.ShapeDtypeStruct((1, D), F32)], ["row", "row", "row", "vec", "vec"], T, D)


def _bwd_ffn_norm(dx2, ff, w_pff):
    T, D = dx2.shape

    def body(d_ref, ff_ref, w_ref, dff_ref, dw_ref):
        dff, dw = _rms_bwd(ff_ref[...], w_ref[...], d_ref[...])
        dff_ref[...] = dff.astype(BF16)
        _acc_vec(dw_ref, dw)

    return _rowwise("bwd_ffn_norm", body, [dx2, ff, w_pff], ["row", "row", "vec"],
                    [jax.ShapeDtypeStruct((T, D), BF16), jax.ShapeDtypeStruct((1, D), F32)], ["row", "vec"], T, D)


def _bwd_mid(dx2, dh2, x1, w_pf, mix, w_pm):
    T, D = dx2.shape

    def body(dx2_ref, dh2_ref, x1_ref, wpf_ref, mix_ref, wpm_ref, dx1_ref, dmix_ref, dwpf_ref, dwpm_ref):
        d1, dwpf = _rms_bwd(x1_ref[...], wpf_ref[...], dh2_ref[...])
        dx1 = dx2_ref[...] + d1
        dx1_ref[...] = dx1
        dmix, dwpm = _rms_bwd(mix_ref[...], wpm_ref[...], dx1)
        dmix_ref[...] = dmix.astype(BF16)
        _acc_vec(dwpf_ref, dwpf)
        _acc_vec(dwpm_ref, dwpm)

    return _rowwise("bwd_mid", body, [dx2, dh2, x1, w_pf, mix, w_pm], ["row", "row", "row", "vec", "row", "vec"],
                    [jax.ShapeDtypeStruct((T, D), F32), jax.ShapeDtypeStruct((T, D), BF16),
                     jax.ShapeDtypeStruct((1, D), F32), jax.ShapeDtypeStruct((1, D), F32)],
                    ["row", "row", "vec", "vec"], T, D)


def _bwd_in(dx1, dh1, x, w_pre):
    T, D = dx1.shape

    def body(dx1_ref, dh1_ref, x_ref, w_ref, gx_ref, dw_ref):
        d0, dw = _rms_bwd(x_ref[...], w_ref[...], dh1_ref[...])
        gx_ref[...] = dx1_ref[...] + d0
        _acc_vec(dw_ref, dw)

    return _rowwise("bwd_in", body, [dx1, dh1, x, w_pre], ["row", "row", "row", "vec"],
                    [jax.ShapeDtypeStruct((T, D), F32), jax.ShapeDtypeStruct((1, D), F32)], ["row", "vec"], T, D)


def _ffn_up(h2, wg, wu):
    T, D = h2.shape
    S, _, Fs = wg.shape
    tm, tk = _tile(T, 512), _tile(D, 512)
    nk = D // tk

    def body(h_ref, wg_ref, wu_ref, gt_ref, up_ref, act_ref, accg, accu):
        k = pl.program_id(2)

        @pl.when(k == 0)
        def _():
            accg[...] = jnp.zeros_like(accg)
            accu[...] = jnp.zeros_like(accu)

        h = h_ref[...]
        accg[...] += _dot(h, wg_ref[...], NN)
        accu[...] += _dot(h, wu_ref[...], NN)

        @pl.when(k == nk - 1)
        def _():
            g, u = accg[...], accu[...]
            gt_ref[...] = g.astype(BF16)
            up_ref[...] = u.astype(BF16)
            act_ref[...] = (g * _sigmoid(g) * u).astype(BF16)

    w_spec = pl.BlockSpec((None, tk, Fs), lambda i, j, k: (j, k, 0))
    o_spec = pl.BlockSpec((None, tm, Fs), lambda i, j, k: (j, i, 0))
    return _pcall(
        body, [h2, wg, wu], name="ffn_up", grid=(T // tm, S, nk),
        in_specs=[pl.BlockSpec((tm, tk), lambda i, j, k: (i, k)), w_spec, w_spec],
        out_specs=[o_spec, o_spec, o_spec],
        out_shape=[jax.ShapeDtypeStruct((S, T, Fs), BF16)] * 3,
        scratch_shapes=[pltpu.VMEM((tm, Fs), F32), pltpu.VMEM((tm, Fs), F32)],
        compiler_params=_params(("parallel", "parallel", "arbitrary")),
    )


def _ffn_dact(dff, wd, gt, up):
    T, D = dff.shape
    S, Fs, _ = wd.shape
    tm, tk = _tile(T, 512), _tile(D, 512)
    nk = D // tk

    def body(d_ref, w_ref, gt_ref, up_ref, dgt_ref, dup_ref, acc):
        k = pl.program_id(2)

        @pl.when(k == 0)
        def _():
            acc[...] = jnp.zeros_like(acc)

        acc[...] += _dot(d_ref[...], w_ref[...], NT)

        @pl.when(k == nk - 1)
        def _():
            da, g, u = acc[...], gt_ref[...].astype(F32), up_ref[...].astype(F32)
            sg = _sigmoid(g)
            dgt_ref[...] = (da * u * sg * (1.0 + g * (1.0 - sg))).astype(BF16)
            dup_ref[...] = (da * g * sg).astype(BF16)

    o_spec = pl.BlockSpec((None, tm, Fs), lambda i, j, k: (j, i, 0))
    return _pcall(
        body, [dff, wd, gt, up], name="ffn_dact", grid=(T // tm, S, nk),
        in_specs=[pl.BlockSpec((tm, tk), lambda i, j, k: (i, k)),
                  pl.BlockSpec((None, Fs, tk), lambda i, j, k: (j, 0, k)), o_spec, o_spec],
        out_specs=[o_spec, o_spec],
        out_shape=[jax.ShapeDtypeStruct((S, T, Fs), BF16), jax.ShapeDtypeStruct((S, T, Fs), BF16)],
        scratch_shapes=[pltpu.VMEM((tm, Fs), F32)],
        compiler_params=_params(("parallel", "parallel", "arbitrary")),
    )


def _hgrn2_gates(q, f, lb):
    C = GLA_CHUNK
    sq = _sigmoid(q)
    qs = q * sq
    sf = _sigmoid(f)
    fg = lb + (1.0 - lb) * sf
    lf = jnp.log(jnp.maximum(fg, 1e-30))
    row = lax.broadcasted_iota(jnp.int32, (C, C), 0)
    col = lax.broadcasted_iota(jnp.int32, (C, C), 1)
    tril = row >= col
    b = _dot(jnp.where(tril, 1.0, 0.0), lf, NN, precision=lax.Precision.HIGHEST)
    bmid = b[C // 2 - 1:C // 2, :]
    bend = b[C - 1:C, :]
    ea = jnp.exp(b - bmid)
    ena = jnp.exp(bmid - b)
    eb = ea * jnp.exp(bmid)
    ee = ena * jnp.exp(bend - bmid)
    return sq, qs, sf, fg, tril, ea, ena, eb, ee, jnp.exp(bend)


def _lower_bound(lbp):
    return _sigmoid(lbp[0:1, :] - lbp[1:2, :])


def _hgrn2_fwd(proj6, lb_param, a_norm_w):
    _, T, Wd = proj6.shape
    H, C = Wd // HEAD, GLA_CHUNK
    TB = _tile(T, 512, C)
    NB, NCB = T // TB, TB // C

    def body(p_ref, lbp_ref, w_ref, cat_ref, o_ref, st_ref, S):
        @pl.when(pl.program_id(1) == 0)
        def _():
            S[...] = jnp.zeros_like(S)

        lb = _lower_bound(lbp_ref[...])
        w = w_ref[...]

        def chunk(c, carry):
            rows = pl.ds(pl.multiple_of(c * C, C), C)
            q, f, v, g = p_ref[0, rows, :], p_ref[1, rows, :], p_ref[2, rows, :], p_ref[3, rows, :]
            _, qs, _, fg, tril, ea, ena, eb, ee, ebend = _hgrn2_gates(q, f, lb)
            k = 1.0 - fg
            st = S[...]
            st_ref[0, c] = st
            p = jnp.where(tril, _bdot(qs * ea, k * ena, NT), 0.0)
            o = _bdot(qs * eb, st, NT) + _bdot(p, v, NN)
            S[...] = st * ebend + _bdot(v, k * ee, TN)
            o_ref[rows, :] = o
            r = lax.rsqrt(jnp.mean(o * o, axis=-1, keepdims=True) + EPS)
            cat_ref[rows, :] = (o * r * w * (g * _sigmoid(g))).astype(BF16)
            return carry

        lax.fori_loop(0, NCB, chunk, 0)

    return _pcall(
        body, [proj6, lb_param, a_norm_w], name="hgrn2_fwd", grid=(H, NB),
        in_specs=[pl.BlockSpec((4, TB, HEAD), lambda h, t: (0, t, h)),
                  pl.BlockSpec((2, HEAD), lambda h, t: (0, h)),
                  pl.BlockSpec((1, HEAD), lambda h, t: (0, h))],
        out_specs=[pl.BlockSpec((None, TB, HEAD), lambda h, t: (0, t, h)),
                   pl.BlockSpec((TB, HEAD), lambda h, t: (t, h)),
                   pl.BlockSpec((1, NCB, HEAD, HEAD), lambda h, t: (h, t, 0, 0))],
        out_shape=[jax.ShapeDtypeStruct((2, T, Wd), BF16), jax.ShapeDtypeStruct((T, Wd), F32),
                   jax.ShapeDtypeStruct((H, T // C, HEAD, HEAD), F32)],
        scratch_shapes=[pltpu.VMEM((HEAD, HEAD), F32)],
        compiler_params=_params(("parallel", "arbitrary")),
    )


def _hgrn2_bwd(proj6, o, states, dcat, lb_param, a_norm_w):
    _, T, Wd = proj6.shape
    H, C = Wd // HEAD, GLA_CHUNK
    TB = _tile(T, 512, C)
    NB, NCB = T // TB, TB // C

    def body(p_ref, o_ref, st_ref, da_ref, lbp_ref, w_ref, dp_ref, dw_ref, dlb_ref, dS):
        @pl.when(pl.program_id(1) == 0)
        def _():
            dS[...] = jnp.zeros_like(dS)
            dw_ref[...] = jnp.zeros_like(dw_ref)
            dlb_ref[...] = jnp.zeros_like(dlb_ref)

        lb = _lower_bound(lbp_ref[...])
        w = w_ref[...]

        def chunk(i, carry):
            c = NCB - 1 - i
            rows = pl.ds(pl.multiple_of(c * C, C), C)
            q, f, v, g = p_ref[0, rows, :], p_ref[1, rows, :], p_ref[2, rows, :], p_ref[3, rows, :]
            sq, qs, sf, fg, tril, ea, ena, eb, ee, ebend = _hgrn2_gates(q, f, lb)
            k = 1.0 - fg
            qt, kt, qb, ke = qs * ea, k * ena, qs * eb, k * ee
            st = st_ref[0, c]
            dst = dS[...]
            p = jnp.where(tril, _bdot(qt, kt, NT), 0.0)
            ov, da = o_ref[rows, :], da_ref[rows, :]
            r = lax.rsqrt(jnp.mean(ov * ov, axis=-1, keepdims=True) + EPS)
            oh = ov * r
            sg = _sigmoid(g)
            gs = g * sg
            dw_ref[...] += jnp.sum(da * oh * gs, axis=0, keepdims=True)
            dg = da * oh * w * sg * (1.0 + g * (1.0 - sg))
            doh = da * w * gs
            do = r * (doh - oh * jnp.mean(doh * oh, axis=-1, keepdims=True))
            dqb = _fdot(do, st, NN)
            dp = jnp.where(tril, _bdot(do, v, NT), 0.0)
            dv = _bdot(p, do, TN) + _bdot(ke, dst, NT)
            dqt = _fdot(dp, kt, NN)
            dkt = _fdot(dp, qt, TN)
            dke = _fdot(v, dst, NN)
            dqs = dqt * ea + dqb * eb
            dk = dkt * ena + dke * ee
            db = dqt * qt + dqb * qb - dkt * kt - dke * ke
            db_end = (jnp.sum(dke * ke, axis=0, keepdims=True)
                      + ebend * jnp.sum(dst * st, axis=0, keepdims=True))
            triu = lax.broadcasted_iota(jnp.int32, (C, C), 0) <= lax.broadcasted_iota(jnp.int32, (C, C), 1)
            dlf = _dot(jnp.where(triu, 1.0, 0.0), db, NN, precision=lax.Precision.HIGHEST) + db_end
            dfg = jnp.where(fg > 1e-30, dlf / fg, 0.0) - dk
            dlb_ref[...] += jnp.sum(dfg * (1.0 - sf), axis=0, keepdims=True)
            dp_ref[0, rows, :] = (dqs * sq * (1.0 + q * (1.0 - sq))).astype(BF16)
            dp_ref[1, rows, :] = (dfg * (1.0 - lb) * sf * (1.0 - sf)).astype(BF16)
            dp_ref[2, rows, :] = dv.astype(BF16)
            dp_ref[3, rows, :] = dg.astype(BF16)
            dS[...] = dst * ebend + _bdot(do, qb, TN)
            return carry

        lax.fori_loop(0, NCB, chunk, 0)

    rev = lambda t: NB - 1 - t
    return _pcall(
        body, [proj6, o, states, dcat, lb_param, a_norm_w], name="hgrn2_bwd", grid=(H, NB),
        in_specs=[pl.BlockSpec((4, TB, HEAD), lambda h, t: (0, rev(t), h)),
                  pl.BlockSpec((TB, HEAD), lambda h, t: (rev(t), h)),
                  pl.BlockSpec((1, NCB, HEAD, HEAD), lambda h, t: (h, rev(t), 0, 0)),
                  pl.BlockSpec((None, TB, HEAD), lambda h, t: (0, rev(t), h)),
                  pl.BlockSpec((2, HEAD), lambda h, t: (0, h)),
                  pl.BlockSpec((1, HEAD), lambda h, t: (0, h))],
        out_specs=[pl.BlockSpec((4, TB, HEAD), lambda h, t: (0, rev(t), h)),
                   pl.BlockSpec((1, HEAD), lambda h, t: (0, h)),
                   pl.BlockSpec((1, HEAD), lambda h, t: (0, h))],
        out_shape=[jax.ShapeDtypeStruct((6, T, Wd), BF16), jax.ShapeDtypeStruct((1, Wd), F32),
                   jax.ShapeDtypeStruct((1, Wd), F32)],
        scratch_shapes=[pltpu.VMEM((HEAD, HEAD), F32)],
        compiler_params=_params(("parallel", "arbitrary")),
    )


_INV_SQRT2 = 1.0 / math.sqrt(2.0)
_INV_SQRT2PI = 1.0 / math.sqrt(2.0 * math.pi)


def _gelu(v):
    return 0.5 * v * (1.0 + lax.erf(v * _INV_SQRT2))


def _gelu_grad(v):
    return 0.5 * (1.0 + lax.erf(v * _INV_SQRT2)) + v * jnp.exp(-0.5 * v * v) * _INV_SQRT2PI


def _gmlp_norm(v, ln_w, ln_b):
    vf = _gelu(v)
    mu = jnp.mean(vf, axis=-1, keepdims=True)
    cen = vf - mu
    rstd = lax.rsqrt(jnp.mean(cen * cen, axis=-1, keepdims=True) + EPS)
    vh = cen * rstd
    return vh, rstd, vh * ln_w + ln_b


def _tril_mask():
    n = GMLP_CHUNK
    return lax.broadcasted_iota(jnp.int32, (n, n), 0) >= lax.broadcasted_iota(jnp.int32, (n, n), 1)


def _gmlp_fwd(proj6, cat, ln_w, ln_b, w_s, b_st):
    _, T, Wd = proj6.shape
    G, N = Wd // HEAD, GMLP_CHUNK

    def body(p_ref, cat_in, lnw_ref, lnb_ref, ws_ref, bst_ref, out_ref):
        del cat_in
        _, _, vn = _gmlp_norm(p_ref[1], lnw_ref[...], lnb_ref[...])
        tril = _tril_mask()
        for g in range(G):
            cols = slice(g * HEAD, (g + 1) * HEAD)
            z = _bdot(jnp.where(tril, ws_ref[g], 0.0), vn[:, cols], NN) + bst_ref[:, g:g + 1]
            out_ref[:, cols] = (_gelu(p_ref[0, :, cols]) * z).astype(BF16)

    return _pcall(
        body, [proj6, cat, ln_w, ln_b, w_s, b_st], name="gmlp_fwd", grid=(T // N,),
        in_specs=[pl.BlockSpec((2, N, Wd), lambda t: (2, t, 0)), pl.BlockSpec(memory_space=pl.ANY),
                  pl.BlockSpec((1, Wd), lambda t: (0, 0)), pl.BlockSpec((1, Wd), lambda t: (0, 0)),
                  pl.BlockSpec((G, N, N), lambda t: (0, 0, 0)), pl.BlockSpec((N, LANES), lambda t: (0, 0))],
        out_specs=pl.BlockSpec((None, N, Wd), lambda t: (1, t, 0)),
        out_shape=jax.ShapeDtypeStruct(cat.shape, cat.dtype),
        input_output_aliases={1: 0},
        compiler_params=_params(("arbitrary",)),
    )


def _gmlp_bwd(proj6, dproj6, dcat, ln_w, ln_b, w_s, b_st):
    _, T, Wd = proj6.shape
    G, N = Wd // HEAD, GMLP_CHUNK

    def body(p_ref, dp_in, db_ref, lnw_ref, lnb_ref, ws_ref, bst_ref, dp_ref, dlnw_ref, dlnb_ref, dws_ref, dbst_ref, dvn):
        del dp_in

        @pl.when(pl.program_id(0) == 0)
        def _():
            dlnw_ref[...] = jnp.zeros_like(dlnw_ref)
            dlnb_ref[...] = jnp.zeros_like(dlnb_ref)
            dws_ref[...] = jnp.zeros_like(dws_ref)
            dbst_ref[...] = jnp.zeros_like(dbst_ref)

        lnw = lnw_ref[...]
        v = p_ref[1]
        vh, rstd, vn = _gmlp_norm(v, lnw, lnb_ref[...])
        tril = _tril_mask()
        lane = lax.broadcasted_iota(jnp.int32, (N, LANES), 1)
        dbst = jnp.zeros((N, LANES), F32)
        for g in range(G):
            cols = slice(g * HEAD, (g + 1) * HEAD)
            wt = jnp.where(tril, ws_ref[g], 0.0)
            vn_g = vn[:, cols]
            z = _bdot(wt, vn_g, NN) + bst_ref[:, g:g + 1]
            u = p_ref[0, :, cols]
            db = db_ref[:, cols]
            dp_ref[0, :, cols] = (db * z * _gelu_grad(u)).astype(BF16)
            dz = db * _gelu(u)
            dbst = dbst + jnp.where(lane == g, jnp.sum(dz, axis=1, keepdims=True), 0.0)
            dws_ref[g] += jnp.where(tril, _bdot(dz, vn_g, NT), 0.0)
            dvn[:, cols] = _bdot(wt, dz, TN)
        dbst_ref[...] += dbst
        dv = dvn[...]
        dlnw_ref[...] += jnp.sum(dv * vh, axis=0, keepdims=True)
        dlnb_ref[...] += jnp.sum(dv, axis=0, keepdims=True)
        dvh = dv * lnw
        dvf = rstd * (dvh - jnp.mean(dvh, axis=-1, keepdims=True) - vh * jnp.mean(dvh * vh, axis=-1, keepdims=True))
        dp_ref[1] = (dvf * _gelu_grad(v)).astype(BF16)

    vec = pl.BlockSpec((1, Wd), lambda t: (0, 0))
    return _pcall(
        body, [proj6, dproj6, dcat, ln_w, ln_b, w_s, b_st], name="gmlp_bwd", grid=(T // N,),
        in_specs=[pl.BlockSpec((2, N, Wd), lambda t: (2, t, 0)), pl.BlockSpec(memory_space=pl.ANY),
                  pl.BlockSpec((None, N, Wd), lambda t: (1, t, 0)), vec, vec,
                  pl.BlockSpec((G, N, N), lambda t: (0, 0, 0)), pl.BlockSpec((N, LANES), lambda t: (0, 0))],
        out_specs=[pl.BlockSpec((2, N, Wd), lambda t: (2, t, 0)), vec, vec,
                   pl.BlockSpec((G, N, N), lambda t: (0, 0, 0)), pl.BlockSpec((N, LANES), lambda t: (0, 0))],
        out_shape=[jax.ShapeDtypeStruct(dproj6.shape, dproj6.dtype), jax.ShapeDtypeStruct((1, Wd), F32),
                   jax.ShapeDtypeStruct((1, Wd), F32), jax.ShapeDtypeStruct((G, N, N), F32),
                   jax.ShapeDtypeStruct((N, LANES), F32)],
        scratch_shapes=[pltpu.VMEM((N, Wd), F32)],
        input_output_aliases={1: 0},
        compiler_params=_params(("arbitrary",)),
    )


def _place():
    return lax.axis_index("x"), lax.axis_index("y"), lax.axis_index("c")


def _other_chips(x, y):
    return [(1 - x, y), (x, 1 - y), (1 - x, 1 - y)]


def _rcopy(src, dst, ssem, rsem, k, to):
    return pltpu.make_async_remote_copy(src_ref=src, dst_ref=dst, send_sem=ssem.at[k], recv_sem=rsem.at[k],
                                        device_id=to, device_id_type=MESH)


def _cast_slot(name, w, place):
    _, R, C = w.shape
    tr = _row_block(R, C)

    def body(p_ref, w_ref, o_ref):
        del p_ref
        o_ref[...] = w_ref[...].astype(BF16)

    return _pcall(
        body, [place[0], w], name=name, prefetch=1, grid=(R // tr,),
        in_specs=[pl.BlockSpec((None, tr, C), lambda i, p: (0, i, 0))],
        out_specs=pl.BlockSpec((None, tr, C), lambda i, p: (p[0], i, 0)),
        out_shape=jax.ShapeDtypeStruct((4, R, C), BF16),
        compiler_params=_params(("parallel",)))


def _half(ref, px, py, pc, Rh):
    return ref.at[2 * px + py, pl.ds(pc * Rh, Rh), :]


def _split_start(name, bufs, n_sem, copies):
    nb = len(bufs)

    def body(*refs):
        for cp in copies(refs[:nb], refs[nb], refs[nb + 1], False):
            cp.start()
        refs[-1][...] = jnp.zeros_like(refs[-1])

    outs = _pcall(
        body, [pltpu.with_memory_space_constraint(b, pltpu.HBM) for b in bufs], same=bufs, token=-1, name=name,
        in_specs=[HBM_SPEC] * nb, out_specs=(SEM_SPEC, SEM_SPEC, *[HBM_SPEC] * nb, VMEM_SPEC),
        out_shape=(pltpu.SemaphoreType.DMA((n_sem,)), pltpu.SemaphoreType.DMA((n_sem,)),
                   *[pltpu.HBM(b.shape, b.dtype) for b in bufs], jax.ShapeDtypeStruct((8, LANES), F32)),
        input_output_aliases={i: 2 + i for i in range(nb)},
        compiler_params=pltpu.CompilerParams(has_side_effects=SIDE_EFFECT))
    return outs[:-1]


def _split_wait(name, started, copies):
    ssem, rsem, *bufs = started
    nb = len(bufs)

    def body(*refs):
        for cp in copies(refs[:nb], refs[nb], refs[nb + 1], True):
            cp.wait_send()
            cp.wait_recv()

    outs = _pcall(
        body, [*bufs, ssem, rsem], name=name, in_specs=[HBM_SPEC] * nb + [SEM_SPEC, SEM_SPEC],
        out_specs=tuple([HBM_SPEC] * nb), out_shape=tuple(pltpu.HBM(b.shape, b.dtype) for b in bufs),
        input_output_aliases={i: i for i in range(nb)},
        compiler_params=pltpu.CompilerParams(has_side_effects=SIDE_EFFECT))
    return outs


def _gather_copies(Rh):
    def copies(refs, ssem, rsem, waiting):
        x, y, c = _place()
        mine = _half(refs[0], x, y, c, Rh)
        return [_rcopy(mine, _half(refs[0], px, py, c, Rh) if waiting else mine, ssem, rsem, j, (px, py, c))
                for j, (px, py) in enumerate(_other_chips(x, y))]
    return copies


def _swap_copies(Rh):
    def copies(refs, ssem, rsem, waiting):
        x, y, c = _place()
        return [_rcopy(refs[0].at[:, pl.ds((1 - c) * Rh, Rh), :], refs[1], ssem, rsem, 0, (x, y, 1 - c))]
    return copies


def _exchange_copies(refs, ssem, rsem, waiting):
    x, y, c = _place()
    return [_rcopy(refs[0].at[2 * px + py], refs[1].at[j], ssem, rsem, j, (px, py, c))
            for j, (px, py) in enumerate(_other_chips(x, y))]


def _landing(shape, dtype):
    return lax.empty(shape, dtype)


def _gather_forward(name, buf):
    _, R, C = buf.shape
    Rh = R // 2

    def body(in_ref, out_ref, ssem, rsem):
        x, y, c = _place()
        chips = _other_chips(x, y)
        passed = [_rcopy(_half(in_ref, px, py, c, Rh), _half(out_ref, px, py, c, Rh), ssem, rsem, j, (x, y, 1 - c))
                  for j, (px, py) in enumerate(chips)]
        for cp in passed:
            cp.start()
        for j, (px, py) in enumerate(chips):
            _rcopy(_half(in_ref, px, py, c, Rh), _half(out_ref, px, py, 1 - c, Rh), ssem, rsem, j, (x, y, 1 - c)).wait_recv()
        for cp in passed:
            cp.wait_send()

    return _pcall(
        body, [buf], name=name, in_specs=[HBM_SPEC], out_specs=HBM_SPEC,
        out_shape=jax.ShapeDtypeStruct(buf.shape, buf.dtype), input_output_aliases={0: 0},
        scratch_shapes=[pltpu.SemaphoreType.DMA((3,)), pltpu.SemaphoreType.DMA((3,))])


def _share_half(name, full):
    R, C = full.shape
    Rh = R // 2

    def body(in_ref, out_ref, ssem, rsem):
        x, y, c = _place()
        cp = _rcopy(in_ref.at[pl.ds(c * Rh, Rh), :], out_ref.at[pl.ds(c * Rh, Rh), :], ssem, rsem, 0, (x, y, 1 - c))
        cp.start()
        cp.wait()

    return _pcall(
        body, [full], name=name, in_specs=[HBM_SPEC], out_specs=HBM_SPEC,
        out_shape=jax.ShapeDtypeStruct((R, C), full.dtype), input_output_aliases={0: 0},
        scratch_shapes=[pltpu.SemaphoreType.DMA((1,)), pltpu.SemaphoreType.DMA((1,))])


def _sum_all_devices(v):
    R, L = v.shape

    def body(v_ref, out_ref, buf, ssem, rsem):
        x, y, c = _place()
        me = 4 * x + 2 * y + c
        buf[me] = v_ref[...]
        copies = []
        for m in range(1, 8):
            to = (x ^ (m >> 2), y ^ ((m >> 1) & 1), c ^ (m & 1))
            copies.append(_rcopy(v_ref, buf.at[me], ssem, rsem, m - 1, to))
        for cp in copies:
            cp.start()
        for cp in copies:
            cp.wait()
        acc = buf[0]
        for d in range(1, 8):
            acc = acc + buf[d]
        out_ref[...] = acc

    return _pcall(
        body, [v], name="sum_all_devices", in_specs=[VMEM_SPEC], out_specs=VMEM_SPEC,
        out_shape=jax.ShapeDtypeStruct((R, L), F32),
        scratch_shapes=[pltpu.VMEM((8, R, L), F32), pltpu.SemaphoreType.DMA((7,)), pltpu.SemaphoreType.DMA((7,))],
        compiler_params=pltpu.CompilerParams(vmem_limit_bytes=VMEM_LIMIT))


def _elementwise(name, body, ins, in_specs, outs, out_specs, grid):
    return _pcall(body, ins, name=name, grid=grid, in_specs=in_specs, out_specs=out_specs, out_shape=outs,
                  compiler_params=_params(("parallel",) * len(grid)))


def _row_block(R, C, bytes_per_row_elem=4, target=1 << 20):
    return _tile(R, max(16, target // (C * bytes_per_row_elem)), 16)


def _add_halves(name, g4, land, c_idx):
    S, R, C = g4.shape
    Rh = R // 2
    tr = _row_block(Rh, C)
    nb = Rh // tr

    def body(c_ref, g_ref, l_ref, o_ref):
        del c_ref
        o_ref[...] = (g_ref[...] + l_ref[...]).astype(BF16)

    return _pcall(
        body, [c_idx, g4, land], name=name, prefetch=1, grid=(S, nb),
        in_specs=[pl.BlockSpec((None, tr, C), lambda s, i, c: (s, c[0] * nb + i, 0)),
                  pl.BlockSpec((None, tr, C), lambda s, i, c: (s, i, 0))],
        out_specs=pl.BlockSpec((None, tr, C), lambda s, i, c: (s, i, 0)),
        out_shape=jax.ShapeDtypeStruct((S, Rh, C), BF16),
        compiler_params=_params(("parallel", "parallel")))


def _sum_chips(name, land, pb, place):
    S, Rh, C = land.shape
    tr = _row_block(Rh, C)
    nb = Rh // tr

    def body(p_ref, c_ref, l_ref, own_ref, o_ref):
        del p_ref, c_ref
        acc = own_ref[...].astype(F32)
        for s in range(S):
            acc = acc + l_ref[s].astype(F32)
        o_ref[...] = acc

    return _pcall(
        body, [place[0], place[1], land, pb], name=name, prefetch=2, grid=(nb,),
        in_specs=[pl.BlockSpec((S, tr, C), lambda i, p, c: (0, i, 0)),
                  pl.BlockSpec((None, tr, C), lambda i, p, c: (p[0], i, 0))],
        out_specs=pl.BlockSpec((tr, C), lambda i, p, c: (c[0] * nb + i, 0)),
        out_shape=jax.ShapeDtypeStruct((2 * Rh, C), F32),
        compiler_params=_params(("parallel",)))


def _adamw(name, w, g, m, v):
    R, C = g.shape
    tr = _row_block(R, C, target=1 << 19)
    c1 = 1.0 - ADAM_B1 ** ADAM_STEP
    c2 = 1.0 - ADAM_B2 ** ADAM_STEP

    def body(w_ref, g_ref, m_ref, v_ref, go_ref, d_ref, nm_ref, nv_ref):
        gv = g_ref[...]
        go_ref[...] = gv
        nm = ADAM_B1 * m_ref[...] + (1.0 - ADAM_B1) * gv
        nv = ADAM_B2 * v_ref[...] + (1.0 - ADAM_B2) * (gv * gv)
        nm_ref[...] = nm
        nv_ref[...] = nv
        d_ref[...] = -ADAM_LR * ((nm / c1) / (jnp.sqrt(nv / c2) + ADAM_EPS) + ADAM_WD * w_ref[...])

    spec = pl.BlockSpec((tr, C), lambda i: (i, 0))
    wspec = pl.BlockSpec((None, tr, C), lambda i: (0, i, 0)) if w.ndim == 3 else spec
    shp = jax.ShapeDtypeStruct((R, C), F32)
    return _elementwise(name, body, [w, g, m, v], [wspec, spec, wspec, wspec], [shp] * 4, [spec] * 4, (R // tr,))


def _lb_grad(dlb, lb_param):
    def body(d_ref, p_ref, o_ref):
        lb = _lower_bound(p_ref[...])
        t = d_ref[...] * lb * (1.0 - lb)
        o_ref[0:1, :] = t
        o_ref[1:2, :] = -t

    return _pcall(body, [dlb, lb_param], name="lb_grad", in_specs=[VMEM_SPEC, VMEM_SPEC], out_specs=VMEM_SPEC,
                  out_shape=jax.ShapeDtypeStruct(lb_param.shape, F32))


class _Reduction:
    def __init__(self, tag, g4, place):
        self.tag, self.place = tag, place
        _, R, C = g4.shape
        self.Rh = R // 2
        self.state = _split_start("swap_start_" + tag, [g4, _landing((4, self.Rh, C), F32)], 1, _swap_copies(self.Rh))

    def exchange(self):
        g4, land = _split_wait("swap_wait_" + self.tag, self.state, _swap_copies(self.Rh))
        pb = _add_halves("addh_" + self.tag, g4, land, self.place[1])
        self.state = _split_start("xchg_start_" + self.tag, [pb, _landing((3,) + pb.shape[1:], BF16)], 3, _exchange_copies)

    def finish(self, w, m, v):
        pb, got = _split_wait("xchg_wait_" + self.tag, self.state, _exchange_copies)
        full = _share_half("share_" + self.tag, _sum_chips("sumc_" + self.tag, got, pb, self.place))
        return tuple(a[None] for a in _adamw("adamw_" + self.tag, w, full, m, v))


def kernel(x, p, pre_mix_w, w_in, lb_param, a_norm_w, gmlp_ln_w, gmlp_ln_b, w_spatial, b_spatial, w_out, post_mix_w, pre_ffn_w, w_gate, w_up, w_down, post_ffn_w, w_ple, w_ple_gate, post_ple_w, loss_target, m_pre_mix_w, m_w_in, m_lb_param, m_a_norm_w, m_gmlp_ln_w, m_gmlp_ln_b, m_w_spatial, m_b_spatial, m_w_out, m_post_mix_w, m_pre_ffn_w, m_w_gate, m_w_up, m_w_down, m_post_ffn_w, m_w_ple, m_w_ple_gate, m_post_ple_w, v_pre_mix_w, v_w_in, v_lb_param, v_a_norm_w, v_gmlp_ln_w, v_gmlp_ln_b, v_w_spatial, v_b_spatial, v_w_out, v_post_mix_w, v_pre_ffn_w, v_w_gate, v_w_up, v_w_down, v_post_ffn_w, v_w_ple, v_w_ple_gate, v_post_ple_w):
    T, D = x.shape[1], x.shape[2]
    Wd = D // 2
    G = Wd // HEAD
    NCH = 4
    Ci = w_in.shape[2]
    Fs = w_gate.shape[2]
    Dq = w_out.shape[1]
    Pd = w_ple.shape[1]
    xs, ps, tgt = x[0], p[0, 0], loss_target[0]
    c_idx = lax.axis_index("c").astype(jnp.int32).reshape((1,))
    place = ((2 * lax.axis_index("x") + lax.axis_index("y")).astype(jnp.int32).reshape((1,)), c_idx)

    weights = dict(w_in=w_in, w_out=w_out, w_gate=w_gate, w_up=w_up, w_down=w_down, w_ple_gate=w_ple_gate, w_ple=w_ple)
    _Order.last = None
    gathering = {}
    for tag, w in weights.items():
        buf = _cast_slot("cast_" + tag, w, place)
        gathering[tag] = _split_start("gather_start_" + tag, [buf], 3, _gather_copies(buf.shape[1] // 2))

    def gathered(tag):
        (buf,) = _split_wait("gather_wait_" + tag, gathering.pop(tag), _gather_copies(weights[tag].shape[1] // 2))
        return _gather_forward("gather_fwd_" + tag, buf)

    res = {}
    moments = dict(w_in=(m_w_in, v_w_in), w_out=(m_w_out, v_w_out), w_gate=(m_w_gate, v_w_gate), w_up=(m_w_up, v_w_up),
                   w_down=(m_w_down, v_w_down), w_ple=(m_w_ple, v_w_ple), w_ple_gate=(m_w_ple_gate, v_w_ple_gate))
    reducing = {}

    def reduce_start(tag, g4):
        reducing[tag] = _Reduction(tag, g4, place)

    def reduce_exchange(tag):
        reducing[tag].exchange()

    def reduce_finish(tag):
        res[tag] = reducing.pop(tag).finish(weights[tag], *moments[tag])

    tm, tn, tk = _tile(T, 1024), _tile(D, 1024), _tile(D, 512)
    ts = _tile(math.gcd(Wd, Ci), 1024)
    tkt = _tile(T, 512)
    n_sec, n_shd = Wd // ts, Ci // ts
    b_st = jnp.pad(b_spatial[0].T, ((0, 0), (0, LANES - G)))

    h1 = _norm_in(xs, pre_mix_w)
    gw_in = gathered("w_in")
    proj6 = _mm("proj", [h1, gw_in],
                [pl.BlockSpec((tm, tk), lambda i, j, k: (i, k)),
                 pl.BlockSpec((None, tk, ts), lambda i, j, k: (j // n_shd, k, j % n_shd))],
                jax.ShapeDtypeStruct((6, T, Wd), F32),
                pl.BlockSpec((None, tm, ts), lambda i, j, k: (j // n_sec, i, j % n_sec)),
                (tm, ts), (T // tm, 6 * n_sec, D // tk), NN)
    cat, o_a, states = _hgrn2_fwd(proj6, lb_param, a_norm_w)
    cat = _gmlp_fwd(proj6, cat, gmlp_ln_w, gmlp_ln_b, w_spatial[0], b_st)
    nkc = Wd // tk
    gw_out = gathered("w_out").reshape(D, D)
    mix = _mm("mix", [cat, gw_out],
              [pl.BlockSpec((None, tm, tk), lambda i, j, k: (k // nkc, i, k % nkc)),
               pl.BlockSpec((tk, tn), lambda i, j, k: (k, j))],
              jax.ShapeDtypeStruct((T, D), F32), pl.BlockSpec((tm, tn), lambda i, j, k: (i, j)),
              (tm, tn), (T // tm, D // tn, D // tk), NN)
    x1, h2 = _mid1(xs, mix, post_mix_w, pre_ffn_w)
    gw_gate = gathered("w_gate")
    gw_up = gathered("w_up")
    gt, up, act = _ffn_up(h2, gw_gate, gw_up)
    tnf = _tile(D, 512)
    gw_down = gathered("w_down")
    ff = _mm("ffn_down", [act, gw_down],
             [pl.BlockSpec((None, tm, Fs), lambda i, j, k: (k, i, 0)),
              pl.BlockSpec((None, Fs, tnf), lambda i, j, k: (k, 0, j))],
             jax.ShapeDtypeStruct((T, D), F32), pl.BlockSpec((tm, tnf), lambda i, j, k: (i, j)),
             (tm, tnf), (T // tm, D // tnf, NCH), NN)
    x2, x2b = _mid2(x1, ff, post_ffn_w)
    gw_pg = gathered("w_ple_gate").reshape(D, D)
    gl = _mm("ple_gate", [x2b, gw_pg],
             [pl.BlockSpec((tm, tk), lambda i, j, k: (i, k)), pl.BlockSpec((tk, tn), lambda i, j, k: (k, j))],
             jax.ShapeDtypeStruct((T, D), F32), pl.BlockSpec((tm, tn), lambda i, j, k: (i, j)),
             (tm, tn), (T // tm, D // tn, D // tk), NN)
    tq = _tile(Dq, 1024)
    nq = Dq // tq
    gw_ple = gathered("w_ple")
    e = _mm("ple_embed", [ps, gw_ple],
            [pl.BlockSpec((tm, Pd), lambda i, j, k: (i, 0)),
             pl.BlockSpec((None, Pd, tq), lambda i, j, k: (j // nq, 0, j % nq))],
            jax.ShapeDtypeStruct((T, D), F32), pl.BlockSpec((tm, tq), lambda i, j, k: (i, j)),
            (tm, tq), (T // tm, D // tq, 1), NN)

    d3, de, dgl, dw_pp, loss_vec = _head(x2, gl, e, post_ple_w, tgt)
    g_ple = _mm("d_w_ple", [ps, de],
                [pl.BlockSpec((tkt, Pd), lambda i, j, k: (k, 0)), pl.BlockSpec((tkt, tq), lambda i, j, k: (k, j))],
                jax.ShapeDtypeStruct((NCH, Pd, Dq), F32),
                pl.BlockSpec((None, Pd, tq), lambda i, j, k: (j // nq, 0, j % nq)),
                (Pd, tq), (1, D // tq, T // tkt), TN)
    reduce_start("w_ple", g_ple)
    tmw = _tile(D, 1024)
    g_pg = _mm("d_w_ple_gate", [x2b, dgl],
               [pl.BlockSpec((tkt, tq), lambda i, j, k: (k, i)), pl.BlockSpec((tkt, tn), lambda i, j, k: (k, j))],
               jax.ShapeDtypeStruct((NCH, Dq, D), F32), pl.BlockSpec((None, tq, tn), lambda i, j, k: (i // nq, i % nq, j)),
               (tq, tn), (D // tq, D // tn, T // tkt), TN)
    reduce_start("w_ple_gate", g_pg)
    reduce_exchange("w_ple")
    dx2 = _mm("d_x2", [dgl, gw_pg],
              [pl.BlockSpec((tm, tk), lambda i, j, k: (i, k)), pl.BlockSpec((tn, tk), lambda i, j, k: (j, k))],
              jax.ShapeDtypeStruct((T, D), F32), pl.BlockSpec((tm, tn), lambda i, j, k: (i, j)),
              (tm, tn), (T // tm, D // tn, D // tk), NT,
              add=d3, add_spec=pl.BlockSpec((tm, tn), lambda i, j, k: (i, j)))
    reduce_exchange("w_ple_gate")
    dff, dw_pff = _bwd_ffn_norm(dx2, ff, post_ffn_w)
    g_down = _mm("d_w_down", [act, dff],
                 [pl.BlockSpec((None, tkt, Fs), lambda i, j, k: (i, k, 0)),
                  pl.BlockSpec((tkt, tnf), lambda i, j, k: (k, j))],
                 jax.ShapeDtypeStruct((NCH, Fs, D), F32), pl.BlockSpec((None, Fs, tnf), lambda i, j, k: (i, 0, j)),
                 (Fs, tnf), (NCH, D // tnf, T // tkt), TN)
    reduce_start("w_down", g_down)
    reduce_finish("w_ple")
    dgt, dup = _ffn_dact(dff, gw_down, gt, up)
    reduce_exchange("w_down")
    reduce_finish("w_ple_gate")
    tmf = _tile(D, 512)
    wgrad_specs = [pl.BlockSpec((tkt, tmf), lambda i, j, k: (k, i)),
                   pl.BlockSpec((None, tkt, Fs), lambda i, j, k: (j, k, 0))]
    wgrad_out = pl.BlockSpec((None, tmf, Fs), lambda i, j, k: (j, i, 0))
    g_gate = _mm("d_w_gate", [h2, dgt], wgrad_specs, jax.ShapeDtypeStruct((NCH, D, Fs), F32), wgrad_out,
                 (tmf, Fs), (D // tmf, NCH, T // tkt), TN)
    reduce_start("w_gate", g_gate)
    g_up = _mm("d_w_up", [h2, dup], wgrad_specs, jax.ShapeDtypeStruct((NCH, D, Fs), F32), wgrad_out,
               (tmf, Fs), (D // tmf, NCH, T // tkt), TN)
    reduce_start("w_up", g_up)
    reduce_exchange("w_gate")
    tmh = _tile(T, 1024)
    a_sp =pl.BlockSpec((None, tmh, Fs), lambda i, j, k: (k, i, 0))
    b_sp = pl.BlockSpec((None, tnf, Fs), lambda i, j, k: (k, j, 0))
    dh2 = _mm("d_h2", [dgt, gw_gate, dup, gw_up], [a_sp, b_sp, a_sp, b_sp],
              jax.ShapeDtypeStruct((T, D), F32), pl.BlockSpec((tmh, tnf), lambda i, j, k: (i, j)),
              (tmh, tnf), (T // tmh, D // tnf, NCH), NT)
    reduce_exchange("w_up")
    reduce_finish("w_down")
    dx1, dmix, dw_pf, dw_pm = _bwd_mid(dx2, dh2, x1, pre_ffn_w, mix, post_mix_w)
    tw = _tile(Wd, 1024)
    nw = Wd // tw
    nwq = Wd // tq
    g_out = _mm("d_w_out", [cat, dmix],
                [pl.BlockSpec((None, tkt, tq), lambda i, j, k: (i // nwq, k, i % nwq)),
                 pl.BlockSpec((tkt, tn), lambda i, j, k: (k, j))],
                jax.ShapeDtypeStruct((NCH, Dq, D), F32), pl.BlockSpec((None, tq, tn), lambda i, j, k: (i // nq, i % nq, j)),
                (tq, tn), (D // tq, D // tn, T // tkt), TN)
    reduce_start("w_out", g_out)
    reduce_finish("w_gate")
    dcat = _mm("d_cat", [dmix, gw_out],
               [pl.BlockSpec((tm, tk), lambda i, j, k: (i, k)), pl.BlockSpec((tw, tk), lambda i, j, k: (j, k))],
               jax.ShapeDtypeStruct((2, T, Wd), F32), pl.BlockSpec((None, tm, tw), lambda i, j, k: (j // nw, i, j % nw)),
               (tm, tw), (T // tm, D // tw, D // tk), NT)
    reduce_exchange("w_out")
    dproj6, dw_an, dlb = _hgrn2_bwd(proj6, o_a, states, dcat, lb_param, a_norm_w)
    reduce_finish("w_up")
    dproj6, dw_lnw, dw_lnb, dw_sp, dw_bst = _gmlp_bwd(proj6, dproj6, dcat, gmlp_ln_w, gmlp_ln_b, w_spatial[0], b_st)
    g_in = _mm("d_w_in", [h1, dproj6],
               [pl.BlockSpec((tkt, tmw), lambda i, j, k: (k, i)),
                pl.BlockSpec((None, tkt, ts), lambda i, j, k: (j // n_sec, k, j % n_sec))],
               jax.ShapeDtypeStruct((NCH, D, Ci), F32),
               pl.BlockSpec((None, tmw, ts), lambda i, j, k: (j // n_shd, i, j % n_shd)),
               (tmw, ts), (D // tmw, 6 * n_sec, T // tkt), TN)
    reduce_start("w_in", g_in)
    reduce_finish("w_out")
    dh1 = _mm("d_h1", [dproj6, gw_in],
              [pl.BlockSpec((None, tm, ts), lambda i, j, k: (k // n_sec, i, k % n_sec)),
               pl.BlockSpec((None, tn, ts), lambda i, j, k: (k // n_shd, j, k % n_shd))],
              jax.ShapeDtypeStruct((T, D), F32), pl.BlockSpec((tm, tn), lambda i, j, k: (i, j)),
              (tm, tn), (T // tm, D // tn, 6 * n_sec), NT)
    reduce_exchange("w_in")
    grad_x, dw_pre = _bwd_in(dx1, dh1, xs, pre_mix_w)

    small = [("pre_mix_w", dw_pre), ("lb", dlb), ("a_norm_w", dw_an), ("gmlp_ln_w", dw_lnw), ("gmlp_ln_b", dw_lnb),
             ("w_spatial", dw_sp), ("b_spatial_t", dw_bst), ("post_mix_w", dw_pm), ("pre_ffn_w", dw_pf),
             ("post_ffn_w", dw_pff), ("post_ple_w", dw_pp), ("loss", loss_vec)]
    rows = [a.size // LANES for _, a in small]
    total = sum(rows)
    pad = (-total) % 8
    packed = jnp.concatenate([a.reshape(-1, LANES) for _, a in small] + [jnp.zeros((pad, LANES), F32)], axis=0)
    summed = _sum_all_devices(packed)
    off, piece = 0, {}
    for (nm, a), r in zip(small, rows):
        piece[nm] = summed[off:off + r].reshape(a.shape)
        off += r
    loss = jnp.sum(piece["loss"])
    g_small = {
        "pre_mix_w": piece["pre_mix_w"], "lb_param": _lb_grad(piece["lb"], lb_param), "a_norm_w": piece["a_norm_w"],
        "gmlp_ln_w": piece["gmlp_ln_w"], "gmlp_ln_b": piece["gmlp_ln_b"], "w_spatial": piece["w_spatial"][None],
        "b_spatial": piece["b_spatial_t"][:, :G].T[None], "post_mix_w": piece["post_mix_w"],
        "pre_ffn_w": piece["pre_ffn_w"], "post_ffn_w": piece["post_ffn_w"], "post_ple_w": piece["post_ple_w"],
    }
    w_small = dict(pre_mix_w=(pre_mix_w, m_pre_mix_w, v_pre_mix_w), lb_param=(lb_param, m_lb_param, v_lb_param),
                   a_norm_w=(a_norm_w, m_a_norm_w, v_a_norm_w), gmlp_ln_w=(gmlp_ln_w, m_gmlp_ln_w, v_gmlp_ln_w),
                   gmlp_ln_b=(gmlp_ln_b, m_gmlp_ln_b, v_gmlp_ln_b), w_spatial=(w_spatial, m_w_spatial, v_w_spatial),
                   b_spatial=(b_spatial, m_b_spatial, v_b_spatial), post_mix_w=(post_mix_w, m_post_mix_w, v_post_mix_w),
                   pre_ffn_w=(pre_ffn_w, m_pre_ffn_w, v_pre_ffn_w), post_ffn_w=(post_ffn_w, m_post_ffn_w, v_post_ffn_w),
                   post_ple_w=(post_ple_w, m_post_ple_w, v_post_ple_w))
    names_small = list(w_small)

    def pack(arrs):
        flat = jnp.concatenate([a.reshape(-1, LANES) for a in arrs], axis=0)
        return jnp.pad(flat, ((0, (-flat.shape[0]) % 16), (0, 0)))

    pk = [pack([w_small[n][i] for n in names_small]) for i in range(3)]
    pg = pack([g_small[n] for n in names_small])
    _, sd, sm, sv = _adamw("adamw_small", pk[0], pg, pk[1], pk[2])
    off = 0
    for n in names_small:
        shp = w_small[n][0].shape
        r = w_small[n][0].size // LANES
        res[n] = (g_small[n].reshape(shp), sd[off:off + r].reshape(shp), sm[off:off + r].reshape(shp),
                  sv[off:off + r].reshape(shp))
        off += r

    reduce_finish("w_in")

    order =["pre_mix_w", "w_in", "lb_param", "a_norm_w", "gmlp_ln_w", "gmlp_ln_b", "w_spatial", "b_spatial", "w_out",
             "post_mix_w", "pre_ffn_w", "w_gate", "w_up", "w_down", "post_ffn_w", "w_ple", "w_ple_gate", "post_ple_w"]
    return (loss, grad_x[None], *[res[n][0] for n in order], *[res[n][1] for n in order],
            *[res[n][2] for n in order], *[res[n][3] for n in order])
```

```python
import functools
import math

import jax
import jax.numpy as jnp
from jax import lax
from jax.experimental import pallas as pl
from jax.experimental.pallas import tpu as pltpu

F32 = jnp.float32
BF16 = jnp.bfloat16
PARTIAL = jnp.bfloat16
EPS = 1e-6
HEAD = 128
GLA_CHUNK = 64
HEADS_PER_STEP = 2
GMLP_CHUNK = 128
LANES = 128
VMEM_LIMIT = 48 * 1024 * 1024
ADAM_LR, ADAM_B1, ADAM_B2, ADAM_EPS, ADAM_WD, ADAM_STEP = 0.001, 0.9, 0.999, 1e-08, 0.01, 10
MESH = pl.DeviceIdType.MESH
HBM_SPEC = pl.BlockSpec(memory_space=pltpu.HBM)
VMEM_SPEC = pl.BlockSpec(memory_space=pltpu.VMEM)
SEM_SPEC = pl.BlockSpec(memory_space=pltpu.SEMAPHORE)
SIDE_EFFECT = pltpu.SideEffectType.DATAFLOW_SIDE_EFFECTING

NN = ((1,), (0,))
NT = ((1,), (1,))
TN = ((0,), (0,))


def _tile(n, pref, mult=LANES):
    if n <= pref:
        return n
    t = (pref // mult) * mult
    while t >= mult:
        if n % t == 0:
            return t
        t -= mult
    return n


def _params(sem):
    return pltpu.CompilerParams(dimension_semantics=sem, vmem_limit_bytes=VMEM_LIMIT)


class _Order:
    last = None


def _pcall(body, args, *, token=0, prefetch=0, grid=(), in_specs, out_specs, scratch_shapes=(), same=(), **kw):
    args, in_specs = list(args), list(in_specs)
    n = len(args)
    run = body
    if _Order.last is not None and not any(_Order.last is a for a in [*args, *same]):
        def run(*refs):
            return body(*refs[:n], *refs[n + 1:])
        args.append(_Order.last)
        in_specs.append(pl.BlockSpec(memory_space=pl.ANY))
    if prefetch:
        kw["grid_spec"] = pltpu.PrefetchScalarGridSpec(num_scalar_prefetch=prefetch, grid=grid, in_specs=in_specs,
                                                       out_specs=out_specs, scratch_shapes=list(scratch_shapes))
    else:
        kw.update(grid=grid, in_specs=in_specs, out_specs=out_specs, scratch_shapes=list(scratch_shapes))
    outs = pl.pallas_call(run, **kw)(*args)
    _Order.last = outs[token] if isinstance(outs, (tuple, list)) else outs
    return outs


def _dot(a, b, dims, precision=None):
    return lax.dot_general(a, b, (dims, ((), ())), preferred_element_type=F32, precision=precision)


def _bdot(a, b, dims):
    return _dot(a.astype(BF16), b.astype(BF16), dims)


def _fdot(a, b, dims):
    return _dot(a, b, dims, precision=lax.Precision.HIGHEST)


def _sigmoid(v):
    return 1.0 / (1.0 + jnp.exp(-v))


def _mm(name, operands, specs, out_shape, out_spec, acc_shape, grid, dims, add=None, add_spec=None):
    n_pairs = len(operands) // 2
    nk = grid[2]

    def body(*refs):
        ins = refs[:2 * n_pairs]
        r_ref = refs[2 * n_pairs] if add is not None else None
        o_ref, acc = refs[-2], refs[-1]
        k = pl.program_id(2)

        @pl.when(k == 0)
        def _():
            acc[...] = jnp.zeros_like(acc)

        part = _bdot(ins[0][...], ins[1][...], dims)
        for q in range(1, n_pairs):
            part = part + _bdot(ins[2 * q][...], ins[2 * q + 1][...], dims)
        acc[...] += part

        @pl.when(k == nk - 1)
        def _():
            r = acc[...]
            if r_ref is not None:
                r = r + r_ref[...].astype(F32)
            o_ref[...] = r.astype(o_ref.dtype)

    in_specs = list(specs) + ([add_spec] if add is not None else [])
    args = list(operands) + ([add] if add is not None else [])
    return _pcall(
        body, args, name=name, grid=grid, in_specs=in_specs, out_specs=out_spec, out_shape=out_shape,
        scratch_shapes=[pltpu.VMEM(acc_shape, F32)],
        compiler_params=_params(("parallel", "parallel", "arbitrary")),
    )


def _rms(v, w):
    r = lax.rsqrt(jnp.mean(v * v, axis=-1, keepdims=True) + EPS)
    return v * r * w


def _rms_bwd(v, w, dy):
    r = lax.rsqrt(jnp.mean(v * v, axis=-1, keepdims=True) + EPS)
    vh = v * r
    gy = dy * w
    dv = r * (gy - vh * jnp.mean(gy * vh, axis=-1, keepdims=True))
    return dv, jnp.sum(dy * vh, axis=0, keepdims=True)


def _rowwise(name, body, ins, in_kinds, outs, out_kinds, T, D):
    tr = _tile(T, 128, 16)
    row = pl.BlockSpec((tr, D), lambda i: (i, 0))
    vec = pl.BlockSpec((1, D), lambda i: (0, 0))
    pick = {"row": row, "vec": vec}
    return _pcall(
        body, ins, name=name, grid=(T // tr,),
        in_specs=[pick[k] for k in in_kinds], out_specs=[pick[k] for k in out_kinds], out_shape=outs,
        compiler_params=_params(("arbitrary",)),
    )


def _acc_vec(ref, val):
    @pl.when(pl.program_id(0) == 0)
    def _():
        ref[...] = jnp.zeros_like(ref)
    ref[...] += val


def _norm_in(x, w):
    T, D = x.shape

    def body(x_ref, w_ref, h_ref):
        h_ref[...] = _rms(x_ref[...], w_ref[...]).astype(BF16)

    return _rowwise("norm_in", body, [x, w], ["row", "vec"], [jax.ShapeDtypeStruct((T, D), BF16)], ["row"], T, D)[0]


def _mid1(x, mix, w_pm, w_pf):
    T, D = x.shape

    def body(x_ref, mix_ref, wpm_ref, wpf_ref, x1_ref, h2_ref):
        x1 = x_ref[...] + _rms(mix_ref[...], wpm_ref[...])
        x1_ref[...] = x1
        h2_ref[...] = _rms(x1, wpf_ref[...]).astype(BF16)

    return _rowwise("mid1", body, [x, mix, w_pm, w_pf], ["row", "row", "vec", "vec"],
                    [jax.ShapeDtypeStruct((T, D), F32), jax.ShapeDtypeStruct((T, D), BF16)], ["row", "row"], T, D)


def _mid2(x1, ff, w_pff):
    T, D = x1.shape

    def body(x1_ref, ff_ref, w_ref, x2_ref, x2b_ref):
        x2 = x1_ref[...] + _rms(ff_ref[...], w_ref[...])
        x2_ref[...] = x2
        x2b_ref[...] = x2.astype(BF16)

    return _rowwise("mid2", body, [x1, ff, w_pff], ["row", "row", "vec"],
                    [jax.ShapeDtypeStruct((T, D), F32), jax.ShapeDtypeStruct((T, D), BF16)], ["row", "row"], T, D)


def _head(x2, gl, e, w_pp, tgt):
    T, D = x2.shape

    def body(x2_ref, gl_ref, e_ref, w_ref, t_ref, d3_ref, de_ref, dgl_ref, dw_ref, loss_ref):
        ev = e_ref[...]
        gate = _sigmoid(gl_ref[...])
        pe = ev * gate
        w = w_ref[...]
        err = x2_ref[...] + _rms(pe, w) - t_ref[...]
        d3 = err * (1.0 / D)
        d3_ref[...] = d3
        dpe, dw = _rms_bwd(pe, w, d3)
        de_ref[...] = (dpe * gate).astype(BF16)
        dgl_ref[...] = (dpe * ev * gate * (1.0 - gate)).astype(BF16)
        _acc_vec(dw_ref, dw)
        _acc_vec(loss_ref, jnp.sum(err * err, axis=0, keepdims=True) * (0.5 / D))

    return _rowwise("head", body, [x2, gl, e, w_pp, tgt], ["row", "row", "row", "vec", "row"],
                    [jax.ShapeDtypeStruct((T, D), F32), jax.ShapeDtypeStruct((T, D), BF16),
                     jax.ShapeDtypeStruct((T, D), BF16), jax.ShapeDtypeStruct((1, D), F32),
                     jax.ShapeDtypeStruct((1, D), F32)], ["row", "row", "row", "vec", "vec"], T, D)


def _bwd_ffn_norm(dx2, ff, w_pff):
    T, D = dx2.shape

    def body(d_ref, ff_ref, w_ref, dff_ref, dw_ref):
        dff, dw = _rms_bwd(ff_ref[...], w_ref[...], d_ref[...])
        dff_ref[...] = dff.astype(BF16)
        _acc_vec(dw_ref, dw)

    return _rowwise("bwd_ffn_norm", body, [dx2, ff, w_pff], ["row", "row", "vec"],
                    [jax.ShapeDtypeStruct((T, D), BF16), jax.ShapeDtypeStruct((1, D), F32)], ["row", "vec"], T, D)


def _bwd_mid(dx2, dh2, x1, w_pf, mix, w_pm):
    T, D = dx2.shape

    def body(dx2_ref, dh2_ref, x1_ref, wpf_ref, mix_ref, wpm_ref, dx1_ref, dmix_ref, dwpf_ref, dwpm_ref):
        d1, dwpf = _rms_bwd(x1_ref[...], wpf_ref[...], dh2_ref[...])
        dx1 = dx2_ref[...] + d1
        dx1_ref[...] = dx1
        dmix, dwpm = _rms_bwd(mix_ref[...], wpm_ref[...], dx1)
        dmix_ref[...] = dmix.astype(BF16)
        _acc_vec(dwpf_ref, dwpf)
        _acc_vec(dwpm_ref, dwpm)

    return _rowwise("bwd_mid", body, [dx2, dh2, x1, w_pf, mix, w_pm], ["row", "row", "row", "vec", "row", "vec"],
                    [jax.ShapeDtypeStruct((T, D), F32), jax.ShapeDtypeStruct((T, D), BF16),
                     jax.ShapeDtypeStruct((1, D), F32), jax.ShapeDtypeStruct((1, D), F32)],
                    ["row", "row", "vec", "vec"], T, D)


def _bwd_in(dx1, dh1, x, w_pre):
    T, D = dx1.shape

    def body(dx1_ref, dh1_ref, x_ref, w_ref, gx_ref, dw_ref):
        d0, dw = _rms_bwd(x_ref[...], w_ref[...], dh1_ref[...])
        gx_ref[...] = dx1_ref[...] + d0
        _acc_vec(dw_ref, dw)

    return _rowwise("bwd_in", body, [dx1, dh1, x, w_pre], ["row", "row", "row", "vec"],
                    [jax.ShapeDtypeStruct((T, D), F32), jax.ShapeDtypeStruct((1, D), F32)], ["row", "vec"], T, D)


def _ffn_up(h2, wg, wu):
    T, D = h2.shape
    S, _, Fs = wg.shape
    tm, tk = _tile(T, 512), _tile(D, 512)
    nk = D // tk

    def body(h_ref, wg_ref, wu_ref, gt_ref, up_ref, act_ref, accg, accu):
        k = pl.program_id(2)

        @pl.when(k == 0)
        def _():
            accg[...] = jnp.zeros_like(accg)
            accu[...] = jnp.zeros_like(accu)

        h = h_ref[...]
        accg[...] += _dot(h, wg_ref[...], NN)
        accu[...] += _dot(h, wu_ref[...], NN)

        @pl.when(k == nk - 1)
        def _():
            g, u = accg[...], accu[...]
            gt_ref[...] = g.astype(BF16)
            up_ref[...] = u.astype(BF16)
            act_ref[...] = (g * _sigmoid(g) * u).astype(BF16)

    w_spec = pl.BlockSpec((None, tk, Fs), lambda i, j, k: (j, k, 0))
    o_spec = pl.BlockSpec((None, tm, Fs), lambda i, j, k: (j, i, 0))
    return _pcall(
        body, [h2, wg, wu], name="ffn_up", grid=(T // tm, S, nk),
        in_specs=[pl.BlockSpec((tm, tk), lambda i, j, k: (i, k)), w_spec, w_spec],
        out_specs=[o_spec, o_spec, o_spec],
        out_shape=[jax.ShapeDtypeStruct((S, T, Fs), BF16)] * 3,
        scratch_shapes=[pltpu.VMEM((tm, Fs), F32), pltpu.VMEM((tm, Fs), F32)],
        compiler_params=_params(("parallel", "parallel", "arbitrary")),
    )


def _ffn_dact(dff, wd, gt, up):
    T, D = dff.shape
    S, Fs, _ = wd.shape
    tm, tk = _tile(T, 512), _tile(D, 512)
    nk = D // tk

    def body(d_ref, w_ref, gt_ref, up_ref, dgt_ref, dup_ref, acc):
        k = pl.program_id(2)

        @pl.when(k == 0)
        def _():
            acc[...] = jnp.zeros_like(acc)

        acc[...] += _dot(d_ref[...], w_ref[...], NT)

        @pl.when(k == nk - 1)
        def _():
            da, g, u = acc[...], gt_ref[...].astype(F32), up_ref[...].astype(F32)
            sg = _sigmoid(g)
            dgt_ref[...] = (da * u * sg * (1.0 + g * (1.0 - sg))).astype(BF16)
            dup_ref[...] = (da * g * sg).astype(BF16)

    o_spec = pl.BlockSpec((None, tm, Fs), lambda i, j, k: (j, i, 0))
    return _pcall(
        body, [dff, wd, gt, up], name="ffn_dact", grid=(T // tm, S, nk),
        in_specs=[pl.BlockSpec((tm, tk), lambda i, j, k: (i, k)),
                  pl.BlockSpec((None, Fs, tk), lambda i, j, k: (j, 0, k)), o_spec, o_spec],
        out_specs=[o_spec, o_spec],
        out_shape=[jax.ShapeDtypeStruct((S, T, Fs), BF16), jax.ShapeDtypeStruct((S, T, Fs), BF16)],
        scratch_shapes=[pltpu.VMEM((tm, Fs), F32)],
        compiler_params=_params(("parallel", "parallel", "arbitrary")),
    )


def _hgrn2_gates(q, f, lb):
    C = GLA_CHUNK
    sq = _sigmoid(q)
    qs = q * sq
    sf = _sigmoid(f)
    fg = lb + (1.0 - lb) * sf
    lf = jnp.log(jnp.maximum(fg, 1e-30))
    row = lax.broadcasted_iota(jnp.int32, (C, C), 0)
    col = lax.broadcasted_iota(jnp.int32, (C, C), 1)
    tril = row >= col
    b = _dot(jnp.where(tril, 1.0, 0.0), lf, NN, precision=lax.Precision.HIGHEST)
    bmid = b[C // 2 - 1:C // 2, :]
    bend = b[C - 1:C, :]
    ea = jnp.exp(b - bmid)
    ena = jnp.exp(bmid - b)
    eb = ea * jnp.exp(bmid)
    ee = ena * jnp.exp(bend - bmid)
    return sq, qs, sf, fg, tril, ea, ena, eb, ee, jnp.exp(bend)


def _lower_bound(lbp):
    return _sigmoid(lbp[0:1, :] - lbp[1:2, :])


def _hgrn2_fwd(proj6, lb_param, a_norm_w):
    _, T, Wd = proj6.shape
    H, C = Wd // HEAD, GLA_CHUNK
    TB = _tile(T, 512, C)
    NB, NCB = T // TB, TB // C

    HS, W2 = HEADS_PER_STEP, HEADS_PER_STEP * HEAD

    def body(p_ref, lbp_ref, w_ref, cat_ref, o_ref, st_ref, S):
        @pl.when(pl.program_id(1) == 0)
        def _():
            S[...] = jnp.zeros_like(S)

        lanes = [slice(hh * HEAD, (hh + 1) * HEAD) for hh in range(HS)]
        lbs = [_lower_bound(lbp_ref[:, ls]) for ls in lanes]
        ws = [w_ref[:, ls] for ls in lanes]

        def chunk(c, carry):
            rows = pl.ds(pl.multiple_of(c * C, C), C)
            loaded = [([p_ref[s, rows, ls] for s in range(4)], S[hh]) for hh, ls in enumerate(lanes)]
            done = []
            for hh, ((q, f, v, g), st) in enumerate(loaded):
                _, qs, _, fg, tril, ea, ena, eb, ee, ebend = _hgrn2_gates(q, f, lbs[hh])
                k = 1.0 - fg
                p = jnp.where(tril, _bdot(qs * ea, k * ena, NT), 0.0)
                o = _bdot(qs * eb, st, NT) + _bdot(p, v, NN)
                r = lax.rsqrt(jnp.mean(o * o, axis=-1, keepdims=True) + EPS)
                done.append((st, st * ebend + _bdot(v, k * ee, TN), o,
                             (o * r * ws[hh] * (g * _sigmoid(g))).astype(BF16)))
            for hh, ls in enumerate(lanes):
                st, st_new, o, a_out = done[hh]
                st_ref[hh, c] = st
                S[hh] = st_new
                o_ref[rows, ls] = o
                cat_ref[rows, ls] = a_out
            return carry

        lax.fori_loop(0, NCB, chunk, 0)

    return _pcall(
        body, [proj6, lb_param, a_norm_w], name="hgrn2_fwd", grid=(H // HS, NB),
        in_specs=[pl.BlockSpec((4, TB, W2), lambda h, t: (0, t, h)),
                  pl.BlockSpec((2, W2), lambda h, t: (0, h)),
                  pl.BlockSpec((1, W2), lambda h, t: (0, h))],
        out_specs=[pl.BlockSpec((None, TB, W2), lambda h, t: (0, t, h)),
                   pl.BlockSpec((TB, W2), lambda h, t: (t, h)),
                   pl.BlockSpec((HS, NCB, HEAD, HEAD), lambda h, t: (h, t, 0, 0))],
        out_shape=[jax.ShapeDtypeStruct((2, T, Wd), BF16), jax.ShapeDtypeStruct((T, Wd), F32),
                   jax.ShapeDtypeStruct((H, T // C, HEAD, HEAD), F32)],
        scratch_shapes=[pltpu.VMEM((HS, HEAD, HEAD), F32)],
        compiler_params=_params(("parallel", "arbitrary")),
    )


def _hgrn2_bwd(proj6, o, states, dcat, lb_param, a_norm_w):
    _, T, Wd = proj6.shape
    H, C = Wd // HEAD, GLA_CHUNK
    TB = _tile(T, 512, C)
    NB, NCB = T // TB, TB // C
    HS, W2 = HEADS_PER_STEP, HEADS_PER_STEP * HEAD

    def body(p_ref, o_ref, st_ref, da_ref, lbp_ref, w_ref, dp_ref, dw_ref, dlb_ref, dS):
        @pl.when(pl.program_id(1) == 0)
        def _():
            dS[...] = jnp.zeros_like(dS)
            dw_ref[...] = jnp.zeros_like(dw_ref)
            dlb_ref[...] = jnp.zeros_like(dlb_ref)

        lanes = [slice(hh * HEAD, (hh + 1) * HEAD) for hh in range(HS)]
        lbs = [_lower_bound(lbp_ref[:, ls]) for ls in lanes]
        ws = [w_ref[:, ls] for ls in lanes]

        def chunk(i, carry):
            c = NCB - 1 - i
            rows = pl.ds(pl.multiple_of(c * C, C), C)
            loaded = [([p_ref[s, rows, ls] for s in range(4)], st_ref[hh, c], dS[hh], o_ref[rows, ls], da_ref[rows, ls],
                       dw_ref[:, ls], dlb_ref[:, ls]) for hh, ls in enumerate(lanes)]
            done = []
            for hh, ((q, f, v, g), st, dst, ov, da, dw_old, dlb_old) in enumerate(loaded):
                lb, w = lbs[hh], ws[hh]
                sq, qs, sf, fg, tril, ea, ena, eb, ee, ebend = _hgrn2_gates(q, f, lb)
                k = 1.0 - fg
                qt, kt, qb, ke = qs * ea, k * ena, qs * eb, k * ee
                p = jnp.where(tril, _bdot(qt, kt, NT), 0.0)
                r = lax.rsqrt(jnp.mean(ov * ov, axis=-1, keepdims=True) + EPS)
                oh = ov * r
                sg = _sigmoid(g)
                gs = g * sg
                dw_new = dw_old + jnp.sum(da * oh * gs, axis=0, keepdims=True)
                dg = da * oh * w * sg * (1.0 + g * (1.0 - sg))
                doh = da * w * gs
                do = r * (doh - oh * jnp.mean(doh * oh, axis=-1, keepdims=True))
                dqb = _fdot(do, st, NN)
                dp = jnp.where(tril, _bdot(do, v, NT), 0.0)
                dv = _bdot(p, do, TN) + _bdot(ke, dst, NT)
                dqt = _fdot(dp, kt, NN)
                dkt = _fdot(dp, qt, TN)
                dke = _fdot(v, dst, NN)
                dqs = dqt * ea + dqb * eb
                dk = dkt * ena + dke * ee
                db = dqt * qt + dqb * qb - dkt * kt - dke * ke
                db_end = (jnp.sum(dke * ke, axis=0, keepdims=True)
                          + ebend * jnp.sum(dst * st, axis=0, keepdims=True))
                triu = lax.broadcasted_iota(jnp.int32, (C, C), 0) <= lax.broadcasted_iota(jnp.int32, (C, C), 1)
                dlf = _fdot(jnp.where(triu, 1.0, 0.0), db, NN) + db_end
                dfg = jnp.where(fg > 1e-30, dlf / fg, 0.0) - dk
                done.append((dw_new, dlb_old + jnp.sum(dfg * (1.0 - sf), axis=0, keepdims=True),
                             [(dqs * sq * (1.0 + q * (1.0 - sq))).astype(BF16),
                              (dfg * (1.0 - lb) * sf * (1.0 - sf)).astype(BF16), dv.astype(BF16), dg.astype(BF16)],
                             dst * ebend + _bdot(do, qb, TN)))
            for hh, ls in enumerate(lanes):
                dw_new, dlb_new, dsec, dst_new = done[hh]
                dw_ref[:, ls] = dw_new
                dlb_ref[:, ls] = dlb_new
                for s in range(4):
                    dp_ref[s, rows, ls] = dsec[s]
                dS[hh] = dst_new
            return carry

        lax.fori_loop(0, NCB, chunk, 0)

    rev = lambda t: NB - 1 - t
    return _pcall(
        body, [proj6, o, states, dcat, lb_param, a_norm_w], name="hgrn2_bwd", grid=(H // HS, NB),
        in_specs=[pl.BlockSpec((4, TB, W2), lambda h, t: (0, rev(t), h)),
                  pl.BlockSpec((TB, W2), lambda h, t: (rev(t), h)),
                  pl.BlockSpec((HS, NCB, HEAD, HEAD), lambda h, t: (h, rev(t), 0, 0)),
                  pl.BlockSpec((None, TB, W2), lambda h, t: (0, rev(t), h)),
                  pl.BlockSpec((2, W2), lambda h, t: (0, h)),
                  pl.BlockSpec((1, W2), lambda h, t: (0, h))],
        out_specs=[pl.BlockSpec((4, TB, W2), lambda h, t: (0, rev(t), h)),
                   pl.BlockSpec((1, W2), lambda h, t: (0, h)),
                   pl.BlockSpec((1, W2), lambda h, t: (0, h))],
        out_shape=[jax.ShapeDtypeStruct((6, T, Wd), BF16), jax.ShapeDtypeStruct((1, Wd), F32),
                   jax.ShapeDtypeStruct((1, Wd), F32)],
        scratch_shapes=[pltpu.VMEM((HS, HEAD, HEAD), F32)],
        compiler_params=_params(("parallel", "arbitrary")),
    )


_INV_SQRT2 = 1.0 / math.sqrt(2.0)
_INV_SQRT2PI = 1.0 / math.sqrt(2.0 * math.pi)


def _gelu(v):
    return 0.5 * v * (1.0 + lax.erf(v * _INV_SQRT2))


def _gelu_grad(v):
    return 0.5 * (1.0 + lax.erf(v * _INV_SQRT2)) + v * jnp.exp(-0.5 * v * v) * _INV_SQRT2PI


def _gmlp_norm(v, ln_w, ln_b):
    vf = _gelu(v)
    mu = jnp.mean(vf, axis=-1, keepdims=True)
    cen = vf - mu
    rstd = lax.rsqrt(jnp.mean(cen * cen, axis=-1, keepdims=True) + EPS)
    vh = cen * rstd
    return vh, rstd, vh * ln_w + ln_b


def _tril_mask():
    n = GMLP_CHUNK
    return lax.broadcasted_iota(jnp.int32, (n, n), 0) >= lax.broadcasted_iota(jnp.int32, (n, n), 1)


def _gmlp_fwd(proj6, cat, ln_w, ln_b, w_s, b_st):
    _, T, Wd = proj6.shape
    G, N = Wd // HEAD, GMLP_CHUNK

    def body(p_ref, cat_in, lnw_ref, lnb_ref, ws_ref, bst_ref, out_ref):
        del cat_in
        _, _, vn = _gmlp_norm(p_ref[1], lnw_ref[...], lnb_ref[...])
        tril = _tril_mask()
        for g in range(G):
            cols = slice(g * HEAD, (g + 1) * HEAD)
            z = _bdot(jnp.where(tril, ws_ref[g], 0.0), vn[:, cols], NN) + bst_ref[:, g:g + 1]
            out_ref[:, cols] = (_gelu(p_ref[0, :, cols]) * z).astype(BF16)

    return _pcall(
        body, [proj6, cat, ln_w, ln_b, w_s, b_st], name="gmlp_fwd", grid=(T // N,),
        in_specs=[pl.BlockSpec((2, N, Wd), lambda t: (2, t, 0)), pl.BlockSpec(memory_space=pl.ANY),
                  pl.BlockSpec((1, Wd), lambda t: (0, 0)), pl.BlockSpec((1, Wd), lambda t: (0, 0)),
                  pl.BlockSpec((G, N, N), lambda t: (0, 0, 0)), pl.BlockSpec((N, LANES), lambda t: (0, 0))],
        out_specs=pl.BlockSpec((None, N, Wd), lambda t: (1, t, 0)),
        out_shape=jax.ShapeDtypeStruct(cat.shape, cat.dtype),
        input_output_aliases={1: 0},
        compiler_params=_params(("arbitrary",)),
    )


def _gmlp_bwd(proj6, dproj6, dcat, ln_w, ln_b, w_s, b_st):
    _, T, Wd = proj6.shape
    G, N = Wd // HEAD, GMLP_CHUNK

    def body(p_ref, dp_in, db_ref, lnw_ref, lnb_ref, ws_ref, bst_ref, dp_ref, dlnw_ref, dlnb_ref, dws_ref, dbst_ref, dvn):
        del dp_in

        @pl.when(pl.program_id(0) == 0)
        def _():
            dlnw_ref[...] = jnp.zeros_like(dlnw_ref)
            dlnb_ref[...] = jnp.zeros_like(dlnb_ref)
            dws_ref[...] = jnp.zeros_like(dws_ref)
            dbst_ref[...] = jnp.zeros_like(dbst_ref)

        lnw = lnw_ref[...]
        v = p_ref[1]
        vh, rstd, vn = _gmlp_norm(v, lnw, lnb_ref[...])
        tril = _tril_mask()
        lane = lax.broadcasted_iota(jnp.int32, (N, LANES), 1)
        dbst = jnp.zeros((N, LANES), F32)
        for g in range(G):
            cols = slice(g * HEAD, (g + 1) * HEAD)
            wt = jnp.where(tril, ws_ref[g], 0.0)
            vn_g = vn[:, cols]
            z = _bdot(wt, vn_g, NN) + bst_ref[:, g:g + 1]
            u = p_ref[0, :, cols]
            db = db_ref[:, cols]
            dp_ref[0, :, cols] = (db * z * _gelu_grad(u)).astype(BF16)
            dz = db * _gelu(u)
            dbst = dbst + jnp.where(lane == g, jnp.sum(dz, axis=1, keepdims=True), 0.0)
            dws_ref[g] += jnp.where(tril, _bdot(dz, vn_g, NT), 0.0)
            dvn[:, cols] = _bdot(wt, dz, TN)
        dbst_ref[...] += dbst
        dv = dvn[...]
        dlnw_ref[...] += jnp.sum(dv * vh, axis=0, keepdims=True)
        dlnb_ref[...] += jnp.sum(dv, axis=0, keepdims=True)
        dvh = dv * lnw
        dvf = rstd * (dvh - jnp.mean(dvh, axis=-1, keepdims=True) - vh * jnp.mean(dvh * vh, axis=-1, keepdims=True))
        dp_ref[1] = (dvf * _gelu_grad(v)).astype(BF16)

    vec = pl.BlockSpec((1, Wd), lambda t: (0, 0))
    return _pcall(
        body, [proj6, dproj6, dcat, ln_w, ln_b, w_s, b_st], name="gmlp_bwd", grid=(T // N,),
        in_specs=[pl.BlockSpec((2, N, Wd), lambda t: (2, t, 0)), pl.BlockSpec(memory_space=pl.ANY),
                  pl.BlockSpec((None, N, Wd), lambda t: (1, t, 0)), vec, vec,
                  pl.BlockSpec((G, N, N), lambda t: (0, 0, 0)), pl.BlockSpec((N, LANES), lambda t: (0, 0))],
        out_specs=[pl.BlockSpec((2, N, Wd), lambda t: (2, t, 0)), vec, vec,
                   pl.BlockSpec((G, N, N), lambda t: (0, 0, 0)), pl.BlockSpec((N, LANES), lambda t: (0, 0))],
        out_shape=[jax.ShapeDtypeStruct(dproj6.shape, dproj6.dtype), jax.ShapeDtypeStruct((1, Wd), F32),
                   jax.ShapeDtypeStruct((1, Wd), F32), jax.ShapeDtypeStruct((G, N, N), F32),
                   jax.ShapeDtypeStruct((N, LANES), F32)],
        scratch_shapes=[pltpu.VMEM((N, Wd), F32)],
        input_output_aliases={1: 0},
        compiler_params=_params(("arbitrary",)),
    )


def _place():
    return lax.axis_index("x"), lax.axis_index("y"), lax.axis_index("c")


def _other_chips(x, y):
    return [(1 - x, y), (x, 1 - y), (1 - x, 1 - y)]


def _rcopy(src, dst, ssem, rsem, k, to):
    return pltpu.make_async_remote_copy(src_ref=src, dst_ref=dst, send_sem=ssem.at[k], recv_sem=rsem.at[k],
                                        device_id=to, device_id_type=MESH)


def _cast_slot(name, w, place):
    _, R, C = w.shape
    tr = _row_block(R, C)

    def body(p_ref, w_ref, o_ref):
        del p_ref
        o_ref[...] = w_ref[...].astype(BF16)

    return _pcall(
        body, [place[0], w], name=name, prefetch=1, grid=(R // tr,),
        in_specs=[pl.BlockSpec((None, tr, C), lambda i, p: (0, i, 0))],
        out_specs=pl.BlockSpec((None, tr, C), lambda i, p: (p[0], i, 0)),
        out_shape=jax.ShapeDtypeStruct((4, R, C), BF16),
        compiler_params=_params(("parallel",)))


def _half(ref, px, py, pc, Rh):
    return ref.at[2 * px + py, pl.ds(pc * Rh, Rh), :]


def _split_start(name, bufs, n_sem, copies):
    nb = len(bufs)

    def body(*refs):
        for cp in copies(refs[:nb], refs[nb], refs[nb + 1], False):
            cp.start()
        refs[-1][...] = jnp.zeros_like(refs[-1])

    outs = _pcall(
        body, [pltpu.with_memory_space_constraint(b, pltpu.HBM) for b in bufs], same=bufs, token=-1, name=name,
        in_specs=[HBM_SPEC] * nb, out_specs=(SEM_SPEC, SEM_SPEC, *[HBM_SPEC] * nb, VMEM_SPEC),
        out_shape=(pltpu.SemaphoreType.DMA((n_sem,)), pltpu.SemaphoreType.DMA((n_sem,)),
                   *[pltpu.HBM(b.shape, b.dtype) for b in bufs], jax.ShapeDtypeStruct((8, LANES), F32)),
        input_output_aliases={i: 2 + i for i in range(nb)},
        compiler_params=pltpu.CompilerParams(has_side_effects=SIDE_EFFECT))
    return outs[:-1]


def _split_wait(name, started, copies):
    ssem, rsem, *bufs = started
    nb = len(bufs)

    def body(*refs):
        for cp in copies(refs[:nb], refs[nb], refs[nb + 1], True):
            cp.wait_send()
            cp.wait_recv()

    outs = _pcall(
        body, [*bufs, ssem, rsem], name=name, in_specs=[HBM_SPEC] * nb + [SEM_SPEC, SEM_SPEC],
        out_specs=tuple([HBM_SPEC] * nb), out_shape=tuple(pltpu.HBM(b.shape, b.dtype) for b in bufs),
        input_output_aliases={i: i for i in range(nb)},
        compiler_params=pltpu.CompilerParams(has_side_effects=SIDE_EFFECT))
    return outs


def _gather_copies(Rh):
    def copies(refs, ssem, rsem, waiting):
        x, y, c = _place()
        mine = _half(refs[0], x, y, c, Rh)
        return [_rcopy(mine, _half(refs[0], px, py, c, Rh) if waiting else mine, ssem, rsem, j, (px, py, c))
                for j, (px, py) in enumerate(_other_chips(x, y))]
    return copies


def _swap_copies(Rh):
    def copies(refs, ssem, rsem, waiting):
        x, y, c = _place()
        return [_rcopy(refs[0].at[:, pl.ds((1 - c) * Rh, Rh), :], refs[1], ssem, rsem, 0, (x, y, 1 - c))]
    return copies


def _exchange_copies(refs, ssem, rsem, waiting):
    x, y, c = _place()
    return [_rcopy(refs[0].at[2 * px + py], refs[1].at[j], ssem, rsem, j, (px, py, c))
            for j, (px, py) in enumerate(_other_chips(x, y))]


def _landing(shape, dtype):
    return lax.empty(shape, dtype)


def _gather_forward(name, buf):
    _, R, C = buf.shape
    Rh = R // 2

    def body(in_ref, out_ref, ssem, rsem):
        x, y, c = _place()
        chips = _other_chips(x, y)
        passed = [_rcopy(_half(in_ref, px, py, c, Rh), _half(out_ref, px, py, c, Rh), ssem, rsem, j, (x, y, 1 - c))
                  for j, (px, py) in enumerate(chips)]
        for cp in passed:
            cp.start()
        for j, (px, py) in enumerate(chips):
            _rcopy(_half(in_ref, px, py, c, Rh), _half(out_ref, px, py, 1 - c, Rh), ssem, rsem, j, (x, y, 1 - c)).wait_recv()
        for cp in passed:
            cp.wait_send()

    return _pcall(
        body, [buf], name=name, in_specs=[HBM_SPEC], out_specs=HBM_SPEC,
        out_shape=jax.ShapeDtypeStruct(buf.shape, buf.dtype), input_output_aliases={0: 0},
        scratch_shapes=[pltpu.SemaphoreType.DMA((3,)), pltpu.SemaphoreType.DMA((3,))])


def _share_half(name, full):
    R, C = full.shape
    Rh = R // 2

    def body(in_ref, out_ref, ssem, rsem):
        x, y, c = _place()
        cp = _rcopy(in_ref.at[pl.ds(c * Rh, Rh), :], out_ref.at[pl.ds(c * Rh, Rh), :], ssem, rsem, 0, (x, y, 1 - c))
        cp.start()
        cp.wait()

    return _pcall(
        body, [full], name=name, in_specs=[HBM_SPEC], out_specs=HBM_SPEC,
        out_shape=jax.ShapeDtypeStruct((R, C), full.dtype), input_output_aliases={0: 0},
        scratch_shapes=[pltpu.SemaphoreType.DMA((1,)), pltpu.SemaphoreType.DMA((1,))])


def _sum_all_devices(v):
    R, L = v.shape

    def body(v_ref, out_ref, buf, ssem, rsem):
        x, y, c = _place()
        me = 4 * x + 2 * y + c
        buf[me] = v_ref[...]
        copies = []
        for m in range(1, 8):
            to = (x ^ (m >> 2), y ^ ((m >> 1) & 1), c ^ (m & 1))
            copies.append(_rcopy(v_ref, buf.at[me], ssem, rsem, m - 1, to))
        for cp in copies:
            cp.start()
        for cp in copies:
            cp.wait()
        acc = buf[0]
        for d in range(1, 8):
            acc = acc + buf[d]
        out_ref[...] = acc

    return _pcall(
        body, [v], name="sum_all_devices", in_specs=[VMEM_SPEC], out_specs=VMEM_SPEC,
        out_shape=jax.ShapeDtypeStruct((R, L), F32),
        scratch_shapes=[pltpu.VMEM((8, R, L), F32), pltpu.SemaphoreType.DMA((7,)), pltpu.SemaphoreType.DMA((7,))],
        compiler_params=pltpu.CompilerParams(vmem_limit_bytes=VMEM_LIMIT))


def _elementwise(name, body, ins, in_specs, outs, out_specs, grid):
    return _pcall(body, ins, name=name, grid=grid, in_specs=in_specs, out_specs=out_specs, out_shape=outs,
                  compiler_params=_params(("parallel",) * len(grid)))


def _row_block(R, C, bytes_per_row_elem=4, target=1 << 20):
    return _tile(R, max(16, target // (C * bytes_per_row_elem)), 16)


def _add_halves(name, g4, land, c_idx):
    S, R, C = g4.shape
    Rh = R // 2
    tr = _row_block(Rh, C)
    nb = Rh // tr

    def body(c_ref, g_ref, l_ref, o_ref):
        del c_ref
        o_ref[...] = (g_ref[...].astype(F32) + l_ref[...].astype(F32)).astype(BF16)

    return _pcall(
        body, [c_idx, g4, land], name=name, prefetch=1, grid=(S, nb),
        in_specs=[pl.BlockSpec((None, tr, C), lambda s, i, c: (s, c[0] * nb + i, 0)),
                  pl.BlockSpec((None, tr, C), lambda s, i, c: (s, i, 0))],
        out_specs=pl.BlockSpec((None, tr, C), lambda s, i, c: (s, i, 0)),
        out_shape=jax.ShapeDtypeStruct((S, Rh, C), BF16),
        compiler_params=_params(("parallel", "parallel")))


def _sum_chips(name, land, pb, place):
    S, Rh, C = land.shape
    tr = _row_block(Rh, C)
    nb = Rh // tr

    def body(p_ref, c_ref, l_ref, own_ref, o_ref):
        del p_ref, c_ref
        acc = own_ref[...].astype(F32)
        for s in range(S):
            acc = acc + l_ref[s].astype(F32)
        o_ref[...] = acc

    return _pcall(
        body, [place[0], place[1], land, pb], name=name, prefetch=2, grid=(nb,),
        in_specs=[pl.BlockSpec((S, tr, C), lambda i, p, c: (0, i, 0)),
                  pl.BlockSpec((None, tr, C), lambda i, p, c: (p[0], i, 0))],
        out_specs=pl.BlockSpec((tr, C), lambda i, p, c: (c[0] * nb + i, 0)),
        out_shape=jax.ShapeDtypeStruct((2 * Rh, C), F32),
        compiler_params=_params(("parallel",)))


def _adamw(name, w, g, m, v):
    R, C = g.shape
    tr = _row_block(R, C, target=1 << 19)
    c1 = 1.0 - ADAM_B1 ** ADAM_STEP
    c2 = 1.0 - ADAM_B2 ** ADAM_STEP

    def body(w_ref, g_ref, m_ref, v_ref, go_ref, d_ref, nm_ref, nv_ref):
        gv = g_ref[...]
        go_ref[...] = gv
        nm = ADAM_B1 * m_ref[...] + (1.0 - ADAM_B1) * gv
        nv = ADAM_B2 * v_ref[...] + (1.0 - ADAM_B2) * (gv * gv)
        nm_ref[...] = nm
        nv_ref[...] = nv
        d_ref[...] = -ADAM_LR * ((nm / c1) / (jnp.sqrt(nv / c2) + ADAM_EPS) + ADAM_WD * w_ref[...])

    spec = pl.BlockSpec((tr, C), lambda i: (i, 0))
    wspec = pl.BlockSpec((None, tr, C), lambda i: (0, i, 0)) if w.ndim == 3 else spec
    shp = jax.ShapeDtypeStruct((R, C), F32)
    return _elementwise(name, body, [w, g, m, v], [wspec, spec, wspec, wspec], [shp] * 4, [spec] * 4, (R // tr,))


def _lb_grad(dlb, lb_param):
    def body(d_ref, p_ref, o_ref):
        lb = _lower_bound(p_ref[...])
        t = d_ref[...] * lb * (1.0 - lb)
        o_ref[0:1, :] = t
        o_ref[1:2, :] = -t

    return _pcall(body, [dlb, lb_param], name="lb_grad", in_specs=[VMEM_SPEC, VMEM_SPEC], out_specs=VMEM_SPEC,
                  out_shape=jax.ShapeDtypeStruct(lb_param.shape, F32))


class _Reduction:
    def __init__(self, tag, g4, place):
        self.tag, self.place = tag, place
        _, R, C = g4.shape
        self.Rh = R // 2
        self.state = _split_start("swap_start_" + tag, [g4, _landing((4, self.Rh, C), g4.dtype)], 1, _swap_copies(self.Rh))

    def exchange(self):
        g4, land = _split_wait("swap_wait_" + self.tag, self.state, _swap_copies(self.Rh))
        pb = _add_halves("addh_" + self.tag, g4, land, self.place[1])
        self.state = _split_start("xchg_start_" + self.tag, [pb, _landing((3,) + pb.shape[1:], BF16)], 3, _exchange_copies)

    def finish(self, w, m, v):
        pb, got = _split_wait("xchg_wait_" + self.tag, self.state, _exchange_copies)
        full = _share_half("share_" + self.tag, _sum_chips("sumc_" + self.tag, got, pb, self.place))
        return tuple(a[None] for a in _adamw("adamw_" + self.tag, w, full, m, v))


def kernel(x, p, pre_mix_w, w_in, lb_param, a_norm_w, gmlp_ln_w, gmlp_ln_b, w_spatial, b_spatial, w_out, post_mix_w, pre_ffn_w, w_gate, w_up, w_down, post_ffn_w, w_ple, w_ple_gate, post_ple_w, loss_target, m_pre_mix_w, m_w_in, m_lb_param, m_a_norm_w, m_gmlp_ln_w, m_gmlp_ln_b, m_w_spatial, m_b_spatial, m_w_out, m_post_mix_w, m_pre_ffn_w, m_w_gate, m_w_up, m_w_down, m_post_ffn_w, m_w_ple, m_w_ple_gate, m_post_ple_w, v_pre_mix_w, v_w_in, v_lb_param, v_a_norm_w, v_gmlp_ln_w, v_gmlp_ln_b, v_w_spatial, v_b_spatial, v_w_out, v_post_mix_w, v_pre_ffn_w, v_w_gate, v_w_up, v_w_down, v_post_ffn_w, v_w_ple, v_w_ple_gate, v_post_ple_w):
    T, D = x.shape[1], x.shape[2]
    Wd = D // 2
    G = Wd // HEAD
    NCH = 4
    Ci = w_in.shape[2]
    Fs = w_gate.shape[2]
    Dq = w_out.shape[1]
    Pd = w_ple.shape[1]
    xs, ps, tgt = x[0], p[0, 0], loss_target[0]
    c_idx = lax.axis_index("c").astype(jnp.int32).reshape((1,))
    place = ((2 * lax.axis_index("x") + lax.axis_index("y")).astype(jnp.int32).reshape((1,)), c_idx)

    weights = dict(w_in=w_in, w_out=w_out, w_gate=w_gate, w_up=w_up, w_down=w_down, w_ple_gate=w_ple_gate, w_ple=w_ple)
    _Order.last = None
    gathering = {}
    for tag, w in weights.items():
        buf = _cast_slot("cast_" + tag, w, place)
        gathering[tag] = _split_start("gather_start_" + tag, [buf], 3, _gather_copies(buf.shape[1] // 2))

    def gathered(tag):
        (buf,) = _split_wait("gather_wait_" + tag, gathering.pop(tag), _gather_copies(weights[tag].shape[1] // 2))
        return _gather_forward("gather_fwd_" + tag, buf)

    res = {}
    moments = dict(w_in=(m_w_in, v_w_in), w_out=(m_w_out, v_w_out), w_gate=(m_w_gate, v_w_gate), w_up=(m_w_up, v_w_up),
                   w_down=(m_w_down, v_w_down), w_ple=(m_w_ple, v_w_ple), w_ple_gate=(m_w_ple_gate, v_w_ple_gate))
    reducing = {}

    def reduce_start(tag, g4):
        reducing[tag] = _Reduction(tag, g4, place)

    def reduce_exchange(tag):
        reducing[tag].exchange()

    def reduce_finish(tag):
        res[tag] = reducing.pop(tag).finish(weights[tag], *moments[tag])

    tm, tn, tk = _tile(T, 1024), _tile(D, 1024), _tile(Wd, 1024)
    ts = _tile(math.gcd(Wd, Ci), 1024)
    tkt = _tile(T, 1024)
    n_sec, n_shd = Wd // ts, Ci // ts
    b_st = jnp.pad(b_spatial[0].T, ((0, 0), (0, LANES - G)))

    h1 = _norm_in(xs, pre_mix_w)
    gw_in = gathered("w_in")
    proj6 = _mm("proj", [h1, gw_in],
                [pl.BlockSpec((tm, tk), lambda i, j, k: (i, k)),
                 pl.BlockSpec((None, tk, ts), lambda i, j, k: (j // n_shd, k, j % n_shd))],
                jax.ShapeDtypeStruct((6, T, Wd), F32),
                pl.BlockSpec((None, tm, ts), lambda i, j, k: (j // n_sec, i, j % n_sec)),
                (tm, ts), (T // tm, 6 * n_sec, D // tk), NN)
    cat, o_a, states = _hgrn2_fwd(proj6, lb_param, a_norm_w)
    cat = _gmlp_fwd(proj6, cat, gmlp_ln_w, gmlp_ln_b, w_spatial[0], b_st)
    nkc = Wd // tk
    gw_out = gathered("w_out").reshape(D, D)
    mix = _mm("mix", [cat, gw_out],
              [pl.BlockSpec((None, tm, tk), lambda i, j, k: (k // nkc, i, k % nkc)),
               pl.BlockSpec((tk, tn), lambda i, j, k: (k, j))],
              jax.ShapeDtypeStruct((T, D), F32), pl.BlockSpec((tm, tn), lambda i, j, k: (i, j)),
              (tm, tn), (T // tm, D // tn, D // tk), NN)
    x1, h2 = _mid1(xs, mix, post_mix_w, pre_ffn_w)
    gw_gate = gathered("w_gate")
    gw_up = gathered("w_up")
    gt, up, act = _ffn_up(h2, gw_gate, gw_up)
    tnf = _tile(D, 512)
    gw_down = gathered("w_down")
    ff = _mm("ffn_down", [act, gw_down],
             [pl.BlockSpec((None, tm, Fs), lambda i, j, k: (k, i, 0)),
              pl.BlockSpec((None, Fs, tnf), lambda i, j, k: (k, 0, j))],
             jax.ShapeDtypeStruct((T, D), F32), pl.BlockSpec((tm, tnf), lambda i, j, k: (i, j)),
             (tm, tnf), (T // tm, D // tnf, NCH), NN)
    x2, x2b = _mid2(x1, ff, post_ffn_w)
    gw_pg = gathered("w_ple_gate").reshape(D, D)
    gl = _mm("ple_gate", [x2b, gw_pg],
             [pl.BlockSpec((tm, tk), lambda i, j, k: (i, k)), pl.BlockSpec((tk, tn), lambda i, j, k: (k, j))],
             jax.ShapeDtypeStruct((T, D), F32), pl.BlockSpec((tm, tn), lambda i, j, k: (i, j)),
             (tm, tn), (T // tm, D // tn, D // tk), NN)
    tq = _tile(Dq, 1024)
    nq = Dq // tq
    gw_ple = gathered("w_ple")
    e = _mm("ple_embed", [ps, gw_ple],
            [pl.BlockSpec((tm, Pd), lambda i, j, k: (i, 0)),
             pl.BlockSpec((None, Pd, tq), lambda i, j, k: (j // nq, 0, j % nq))],
            jax.ShapeDtypeStruct((T, D), F32), pl.BlockSpec((tm, tq), lambda i, j, k: (i, j)),
            (tm, tq), (T // tm, D // tq, 1), NN)

    d3, de, dgl, dw_pp, loss_vec = _head(x2, gl, e, post_ple_w, tgt)
    g_ple = _mm("d_w_ple", [ps, de],
                [pl.BlockSpec((tkt, Pd), lambda i, j, k: (k, 0)), pl.BlockSpec((tkt, tq), lambda i, j, k: (k, j))],
                jax.ShapeDtypeStruct((NCH, Pd, Dq), PARTIAL),
                pl.BlockSpec((None, Pd, tq), lambda i, j, k: (j // nq, 0, j % nq)),
                (Pd, tq), (1, D // tq, T // tkt), TN)
    reduce_start("w_ple", g_ple)
    tmw = _tile(D, 1024)
    g_pg = _mm("d_w_ple_gate", [x2b, dgl],
               [pl.BlockSpec((tkt, tq), lambda i, j, k: (k, i)), pl.BlockSpec((tkt, tn), lambda i, j, k: (k, j))],
               jax.ShapeDtypeStruct((NCH, Dq, D), PARTIAL), pl.BlockSpec((None, tq, tn), lambda i, j, k: (i // nq, i % nq, j)),
               (tq, tn), (D // tq, D // tn, T // tkt), TN)
    reduce_start("w_ple_gate", g_pg)
    reduce_exchange("w_ple")
    dx2 = _mm("d_x2", [dgl, gw_pg],
              [pl.BlockSpec((tm, tk), lambda i, j, k: (i, k)), pl.BlockSpec((tn, tk), lambda i, j, k: (j, k))],
              jax.ShapeDtypeStruct((T, D), F32), pl.BlockSpec((tm, tn), lambda i, j, k: (i, j)),
              (tm, tn), (T // tm, D // tn, D // tk), NT,
              add=d3, add_spec=pl.BlockSpec((tm, tn), lambda i, j, k: (i, j)))
    reduce_exchange("w_ple_gate")
    dff, dw_pff = _bwd_ffn_norm(dx2, ff, post_ffn_w)
    g_down = _mm("d_w_down", [act, dff],
                 [pl.BlockSpec((None, tkt, Fs), lambda i, j, k: (i, k, 0)),
                  pl.BlockSpec((tkt, tnf), lambda i, j, k: (k, j))],
                 jax.ShapeDtypeStruct((NCH, Fs, D), PARTIAL), pl.BlockSpec((None, Fs, tnf), lambda i, j, k: (i, 0, j)),
                 (Fs, tnf), (NCH, D // tnf, T // tkt), TN)
    reduce_start("w_down", g_down)
    reduce_finish("w_ple")
    dgt, dup = _ffn_dact(dff, gw_down, gt, up)
    reduce_exchange("w_down")
    reduce_finish("w_ple_gate")
    tmf = _tile(D, 512)
    wgrad_specs = [pl.BlockSpec((tkt, tmf), lambda i, j, k: (k, i)),
                   pl.BlockSpec((None, tkt, Fs), lambda i, j, k: (j, k, 0))]
    wgrad_out = pl.BlockSpec((None, tmf, Fs), lambda i, j, k: (j, i, 0))
    g_gate = _mm("d_w_gate", [h2, dgt], wgrad_specs, jax.ShapeDtypeStruct((NCH, D, Fs), PARTIAL), wgrad_out,
                 (tmf, Fs), (D // tmf, NCH, T // tkt), TN)
    reduce_start("w_gate", g_gate)
    g_up = _mm("d_w_up", [h2, dup], wgrad_specs, jax.ShapeDtypeStruct((NCH, D, Fs), PARTIAL), wgrad_out,
               (tmf, Fs), (D // tmf, NCH, T // tkt), TN)
    reduce_start("w_up", g_up)
    reduce_exchange("w_gate")
    tmh = _tile(T, 1024)
    a_sp =pl.BlockSpec((None, tmh, Fs), lambda i, j, k: (k, i, 0))
    b_sp = pl.BlockSpec((None, tnf, Fs), lambda i, j, k: (k, j, 0))
    dh2 = _mm("d_h2", [dgt, gw_gate, dup, gw_up], [a_sp, b_sp, a_sp, b_sp],
              jax.ShapeDtypeStruct((T, D), F32), pl.BlockSpec((tmh, tnf), lambda i, j, k: (i, j)),
              (tmh, tnf), (T // tmh, D // tnf, NCH), NT)
    reduce_exchange("w_up")
    reduce_finish("w_down")
    dx1, dmix, dw_pf, dw_pm = _bwd_mid(dx2, dh2, x1, pre_ffn_w, mix, post_mix_w)
    tw = _tile(Wd, 1024)
    nw = Wd // tw
    nwq = Wd // tq
    g_out = _mm("d_w_out", [cat, dmix],
                [pl.BlockSpec((None, tkt, tq), lambda i, j, k: (i // nwq, k, i % nwq)),
                 pl.BlockSpec((tkt, tn), lambda i, j, k: (k, j))],
                jax.ShapeDtypeStruct((NCH, Dq, D), PARTIAL), pl.BlockSpec((None, tq, tn), lambda i, j, k: (i // nq, i % nq, j)),
                (tq, tn), (D // tq, D // tn, T // tkt), TN)
    reduce_start("w_out", g_out)
    reduce_finish("w_gate")
    dcat = _mm("d_cat", [dmix, gw_out],
               [pl.BlockSpec((tm, tk), lambda i, j, k: (i, k)), pl.BlockSpec((tw, tk), lambda i, j, k: (j, k))],
               jax.ShapeDtypeStruct((2, T, Wd), F32), pl.BlockSpec((None, tm, tw), lambda i, j, k: (j // nw, i, j % nw)),
               (tm, tw), (T // tm, D // tw, D // tk), NT)
    reduce_exchange("w_out")
    dproj6, dw_an, dlb = _hgrn2_bwd(proj6, o_a, states, dcat, lb_param, a_norm_w)
    reduce_finish("w_up")
    dproj6, dw_lnw, dw_lnb, dw_sp, dw_bst = _gmlp_bwd(proj6, dproj6, dcat, gmlp_ln_w, gmlp_ln_b, w_spatial[0], b_st)
    g_in = _mm("d_w_in", [h1, dproj6],
               [pl.BlockSpec((tkt, tmw), lambda i, j, k: (k, i)),
                pl.BlockSpec((None, tkt, ts), lambda i, j, k: (j // n_sec, k, j % n_sec))],
               jax.ShapeDtypeStruct((NCH, D, Ci), PARTIAL),
               pl.BlockSpec((None, tmw, ts), lambda i, j, k: (j // n_shd, i, j % n_shd)),
               (tmw, ts), (D // tmw, 6 * n_sec, T // tkt), TN)
    reduce_start("w_in", g_in)
    reduce_finish("w_out")
    reduce_exchange("w_in")
    dh1 = _mm("d_h1", [dproj6, gw_in],
              [pl.BlockSpec((None, tm, ts), lambda i, j, k: (k // n_sec, i, k % n_sec)),
               pl.BlockSpec((None, tn, ts), lambda i, j, k: (k // n_shd, j, k % n_shd))],
              jax.ShapeDtypeStruct((T, D), F32), pl.BlockSpec((tm, tn), lambda i, j, k: (i, j)),
              (tm, tn), (T // tm, D // tn, 6 * n_sec), NT)
    grad_x, dw_pre = _bwd_in(dx1, dh1, xs, pre_mix_w)

    small = [("pre_mix_w", dw_pre), ("lb", dlb), ("a_norm_w", dw_an), ("gmlp_ln_w", dw_lnw), ("gmlp_ln_b", dw_lnb),
             ("w_spatial", dw_sp), ("b_spatial_t", dw_bst), ("post_mix_w", dw_pm), ("pre_ffn_w", dw_pf),
             ("post_ffn_w", dw_pff), ("post_ple_w", dw_pp), ("loss", loss_vec)]
    rows = [a.size // LANES for _, a in small]
    total = sum(rows)
    pad = (-total) % 8
    packed = jnp.concatenate([a.reshape(-1, LANES) for _, a in small] + [jnp.zeros((pad, LANES), F32)], axis=0)
    summed = _sum_all_devices(packed)
    off, piece = 0, {}
    for (nm, a), r in zip(small, rows):
        piece[nm] = summed[off:off + r].reshape(a.shape)
        off += r
    loss = jnp.sum(piece["loss"])
    g_small = {
        "pre_mix_w": piece["pre_mix_w"], "lb_param": _lb_grad(piece["lb"], lb_param), "a_norm_w": piece["a_norm_w"],
        "gmlp_ln_w": piece["gmlp_ln_w"], "gmlp_ln_b": piece["gmlp_ln_b"], "w_spatial": piece["w_spatial"][None],
        "b_spatial": piece["b_spatial_t"][:, :G].T[None], "post_mix_w": piece["post_mix_w"],
        "pre_ffn_w": piece["pre_ffn_w"], "post_ffn_w": piece["post_ffn_w"], "post_ple_w": piece["post_ple_w"],
    }
    w_small = dict(pre_mix_w=(pre_mix_w, m_pre_mix_w, v_pre_mix_w), lb_param=(lb_param, m_lb_param, v_lb_param),
                   a_norm_w=(a_norm_w, m_a_norm_w, v_a_norm_w), gmlp_ln_w=(gmlp_ln_w, m_gmlp_ln_w, v_gmlp_ln_w),
                   gmlp_ln_b=(gmlp_ln_b, m_gmlp_ln_b, v_gmlp_ln_b), w_spatial=(w_spatial, m_w_spatial, v_w_spatial),
                   b_spatial=(b_spatial, m_b_spatial, v_b_spatial), post_mix_w=(post_mix_w, m_post_mix_w, v_post_mix_w),
                   pre_ffn_w=(pre_ffn_w, m_pre_ffn_w, v_pre_ffn_w), post_ffn_w=(post_ffn_w, m_post_ffn_w, v_post_ffn_w),
                   post_ple_w=(post_ple_w, m_post_ple_w, v_post_ple_w))
    names_small = list(w_small)

    def pack(arrs):
        flat = jnp.concatenate([a.reshape(-1, LANES) for a in arrs], axis=0)
        return jnp.pad(flat, ((0, (-flat.shape[0]) % 16), (0, 0)))

    pk = [pack([w_small[n][i] for n in names_small]) for i in range(3)]
    pg = pack([g_small[n] for n in names_small])
    _, sd, sm, sv = _adamw("adamw_small", pk[0], pg, pk[1], pk[2])
    off = 0
    for n in names_small:
        shp = w_small[n][0].shape
        r = w_small[n][0].size // LANES
        res[n] = (g_small[n].reshape(shp), sd[off:off + r].reshape(shp), sm[off:off + r].reshape(shp),
                  sv[off:off + r].reshape(shp))
        off += r

    reduce_finish("w_in")

    order =["pre_mix_w", "w_in", "lb_param", "a_norm_w", "gmlp_ln_w", "gmlp_ln_b", "w_spatial", "b_spatial", "w_out",
             "post_mix_w", "pre_ffn_w", "w_gate", "w_up", "w_down", "post_ffn_w", "w_ple", "w_ple_gate", "post_ple_w"]
    return (loss, grad_x[None], *[res[n][0] for n in order], *[res[n][1] for n in order],
            *[res[n][2] for n in order], *[res[n][3] for n in order])
```

```python
import functools
import math

import jax
import jax.numpy as jnp
from jax import lax
from jax.experimental import pallas as pl
from jax.experimental.pallas import tpu as pltpu

F32 = jnp.float32
BF16 = jnp.bfloat16
PARTIAL = jnp.bfloat16
EPS = 1e-6
HEAD = 128
GLA_CHUNK = 64
HEADS_PER_STEP = 2
GMLP_CHUNK = 128
LANES = 128
VMEM_LIMIT = 48 * 1024 * 1024
ADAM_LR, ADAM_B1, ADAM_B2, ADAM_EPS, ADAM_WD, ADAM_STEP = 0.001, 0.9, 0.999, 1e-08, 0.01, 10
MESH = pl.DeviceIdType.MESH
HBM_SPEC = pl.BlockSpec(memory_space=pltpu.HBM)
VMEM_SPEC = pl.BlockSpec(memory_space=pltpu.VMEM)
SEM_SPEC = pl.BlockSpec(memory_space=pltpu.SEMAPHORE)
SIDE_EFFECT = pltpu.SideEffectType.DATAFLOW_SIDE_EFFECTING

NN = ((1,), (0,))
NT = ((1,), (1,))
TN = ((0,), (0,))


def _tile(n, pref, mult=LANES):
    if n <= pref:
        return n
    t = (pref // mult) * mult
    while t >= mult:
        if n % t == 0:
            return t
        t -= mult
    return n


def _params(sem):
    return pltpu.CompilerParams(dimension_semantics=sem, vmem_limit_bytes=VMEM_LIMIT)


class _Order:
    last = None


def _pcall(body, args, *, token=0, prefetch=0, grid=(), in_specs, out_specs, scratch_shapes=(), same=(), **kw):
    args, in_specs = list(args), list(in_specs)
    n = len(args)
    run = body
    if _Order.last is not None and not any(_Order.last is a for a in [*args, *same]):
        def run(*refs):
            return body(*refs[:n], *refs[n + 1:])
        args.append(_Order.last)
        in_specs.append(pl.BlockSpec(memory_space=pl.ANY))
    if prefetch:
        kw["grid_spec"] = pltpu.PrefetchScalarGridSpec(num_scalar_prefetch=prefetch, grid=grid, in_specs=in_specs,
                                                       out_specs=out_specs, scratch_shapes=list(scratch_shapes))
    else:
        kw.update(grid=grid, in_specs=in_specs, out_specs=out_specs, scratch_shapes=list(scratch_shapes))
    outs = pl.pallas_call(run, **kw)(*args)
    _Order.last = outs[token] if isinstance(outs, (tuple, list)) else outs
    return outs


def _dot(a, b, dims, precision=None):
    return lax.dot_general(a, b, (dims, ((), ())), preferred_element_type=F32, precision=precision)


def _bdot(a, b, dims):
    return _dot(a.astype(BF16), b.astype(BF16), dims)


def _fdot(a, b, dims):
    return _dot(a, b, dims, precision=lax.Precision.HIGHEST)


def _sigmoid(v):
    return 1.0 / (1.0 + jnp.exp(-v))


def _mm(name, operands, specs, out_shape, out_spec, acc_shape, grid, dims, add=None, add_spec=None):
    n_pairs = len(operands) // 2
    nk = grid[2]

    def body(*refs):
        ins = refs[:2 * n_pairs]
        r_ref = refs[2 * n_pairs] if add is not None else None
        o_ref, acc = refs[-2], refs[-1]
        k = pl.program_id(2)

        @pl.when(k == 0)
        def _():
            acc[...] = jnp.zeros_like(acc)

        part = _bdot(ins[0][...], ins[1][...], dims)
        for q in range(1, n_pairs):
            part = part + _bdot(ins[2 * q][...], ins[2 * q + 1][...], dims)
        acc[...] += part

        @pl.when(k == nk - 1)
        def _():
            r = acc[...]
            if r_ref is not None:
                r = r + r_ref[...].astype(F32)
            o_ref[...] = r.astype(o_ref.dtype)

    in_specs = list(specs) + ([add_spec] if add is not None else [])
    args = list(operands) + ([add] if add is not None else [])
    return _pcall(
        body, args, name=name, grid=grid, in_specs=in_specs, out_specs=out_spec, out_shape=out_shape,
        scratch_shapes=[pltpu.VMEM(acc_shape, F32)],
        compiler_params=_params(("parallel", "parallel", "arbitrary")),
    )


def _rms(v, w):
    r = lax.rsqrt(jnp.mean(v * v, axis=-1, keepdims=True) + EPS)
    return v * r * w


def _rms_bwd(v, w, dy):
    r = lax.rsqrt(jnp.mean(v * v, axis=-1, keepdims=True) + EPS)
    vh = v * r
    gy = dy * w
    dv = r * (gy - vh * jnp.mean(gy * vh, axis=-1, keepdims=True))
    return dv, jnp.sum(dy * vh, axis=0, keepdims=True)


def _rowwise(name, body, ins, in_kinds, outs, out_kinds, T, D):
    tr = _tile(T, 128, 16)
    row = pl.BlockSpec((tr, D), lambda i: (i, 0))
    vec = pl.BlockSpec((1, D), lambda i: (0, 0))
    pick = {"row": row, "vec": vec}
    return _pcall(
        body, ins, name=name, grid=(T // tr,),
        in_specs=[pick[k] for k in in_kinds], out_specs=[pick[k] for k in out_kinds], out_shape=outs,
        compiler_params=_params(("arbitrary",)),
    )


def _acc_vec(ref, val):
    @pl.when(pl.program_id(0) == 0)
    def _():
        ref[...] = jnp.zeros_like(ref)
    ref[...] += val


def _norm_in(x, w):
    T, D = x.shape

    def body(x_ref, w_ref, h_ref):
        h_ref[...] = _rms(x_ref[...], w_ref[...]).astype(BF16)

    return _rowwise("norm_in", body, [x, w], ["row", "vec"], [jax.ShapeDtypeStruct((T, D), BF16)], ["row"], T, D)[0]


def _mid1(x, mix, w_pm, w_pf):
    T, D = x.shape

    def body(x_ref, mix_ref, wpm_ref, wpf_ref, x1_ref, h2_ref):
        x1 = x_ref[...] + _rms(mix_ref[...], wpm_ref[...])
        x1_ref[...] = x1
        h2_ref[...] = _rms(x1, wpf_ref[...]).astype(BF16)

    return _rowwise("mid1", body, [x, mix, w_pm, w_pf], ["row", "row", "vec", "vec"],
                    [jax.ShapeDtypeStruct((T, D), F32), jax.ShapeDtypeStruct((T, D), BF16)], ["row", "row"], T, D)


def _mid2(x1, ff, w_pff):
    T, D = x1.shape

    def body(x1_ref, ff_ref, w_ref, x2_ref, x2b_ref):
        x2 = x1_ref[...] + _rms(ff_ref[...], w_ref[...])
        x2_ref[...] = x2
        x2b_ref[...] = x2.astype(BF16)

    return _rowwise("mid2", body, [x1, ff, w_pff], ["row", "row", "vec"],
                    [jax.ShapeDtypeStruct((T, D), F32), jax.ShapeDtypeStruct((T, D), BF16)], ["row", "row"], T, D)


def _head(x2, gl, e, w_pp, tgt):
    T, D = x2.shape

    def body(x2_ref, gl_ref, e_ref, w_ref, t_ref, d3_ref, de_ref, dgl_ref, dw_ref, loss_ref):
        ev = e_ref[...]
        gate = _sigmoid(gl_ref[...])
        pe = ev * gate
        w = w_ref[...]
        err = x2_ref[...] + _rms(pe, w) - t_ref[...]
        d3 = err * (1.0 / D)
        d3_ref[...] = d3
        dpe, dw = _rms_bwd(pe, w, d3)
        de_ref[...] = (dpe * gate).astype(BF16)
        dgl_ref[...] = (dpe * ev * gate * (1.0 - gate)).astype(BF16)
        _acc_vec(dw_ref, dw)
        _acc_vec(loss_ref, jnp.sum(err * err, axis=0, keepdims=True) * (0.5 / D))

    return _rowwise("head", body, [x2, gl, e, w_pp, tgt], ["row", "row", "row", "vec", "row"],
                    [jax.ShapeDtypeStruct((T, D), F32), jax.ShapeDtypeStruct((T, D), BF16),
                     jax.ShapeDtypeStruct((T, D), BF16), jax.ShapeDtypeStruct((1, D), F32),
                     jax.ShapeDtypeStruct((1, D), F32)], ["row", "row", "row", "vec", "vec"], T, D)


def _bwd_ffn_norm(dx2, ff, w_pff):
    T, D = dx2.shape

    def body(d_ref, ff_ref, w_ref, dff_ref, dw_ref):
        dff, dw = _rms_bwd(ff_ref[...], w_ref[...], d_ref[...])
        dff_ref[...] = dff.astype(BF16)
        _acc_vec(dw_ref, dw)

    return _rowwise("bwd_ffn_norm", body, [dx2, ff, w_pff], ["row", "row", "vec"],
                    [jax.ShapeDtypeStruct((T, D), BF16), jax.ShapeDtypeStruct((1, D), F32)], ["row", "vec"], T, D)


def _bwd_mid(dx2, dh2, x1, w_pf, mix, w_pm):
    T, D = dx2.shape

    def body(dx2_ref, dh2_ref, x1_ref, wpf_ref, mix_ref, wpm_ref, dx1_ref, dmix_ref, dwpf_ref, dwpm_ref):
        d1, dwpf = _rms_bwd(x1_ref[...], wpf_ref[...], dh2_ref[...])
        dx1 = dx2_ref[...] + d1
        dx1_ref[...] = dx1
        dmix, dwpm = _rms_bwd(mix_ref[...], wpm_ref[...], dx1)
        dmix_ref[...] = dmix.astype(BF16)
        _acc_vec(dwpf_ref, dwpf)
        _acc_vec(dwpm_ref, dwpm)

    return _rowwise("bwd_mid", body, [dx2, dh2, x1, w_pf, mix, w_pm], ["row", "row", "row", "vec", "row", "vec"],
                    [jax.ShapeDtypeStruct((T, D), F32), jax.ShapeDtypeStruct((T, D), BF16),
                     jax.ShapeDtypeStruct((1, D), F32), jax.ShapeDtypeStruct((1, D), F32)],
                    ["row", "row", "vec", "vec"], T, D)


def _bwd_in(dx1, dh1, x, w_pre):
    T, D = dx1.shape

    def body(dx1_ref, dh1_ref, x_ref, w_ref, gx_ref, dw_ref):
        d0, dw = _rms_bwd(x_ref[...], w_ref[...], dh1_ref[...])
        gx_ref[...] = dx1_ref[...] + d0
        _acc_vec(dw_ref, dw)

    return _rowwise("bwd_in", body, [dx1, dh1, x, w_pre], ["row", "row", "row", "vec"],
                    [jax.ShapeDtypeStruct((T, D), F32), jax.ShapeDtypeStruct((1, D), F32)], ["row", "vec"], T, D)


def _ffn_up(h2, wg, wu):
    T, D = h2.shape
    S, Fs, _ = wg.shape
    tm, tk = _tile(T, 512), _tile(D, 512)
    nk = D // tk

    def body(h_ref, wg_ref, wu_ref, gt_ref, up_ref, act_ref, accg, accu):
        k = pl.program_id(2)

        @pl.when(k == 0)
        def _():
            accg[...] = jnp.zeros_like(accg)
            accu[...] = jnp.zeros_like(accu)

        h = h_ref[...]
        accg[...] += _dot(h, wg_ref[...], NT)
        accu[...] += _dot(h, wu_ref[...], NT)

        @pl.when(k == nk - 1)
        def _():
            g, u = accg[...], accu[...]
            gt_ref[...] = g.astype(BF16)
            up_ref[...] = u.astype(BF16)
            act_ref[...] = (g * _sigmoid(g) * u).astype(BF16)

    w_spec = pl.BlockSpec((None, Fs, tk), lambda i, j, k: (j, 0, k))
    o_spec = pl.BlockSpec((None, tm, Fs), lambda i, j, k: (j, i, 0))
    return _pcall(
        body, [h2, wg, wu], name="ffn_up", grid=(T // tm, S, nk),
        in_specs=[pl.BlockSpec((tm, tk), lambda i, j, k: (i, k)), w_spec, w_spec],
        out_specs=[o_spec, o_spec, o_spec],
        out_shape=[jax.ShapeDtypeStruct((S, T, Fs), BF16)] * 3,
        scratch_shapes=[pltpu.VMEM((tm, Fs), F32), pltpu.VMEM((tm, Fs), F32)],
        compiler_params=_params(("parallel", "parallel", "arbitrary")),
    )


def _ffn_dact(dff, wd, gt, up):
    T, D = dff.shape
    S, Fs, _ = wd.shape
    tm, tk = _tile(T, 512), _tile(D, 512)
    nk = D // tk

    def body(d_ref, w_ref, gt_ref, up_ref, dgt_ref, dup_ref, acc):
        k = pl.program_id(2)

        @pl.when(k == 0)
        def _():
            acc[...] = jnp.zeros_like(acc)

        acc[...] += _dot(d_ref[...], w_ref[...], NT)

        @pl.when(k == nk - 1)
        def _():
            da, g, u = acc[...], gt_ref[...].astype(F32), up_ref[...].astype(F32)
            sg = _sigmoid(g)
            dgt_ref[...] = (da * u * sg * (1.0 + g * (1.0 - sg))).astype(BF16)
            dup_ref[...] = (da * g * sg).astype(BF16)

    o_spec = pl.BlockSpec((None, tm, Fs), lambda i, j, k: (j, i, 0))
    return _pcall(
        body, [dff, wd, gt, up], name="ffn_dact", grid=(T // tm, S, nk),
        in_specs=[pl.BlockSpec((tm, tk), lambda i, j, k: (i, k)),
                  pl.BlockSpec((None, Fs, tk), lambda i, j, k: (j, 0, k)), o_spec, o_spec],
        out_specs=[o_spec, o_spec],
        out_shape=[jax.ShapeDtypeStruct((S, T, Fs), BF16), jax.ShapeDtypeStruct((S, T, Fs), BF16)],
        scratch_shapes=[pltpu.VMEM((tm, Fs), F32)],
        compiler_params=_params(("parallel", "parallel", "arbitrary")),
    )


def _hgrn2_gates(q, f, lb):
    C = GLA_CHUNK
    sq = _sigmoid(q)
    qs = q * sq
    sf = _sigmoid(f)
    fg = lb + (1.0 - lb) * sf
    lf = jnp.log(jnp.maximum(fg, 1e-30))
    row = lax.broadcasted_iota(jnp.int32, (C, C), 0)
    col = lax.broadcasted_iota(jnp.int32, (C, C), 1)
    tril = row >= col
    b = _dot(jnp.where(tril, 1.0, 0.0), lf, NN, precision=lax.Precision.HIGHEST)
    bmid = b[C // 2 - 1:C // 2, :]
    bend = b[C - 1:C, :]
    ea = jnp.exp(b - bmid)
    ena = jnp.exp(bmid - b)
    eb = ea * jnp.exp(bmid)
    ee = ena * jnp.exp(bend - bmid)
    return sq, qs, sf, fg, tril, ea, ena, eb, ee, jnp.exp(bend)


def _lower_bound(lbp):
    return _sigmoid(lbp[0:1, :] - lbp[1:2, :])


def _hgrn2_fwd(proj6, lb_param, a_norm_w):
    _, T, Wd = proj6.shape
    H, C = Wd // HEAD, GLA_CHUNK
    TB = _tile(T, 512, C)
    NB, NCB = T // TB, TB // C

    HS, W2 = HEADS_PER_STEP, HEADS_PER_STEP * HEAD

    def body(p_ref, lbp_ref, w_ref, cat_ref, o_ref, st_ref, S):
        @pl.when(pl.program_id(1) == 0)
        def _():
            S[...] = jnp.zeros_like(S)

        lanes = [slice(hh * HEAD, (hh + 1) * HEAD) for hh in range(HS)]
        lbs = [_lower_bound(lbp_ref[:, ls]) for ls in lanes]
        ws = [w_ref[:, ls] for ls in lanes]

        def chunk(c, carry):
            rows = pl.ds(pl.multiple_of(c * C, C), C)
            loaded = [([p_ref[s, rows, ls] for s in range(4)], S[hh]) for hh, ls in enumerate(lanes)]
            done = []
            for hh, ((q, f, v, g), st) in enumerate(loaded):
                _, qs, _, fg, tril, ea, ena, eb, ee, ebend = _hgrn2_gates(q, f, lbs[hh])
                k = 1.0 - fg
                p = jnp.where(tril, _bdot(qs * ea, k * ena, NT), 0.0)
                o = _bdot(qs * eb, st, NT) + _bdot(p, v, NN)
                r = lax.rsqrt(jnp.mean(o * o, axis=-1, keepdims=True) + EPS)
                done.append((st, st * ebend + _bdot(v, k * ee, TN), o,
                             (o * r * ws[hh] * (g * _sigmoid(g))).astype(BF16)))
            for hh, ls in enumerate(lanes):
                st, st_new, o, a_out = done[hh]
                st_ref[hh, c] = st
                S[hh] = st_new
                o_ref[rows, ls] = o
                cat_ref[rows, ls] = a_out
            return carry

        lax.fori_loop(0, NCB, chunk, 0)

    return _pcall(
        body, [proj6, lb_param, a_norm_w], name="hgrn2_fwd", grid=(H // HS, NB),
        in_specs=[pl.BlockSpec((4, TB, W2), lambda h, t: (0, t, h)),
                  pl.BlockSpec((2, W2), lambda h, t: (0, h)),
                  pl.BlockSpec((1, W2), lambda h, t: (0, h))],
        out_specs=[pl.BlockSpec((None, TB, W2), lambda h, t: (0, t, h)),
                   pl.BlockSpec((TB, W2), lambda h, t: (t, h)),
                   pl.BlockSpec((HS, NCB, HEAD, HEAD), lambda h, t: (h, t, 0, 0))],
        out_shape=[jax.ShapeDtypeStruct((2, T, Wd), BF16), jax.ShapeDtypeStruct((T, Wd), F32),
                   jax.ShapeDtypeStruct((H, T // C, HEAD, HEAD), F32)],
        scratch_shapes=[pltpu.VMEM((HS, HEAD, HEAD), F32)],
        compiler_params=_params(("parallel", "arbitrary")),
    )


def _hgrn2_bwd(proj6, o, states, dcat, lb_param, a_norm_w):
    _, T, Wd = proj6.shape
    H, C = Wd // HEAD, GLA_CHUNK
    TB = _tile(T, 512, C)
    NB, NCB = T // TB, TB // C
    HS, W2 = HEADS_PER_STEP, HEADS_PER_STEP * HEAD

    def body(p_ref, o_ref, st_ref, da_ref, lbp_ref, w_ref, dp_ref, dw_ref, dlb_ref, dS):
        @pl.when(pl.program_id(1) == 0)
        def _():
            dS[...] = jnp.zeros_like(dS)
            dw_ref[...] = jnp.zeros_like(dw_ref)
            dlb_ref[...] = jnp.zeros_like(dlb_ref)

        lanes = [slice(hh * HEAD, (hh + 1) * HEAD) for hh in range(HS)]
        lbs = [_lower_bound(lbp_ref[:, ls]) for ls in lanes]
        ws = [w_ref[:, ls] for ls in lanes]

        def chunk(i, carry):
            c = NCB - 1 - i
            rows = pl.ds(pl.multiple_of(c * C, C), C)
            loaded = [([p_ref[s, rows, ls] for s in range(4)], st_ref[hh, c], dS[hh], o_ref[rows, ls], da_ref[rows, ls],
                       dw_ref[:, ls], dlb_ref[:, ls]) for hh, ls in enumerate(lanes)]
            done = []
            for hh, ((q, f, v, g), st, dst, ov, da, dw_old, dlb_old) in enumerate(loaded):
                lb, w = lbs[hh], ws[hh]
                sq, qs, sf, fg, tril, ea, ena, eb, ee, ebend = _hgrn2_gates(q, f, lb)
                k = 1.0 - fg
                qt, kt, qb, ke = qs * ea, k * ena, qs * eb, k * ee
                p = jnp.where(tril, _bdot(qt, kt, NT), 0.0)
                r = lax.rsqrt(jnp.mean(ov * ov, axis=-1, keepdims=True) + EPS)
                oh = ov * r
                sg = _sigmoid(g)
                gs = g * sg
                dw_new = dw_old + jnp.sum(da * oh * gs, axis=0, keepdims=True)
                dg = da * oh * w * sg * (1.0 + g * (1.0 - sg))
                doh = da * w * gs
                do = r * (doh - oh * jnp.mean(doh * oh, axis=-1, keepdims=True))
                dqb = _fdot(do, st, NN)
                dp = jnp.where(tril, _bdot(do, v, NT), 0.0)
                dv = _bdot(p, do, TN) + _bdot(ke, dst, NT)
                dqt = _fdot(dp, kt, NN)
                dkt = _fdot(dp, qt, TN)
                dke = _fdot(v, dst, NN)
                dqs = dqt * ea + dqb * eb
                dk = dkt * ena + dke * ee
                db = dqt * qt + dqb * qb - dkt * kt - dke * ke
                db_end = (jnp.sum(dke * ke, axis=0, keepdims=True)
                          + ebend * jnp.sum(dst * st, axis=0, keepdims=True))
                triu = lax.broadcasted_iota(jnp.int32, (C, C), 0) <= lax.broadcasted_iota(jnp.int32, (C, C), 1)
                dlf = _fdot(jnp.where(triu, 1.0, 0.0), db, NN) + db_end
                dfg = jnp.where(fg > 1e-30, dlf / fg, 0.0) - dk
                done.append((dw_new, dlb_old + jnp.sum(dfg * (1.0 - sf), axis=0, keepdims=True),
                             [(dqs * sq * (1.0 + q * (1.0 - sq))).astype(BF16),
                              (dfg * (1.0 - lb) * sf * (1.0 - sf)).astype(BF16), dv.astype(BF16), dg.astype(BF16)],
                             dst * ebend + _bdot(do, qb, TN)))
            for hh, ls in enumerate(lanes):
                dw_new, dlb_new, dsec, dst_new = done[hh]
                dw_ref[:, ls] = dw_new
                dlb_ref[:, ls] = dlb_new
                for s in range(4):
                    dp_ref[s, rows, ls] = dsec[s]
                dS[hh] = dst_new
            return carry

        lax.fori_loop(0, NCB, chunk, 0)

    rev = lambda t: NB - 1 - t
    return _pcall(
        body, [proj6, o, states, dcat, lb_param, a_norm_w], name="hgrn2_bwd", grid=(H // HS, NB),
        in_specs=[pl.BlockSpec((4, TB, W2), lambda h, t: (0, rev(t), h)),
                  pl.BlockSpec((TB, W2), lambda h, t: (rev(t), h)),
                  pl.BlockSpec((HS, NCB, HEAD, HEAD), lambda h, t: (h, rev(t), 0, 0)),
                  pl.BlockSpec((None, TB, W2), lambda h, t: (0, rev(t), h)),
                  pl.BlockSpec((2, W2), lambda h, t: (0, h)),
                  pl.BlockSpec((1, W2), lambda h, t: (0, h))],
        out_specs=[pl.BlockSpec((4, TB, W2), lambda h, t: (0, rev(t), h)),
                   pl.BlockSpec((1, W2), lambda h, t: (0, h)),
                   pl.BlockSpec((1, W2), lambda h, t: (0, h))],
        out_shape=[jax.ShapeDtypeStruct((6, T, Wd), BF16), jax.ShapeDtypeStruct((1, Wd), F32),
                   jax.ShapeDtypeStruct((1, Wd), F32)],
        scratch_shapes=[pltpu.VMEM((HS, HEAD, HEAD), F32)],
        compiler_params=_params(("parallel", "arbitrary")),
    )


_INV_SQRT2 = 1.0 / math.sqrt(2.0)
_INV_SQRT2PI = 1.0 / math.sqrt(2.0 * math.pi)


def _gelu(v):
    return 0.5 * v * (1.0 + lax.erf(v * _INV_SQRT2))


def _gelu_grad(v):
    return 0.5 * (1.0 + lax.erf(v * _INV_SQRT2)) + v * jnp.exp(-0.5 * v * v) * _INV_SQRT2PI


def _gmlp_norm(v, ln_w, ln_b):
    vf = _gelu(v)
    mu = jnp.mean(vf, axis=-1, keepdims=True)
    cen = vf - mu
    rstd = lax.rsqrt(jnp.mean(cen * cen, axis=-1, keepdims=True) + EPS)
    vh = cen * rstd
    return vh, rstd, vh * ln_w + ln_b


def _tril_mask():
    n = GMLP_CHUNK
    return lax.broadcasted_iota(jnp.int32, (n, n), 0) >= lax.broadcasted_iota(jnp.int32, (n, n), 1)


def _gmlp_fwd(proj6, cat, ln_w, ln_b, w_s, b_st):
    _, T, Wd = proj6.shape
    G, N = Wd // HEAD, GMLP_CHUNK

    def body(p_ref, cat_in, lnw_ref, lnb_ref, ws_ref, bst_ref, out_ref):
        del cat_in
        _, _, vn = _gmlp_norm(p_ref[1], lnw_ref[...], lnb_ref[...])
        tril = _tril_mask()
        for g in range(G):
            cols = slice(g * HEAD, (g + 1) * HEAD)
            z = _bdot(jnp.where(tril, ws_ref[g], 0.0), vn[:, cols], NN) + bst_ref[:, g:g + 1]
            out_ref[:, cols] = (_gelu(p_ref[0, :, cols]) * z).astype(BF16)

    return _pcall(
        body, [proj6, cat, ln_w, ln_b, w_s, b_st], name="gmlp_fwd", grid=(T // N,),
        in_specs=[pl.BlockSpec((2, N, Wd), lambda t: (2, t, 0)), pl.BlockSpec(memory_space=pl.ANY),
                  pl.BlockSpec((1, Wd), lambda t: (0, 0)), pl.BlockSpec((1, Wd), lambda t: (0, 0)),
                  pl.BlockSpec((G, N, N), lambda t: (0, 0, 0)), pl.BlockSpec((N, LANES), lambda t: (0, 0))],
        out_specs=pl.BlockSpec((None, N, Wd), lambda t: (1, t, 0)),
        out_shape=jax.ShapeDtypeStruct(cat.shape, cat.dtype),
        input_output_aliases={1: 0},
        compiler_params=_params(("arbitrary",)),
    )


def _gmlp_bwd(proj6, dproj6, dcat, ln_w, ln_b, w_s, b_st):
    _, T, Wd = proj6.shape
    G, N = Wd // HEAD, GMLP_CHUNK

    def body(p_ref, dp_in, db_ref, lnw_ref, lnb_ref, ws_ref, bst_ref, dp_ref, dlnw_ref, dlnb_ref, dws_ref, dbst_ref, dvn):
        del dp_in

        @pl.when(pl.program_id(0) == 0)
        def _():
            dlnw_ref[...] = jnp.zeros_like(dlnw_ref)
            dlnb_ref[...] = jnp.zeros_like(dlnb_ref)
            dws_ref[...] = jnp.zeros_like(dws_ref)
            dbst_ref[...] = jnp.zeros_like(dbst_ref)

        lnw = lnw_ref[...]
        v = p_ref[1]
        vh, rstd, vn = _gmlp_norm(v, lnw, lnb_ref[...])
        tril = _tril_mask()
        lane = lax.broadcasted_iota(jnp.int32, (N, LANES), 1)
        dbst = jnp.zeros((N, LANES), F32)
        for g in range(G):
            cols = slice(g * HEAD, (g + 1) * HEAD)
            wt = jnp.where(tril, ws_ref[g], 0.0)
            vn_g = vn[:, cols]
            z = _bdot(wt, vn_g, NN) + bst_ref[:, g:g + 1]
            u = p_ref[0, :, cols]
            db = db_ref[:, cols]
            dp_ref[0, :, cols] = (db * z * _gelu_grad(u)).astype(BF16)
            dz = db * _gelu(u)
            dbst = dbst + jnp.where(lane == g, jnp.sum(dz, axis=1, keepdims=True), 0.0)
            dws_ref[g] += jnp.where(tril, _bdot(dz, vn_g, NT), 0.0)
            dvn[:, cols] = _bdot(wt, dz, TN)
        dbst_ref[...] += dbst
        dv = dvn[...]
        dlnw_ref[...] += jnp.sum(dv * vh, axis=0, keepdims=True)
        dlnb_ref[...] += jnp.sum(dv, axis=0, keepdims=True)
        dvh = dv * lnw
        dvf = rstd * (dvh - jnp.mean(dvh, axis=-1, keepdims=True) - vh * jnp.mean(dvh * vh, axis=-1, keepdims=True))
        dp_ref[1] = (dvf * _gelu_grad(v)).astype(BF16)

    vec = pl.BlockSpec((1, Wd), lambda t: (0, 0))
    return _pcall(
        body, [proj6, dproj6, dcat, ln_w, ln_b, w_s, b_st], name="gmlp_bwd", grid=(T // N,),
        in_specs=[pl.BlockSpec((2, N, Wd), lambda t: (2, t, 0)), pl.BlockSpec(memory_space=pl.ANY),
                  pl.BlockSpec((None, N, Wd), lambda t: (1, t, 0)), vec, vec,
                  pl.BlockSpec((G, N, N), lambda t: (0, 0, 0)), pl.BlockSpec((N, LANES), lambda t: (0, 0))],
        out_specs=[pl.BlockSpec((2, N, Wd), lambda t: (2, t, 0)), vec, vec,
                   pl.BlockSpec((G, N, N), lambda t: (0, 0, 0)), pl.BlockSpec((N, LANES), lambda t: (0, 0))],
        out_shape=[jax.ShapeDtypeStruct(dproj6.shape, dproj6.dtype), jax.ShapeDtypeStruct((1, Wd), F32),
                   jax.ShapeDtypeStruct((1, Wd), F32), jax.ShapeDtypeStruct((G, N, N), F32),
                   jax.ShapeDtypeStruct((N, LANES), F32)],
        scratch_shapes=[pltpu.VMEM((N, Wd), F32)],
        input_output_aliases={1: 0},
        compiler_params=_params(("arbitrary",)),
    )


def _place():
    return lax.axis_index("x"), lax.axis_index("y"), lax.axis_index("c")


def _other_chips(x, y):
    return [(1 - x, y), (x, 1 - y), (1 - x, 1 - y)]


def _rcopy(src, dst, ssem, rsem, k, to):
    return pltpu.make_async_remote_copy(src_ref=src, dst_ref=dst, send_sem=ssem.at[k], recv_sem=rsem.at[k],
                                        device_id=to, device_id_type=MESH)


def _cast_slot(name, w, place):
    _, R, C = w.shape
    tr = _row_block(R, C)

    def body(p_ref, w_ref, o_ref):
        del p_ref
        o_ref[...] = w_ref[...].astype(BF16)

    return _pcall(
        body, [place[0], w], name=name, prefetch=1, grid=(R // tr,),
        in_specs=[pl.BlockSpec((None, tr, C), lambda i, p: (0, i, 0))],
        out_specs=pl.BlockSpec((None, tr, C), lambda i, p: (p[0], i, 0)),
        out_shape=jax.ShapeDtypeStruct((4, R, C), BF16),
        compiler_params=_params(("parallel",)))


def _half(ref, px, py, pc, Rh):
    return ref.at[2 * px + py, pl.ds(pc * Rh, Rh), :]


def _split_start(name, bufs, n_sem, copies):
    nb = len(bufs)

    def body(*refs):
        for cp in copies(refs[:nb], refs[nb], refs[nb + 1], False):
            cp.start()
        refs[-1][...] = jnp.zeros_like(refs[-1])

    outs = _pcall(
        body, [pltpu.with_memory_space_constraint(b, pltpu.HBM) for b in bufs], same=bufs, token=-1, name=name,
        in_specs=[HBM_SPEC] * nb, out_specs=(SEM_SPEC, SEM_SPEC, *[HBM_SPEC] * nb, VMEM_SPEC),
        out_shape=(pltpu.SemaphoreType.DMA((n_sem,)), pltpu.SemaphoreType.DMA((n_sem,)),
                   *[pltpu.HBM(b.shape, b.dtype) for b in bufs], jax.ShapeDtypeStruct((8, LANES), F32)),
        input_output_aliases={i: 2 + i for i in range(nb)},
        compiler_params=pltpu.CompilerParams(has_side_effects=SIDE_EFFECT))
    return outs[:-1]


def _split_wait(name, started, copies):
    ssem, rsem, *bufs = started
    nb = len(bufs)

    def body(*refs):
        for cp in copies(refs[:nb], refs[nb], refs[nb + 1], True):
            cp.wait_send()
            cp.wait_recv()

    outs = _pcall(
        body, [*bufs, ssem, rsem], name=name, in_specs=[HBM_SPEC] * nb + [SEM_SPEC, SEM_SPEC],
        out_specs=tuple([HBM_SPEC] * nb), out_shape=tuple(pltpu.HBM(b.shape, b.dtype) for b in bufs),
        input_output_aliases={i: i for i in range(nb)},
        compiler_params=pltpu.CompilerParams(has_side_effects=SIDE_EFFECT))
    return outs


def _gather_copies(Rh):
    def copies(refs, ssem, rsem, waiting):
        x, y, c = _place()
        mine = _half(refs[0], x, y, c, Rh)
        return [_rcopy(mine, _half(refs[0], px, py, c, Rh) if waiting else mine, ssem, rsem, j, (px, py, c))
                for j, (px, py) in enumerate(_other_chips(x, y))]
    return copies


def _swap_copies(Rh):
    def copies(refs, ssem, rsem, waiting):
        x, y, c = _place()
        return [_rcopy(refs[0].at[:, pl.ds((1 - c) * Rh, Rh), :], refs[1], ssem, rsem, 0, (x, y, 1 - c))]
    return copies


def _exchange_copies(refs, ssem, rsem, waiting):
    x, y, c = _place()
    return [_rcopy(refs[0].at[2 * px + py], refs[1].at[j], ssem, rsem, j, (px, py, c))
            for j, (px, py) in enumerate(_other_chips(x, y))]


def _landing(shape, dtype):
    return lax.empty(shape, dtype)


def _forward_copies(Rh):
    def copies(refs, ssem, rsem, waiting):
        x, y, c = _place()
        return [_rcopy(_half(refs[0], px, py, c, Rh), _half(refs[0], px, py, 1 - c if waiting else c, Rh),
                       ssem, rsem, j, (x, y, 1 - c)) for j, (px, py) in enumerate(_other_chips(x, y))]
    return copies


def _share_copies(Rh):
    def copies(refs, ssem, rsem, waiting):
        x, y, c = _place()
        mine = refs[0].at[pl.ds(c * Rh, Rh), :]
        return [_rcopy(mine, refs[0].at[pl.ds((1 - c) * Rh, Rh), :] if waiting else mine, ssem, rsem, 0, (x, y, 1 - c))]
    return copies


def _sum_all_devices(v):
    R, L = v.shape

    def body(v_ref, out_ref, buf, ssem, rsem):
        x, y, c = _place()
        me = 4 * x + 2 * y + c
        buf[me] = v_ref[...]
        copies = []
        for m in range(1, 8):
            to = (x ^ (m >> 2), y ^ ((m >> 1) & 1), c ^ (m & 1))
            copies.append(_rcopy(v_ref, buf.at[me], ssem, rsem, m - 1, to))
        for cp in copies:
            cp.start()
        for cp in copies:
            cp.wait()
        acc = buf[0]
        for d in range(1, 8):
            acc = acc + buf[d]
        out_ref[...] = acc

    return _pcall(
        body, [v], name="sum_all_devices", in_specs=[VMEM_SPEC], out_specs=VMEM_SPEC,
        out_shape=jax.ShapeDtypeStruct((R, L), F32),
        scratch_shapes=[pltpu.VMEM((8, R, L), F32), pltpu.SemaphoreType.DMA((7,)), pltpu.SemaphoreType.DMA((7,))],
        compiler_params=pltpu.CompilerParams(vmem_limit_bytes=VMEM_LIMIT))


def _elementwise(name, body, ins, in_specs, outs, out_specs, grid):
    return _pcall(body, ins, name=name, grid=grid, in_specs=in_specs, out_specs=out_specs, out_shape=outs,
                  compiler_params=_params(("parallel",) * len(grid)))


def _row_block(R, C, bytes_per_row_elem=4, target=1 << 20):
    return _tile(R, max(16, target // (C * bytes_per_row_elem)), 16)


def _add_halves(name, g4, land, c_idx):
    S, R, C = g4.shape
    Rh = R // 2
    tr = _row_block(Rh, C)
    nb = Rh // tr

    def body(c_ref, g_ref, l_ref, o_ref):
        del c_ref
        o_ref[...] = (g_ref[...].astype(F32) + l_ref[...].astype(F32)).astype(BF16)

    return _pcall(
        body, [c_idx, g4, land], name=name, prefetch=1, grid=(S, nb),
        in_specs=[pl.BlockSpec((None, tr, C), lambda s, i, c: (s, c[0] * nb + i, 0)),
                  pl.BlockSpec((None, tr, C), lambda s, i, c: (s, i, 0))],
        out_specs=pl.BlockSpec((None, tr, C), lambda s, i, c: (s, i, 0)),
        out_shape=jax.ShapeDtypeStruct((S, Rh, C), BF16),
        compiler_params=_params(("parallel", "parallel")))


def _sum_chips(name, land, pb, place):
    S, Rh, C = land.shape
    tr = _row_block(Rh, C)
    nb = Rh // tr

    def body(p_ref, c_ref, l_ref, own_ref, o_ref):
        del p_ref, c_ref
        acc = own_ref[...].astype(F32)
        for s in range(S):
            acc = acc + l_ref[s].astype(F32)
        o_ref[...] = acc

    return _pcall(
        body, [place[0], place[1], land, pb], name=name, prefetch=2, grid=(nb,),
        in_specs=[pl.BlockSpec((S, tr, C), lambda i, p, c: (0, i, 0)),
                  pl.BlockSpec((None, tr, C), lambda i, p, c: (p[0], i, 0))],
        out_specs=pl.BlockSpec((tr, C), lambda i, p, c: (c[0] * nb + i, 0)),
        out_shape=jax.ShapeDtypeStruct((2 * Rh, C), F32),
        compiler_params=_params(("parallel",)))


def _adamw(name, w, g, m, v):
    R, C = g.shape
    tr = _row_block(R, C, target=1 << 19)
    c1 = 1.0 - ADAM_B1 ** ADAM_STEP
    c2 = 1.0 - ADAM_B2 ** ADAM_STEP

    def body(w_ref, g_ref, m_ref, v_ref, go_ref, d_ref, nm_ref, nv_ref):
        gv = g_ref[...]
        go_ref[...] = gv
        nm = ADAM_B1 * m_ref[...] + (1.0 - ADAM_B1) * gv
        nv = ADAM_B2 * v_ref[...] + (1.0 - ADAM_B2) * (gv * gv)
        nm_ref[...] = nm
        nv_ref[...] = nv
        d_ref[...] = -ADAM_LR * ((nm / c1) / (jnp.sqrt(nv / c2) + ADAM_EPS) + ADAM_WD * w_ref[...])

    spec = pl.BlockSpec((tr, C), lambda i: (i, 0))
    wspec = pl.BlockSpec((None, tr, C), lambda i: (0, i, 0)) if w.ndim == 3 else spec
    shp = jax.ShapeDtypeStruct((R, C), F32)
    return _elementwise(name, body, [w, g, m, v], [wspec, spec, wspec, wspec], [shp] * 4, [spec] * 4, (R // tr,))


def _lb_grad(dlb, lb_param):
    def body(d_ref, p_ref, o_ref):
        lb = _lower_bound(p_ref[...])
        t = d_ref[...] * lb * (1.0 - lb)
        o_ref[0:1, :] = t
        o_ref[1:2, :] = -t

    return _pcall(body, [dlb, lb_param], name="lb_grad", in_specs=[VMEM_SPEC, VMEM_SPEC], out_specs=VMEM_SPEC,
                  out_shape=jax.ShapeDtypeStruct(lb_param.shape, F32))


class _Reduction:
    def __init__(self, tag, g4, place):
        self.tag, self.place = tag, place
        _, R, C = g4.shape
        self.Rh = R // 2
        self.state = _split_start("swap_start_" + tag, [g4, _landing((4, self.Rh, C), g4.dtype)], 1, _swap_copies(self.Rh))

    def exchange(self):
        g4, land = _split_wait("swap_wait_" + self.tag, self.state, _swap_copies(self.Rh))
        pb = _add_halves("addh_" + self.tag, g4, land, self.place[1])
        self.state = _split_start("xchg_start_" + self.tag, [pb, _landing((3,) + pb.shape[1:], BF16)], 3, _exchange_copies)

    def finish(self):
        pb, got = _split_wait("xchg_wait_" + self.tag, self.state, _exchange_copies)
        full = _sum_chips("sumc_" + self.tag, got, pb, self.place)
        self.state = _split_start("share_start_" + self.tag, [full], 1, _share_copies(self.Rh))

    def update(self, w, m, v):
        (full,) = _split_wait("share_wait_" + self.tag, self.state, _share_copies(self.Rh))
        return tuple(a[None] for a in _adamw("adamw_" + self.tag, w, full, m, v))


def kernel(x, p, pre_mix_w, w_in, lb_param, a_norm_w, gmlp_ln_w, gmlp_ln_b, w_spatial, b_spatial, w_out, post_mix_w, pre_ffn_w, w_gate, w_up, w_down, post_ffn_w, w_ple, w_ple_gate, post_ple_w, loss_target, m_pre_mix_w, m_w_in, m_lb_param, m_a_norm_w, m_gmlp_ln_w, m_gmlp_ln_b, m_w_spatial, m_b_spatial, m_w_out, m_post_mix_w, m_pre_ffn_w, m_w_gate, m_w_up, m_w_down, m_post_ffn_w, m_w_ple, m_w_ple_gate, m_post_ple_w, v_pre_mix_w, v_w_in, v_lb_param, v_a_norm_w, v_gmlp_ln_w, v_gmlp_ln_b, v_w_spatial, v_b_spatial, v_w_out, v_post_mix_w, v_pre_ffn_w, v_w_gate, v_w_up, v_w_down, v_post_ffn_w, v_w_ple, v_w_ple_gate, v_post_ple_w):
    T, D = x.shape[1], x.shape[2]
    Wd = D // 2
    G = Wd // HEAD
    NCH = 4
    Ci = w_in.shape[2]
    Fs = w_gate.shape[2]
    Dq = w_out.shape[1]
    Pd = w_ple.shape[1]
    xs, ps, tgt = x[0], p[0, 0], loss_target[0]
    c_idx = lax.axis_index("c").astype(jnp.int32).reshape((1,))
    place = ((2 * lax.axis_index("x") + lax.axis_index("y")).astype(jnp.int32).reshape((1,)), c_idx)

    swap = lambda a: jnp.swapaxes(a, 1, 2)
    transposed = ("w_gate", "w_up")
    weights = dict(w_in=w_in, w_out=w_out, w_gate=swap(w_gate), w_up=swap(w_up), w_down=w_down,
                   w_ple_gate=w_ple_gate, w_ple=w_ple)
    _Order.last = None
    gathering = {}
    for tag, w in weights.items():
        buf = _cast_slot("cast_" + tag, w, place)
        gathering[tag] = _split_start("gather_start_" + tag, [buf], 3, _gather_copies(buf.shape[1] // 2))

    forwarding = {}

    def forward_start(tag):
        Rh = weights[tag].shape[1] // 2
        (buf,) = _split_wait("gather_wait_" + tag, gathering.pop(tag), _gather_copies(Rh))
        forwarding[tag] = _split_start("forward_start_" + tag, [buf], 3, _forward_copies(Rh))

    def gathered(tag):
        (buf,) = _split_wait("forward_wait_" + tag, forwarding.pop(tag), _forward_copies(weights[tag].shape[1] // 2))
        return buf

    res = {}
    moments = dict(w_in=(m_w_in, v_w_in), w_out=(m_w_out, v_w_out), w_gate=(swap(m_w_gate), swap(v_w_gate)),
                   w_up=(swap(m_w_up), swap(v_w_up)), w_down=(m_w_down, v_w_down), w_ple=(m_w_ple, v_w_ple),
                   w_ple_gate=(m_w_ple_gate, v_w_ple_gate))
    reducing = {}

    def reduce_start(tag, g4):
        reducing[tag] = _Reduction(tag, g4, place)

    def reduce_exchange(tag):
        reducing[tag].exchange()

    def reduce_finish(tag):
        reducing[tag].finish()

    def reduce_update(tag):
        outs = reducing.pop(tag).update(weights[tag], *moments[tag])
        res[tag] = tuple(swap(a) for a in outs) if tag in transposed else outs

    tm, tn, tk = _tile(T, 1024), _tile(D, 1024), _tile(Wd, 1024)
    ts = _tile(math.gcd(Wd, Ci), 1024)
    tkt = _tile(T, 1024)
    n_sec, n_shd = Wd // ts, Ci // ts
    b_st = jnp.pad(b_spatial[0].T, ((0, 0), (0, LANES - G)))

    forward_start("w_in")
    h1 = _norm_in(xs, pre_mix_w)
    gw_in = gathered("w_in")
    proj6 = _mm("proj", [h1, gw_in],
                [pl.BlockSpec((tm, tk), lambda i, j, k: (i, k)),
                 pl.BlockSpec((None, tk, ts), lambda i, j, k: (j // n_shd, k, j % n_shd))],
                jax.ShapeDtypeStruct((6, T, Wd), F32),
                pl.BlockSpec((None, tm, ts), lambda i, j, k: (j // n_sec, i, j % n_sec)),
                (tm, ts), (T // tm, 6 * n_sec, D // tk), NN)
    forward_start("w_out")
    cat, o_a, states = _hgrn2_fwd(proj6, lb_param, a_norm_w)
    forward_start("w_gate")
    cat = _gmlp_fwd(proj6, cat, gmlp_ln_w, gmlp_ln_b, w_spatial[0], b_st)
    nkc = Wd // tk
    gw_out = gathered("w_out").reshape(D, D)
    mix = _mm("mix", [cat, gw_out],
              [pl.BlockSpec((None, tm, tk), lambda i, j, k: (k // nkc, i, k % nkc)),
               pl.BlockSpec((tk, tn), lambda i, j, k: (k, j))],
              jax.ShapeDtypeStruct((T, D), F32), pl.BlockSpec((tm, tn), lambda i, j, k: (i, j)),
              (tm, tn), (T // tm, D // tn, D // tk), NN)
    forward_start("w_up")
    x1, h2 = _mid1(xs, mix, post_mix_w, pre_ffn_w)
    gw_gate = gathered("w_gate")
    gw_up = gathered("w_up")
    gt, up, act = _ffn_up(h2, gw_gate, gw_up)
    tnf = _tile(D, 512)
    forward_start("w_down")
    forward_start("w_ple_gate")
    forward_start("w_ple")
    gw_down = gathered("w_down")
    ff = _mm("ffn_down", [act, gw_down],
             [pl.BlockSpec((None, tm, Fs), lambda i, j, k: (k, i, 0)),
              pl.BlockSpec((None, Fs, tnf), lambda i, j, k: (k, 0, j))],
             jax.ShapeDtypeStruct((T, D), F32), pl.BlockSpec((tm, tnf), lambda i, j, k: (i, j)),
             (tm, tnf), (T // tm, D // tnf, NCH), NN)
    x2, x2b = _mid2(x1, ff, post_ffn_w)
    gw_pg = gathered("w_ple_gate").reshape(D, D)
    gl = _mm("ple_gate", [x2b, gw_pg],
             [pl.BlockSpec((tm, tk), lambda i, j, k: (i, k)), pl.BlockSpec((tk, tn), lambda i, j, k: (k, j))],
             jax.ShapeDtypeStruct((T, D), F32), pl.BlockSpec((tm, tn), lambda i, j, k: (i, j)),
             (tm, tn), (T // tm, D // tn, D // tk), NN)
    tq = _tile(Dq, 1024)
    nq = Dq // tq
    gw_ple = gathered("w_ple")
    e = _mm("ple_embed", [ps, gw_ple],
            [pl.BlockSpec((tm, Pd), lambda i, j, k: (i, 0)),
             pl.BlockSpec((None, Pd, tq), lambda i, j, k: (j // nq, 0, j % nq))],
            jax.ShapeDtypeStruct((T, D), F32), pl.BlockSpec((tm, tq), lambda i, j, k: (i, j)),
            (tm, tq), (T // tm, D // tq, 1), NN)

    d3, de, dgl, dw_pp, loss_vec = _head(x2, gl, e, post_ple_w, tgt)
    g_ple = _mm("d_w_ple", [ps, de],
                [pl.BlockSpec((tkt, Pd), lambda i, j, k: (k, 0)), pl.BlockSpec((tkt, tq), lambda i, j, k: (k, j))],
                jax.ShapeDtypeStruct((NCH, Pd, Dq), PARTIAL),
                pl.BlockSpec((None, Pd, tq), lambda i, j, k: (j // nq, 0, j % nq)),
                (Pd, tq), (1, D // tq, T // tkt), TN)
    reduce_start("w_ple", g_ple)
    tmw = _tile(D, 1024)
    g_pg = _mm("d_w_ple_gate", [x2b, dgl],
               [pl.BlockSpec((tkt, tq), lambda i, j, k: (k, i)), pl.BlockSpec((tkt, tn), lambda i, j, k: (k, j))],
               jax.ShapeDtypeStruct((NCH, Dq, D), PARTIAL), pl.BlockSpec((None, tq, tn), lambda i, j, k: (i // nq, i % nq, j)),
               (tq, tn), (D // tq, D // tn, T // tkt), TN)
    reduce_start("w_ple_gate", g_pg)
    reduce_exchange("w_ple")
    dx2 = _mm("d_x2", [dgl, gw_pg],
              [pl.BlockSpec((tm, tk), lambda i, j, k: (i, k)), pl.BlockSpec((tn, tk), lambda i, j, k: (j, k))],
              jax.ShapeDtypeStruct((T, D), F32), pl.BlockSpec((tm, tn), lambda i, j, k: (i, j)),
              (tm, tn), (T // tm, D // tn, D // tk), NT,
              add=d3, add_spec=pl.BlockSpec((tm, tn), lambda i, j, k: (i, j)))
    reduce_exchange("w_ple_gate")
    dff, dw_pff = _bwd_ffn_norm(dx2, ff, post_ffn_w)
    g_down = _mm("d_w_down", [act, dff],
                 [pl.BlockSpec((None, tkt, Fs), lambda i, j, k: (i, k, 0)),
                  pl.BlockSpec((tkt, tnf), lambda i, j, k: (k, j))],
                 jax.ShapeDtypeStruct((NCH, Fs, D), PARTIAL), pl.BlockSpec((None, Fs, tnf), lambda i, j, k: (i, 0, j)),
                 (Fs, tnf), (NCH, D // tnf, T // tkt), TN)
    reduce_start("w_down", g_down)
    reduce_finish("w_ple")
    dgt, dup = _ffn_dact(dff, gw_down, gt, up)
    reduce_exchange("w_down")
    reduce_finish("w_ple_gate")
    reduce_update("w_ple")
    wgrad_specs = [pl.BlockSpec((None, tkt, Fs), lambda i, j, k: (i, k, 0)),
                   pl.BlockSpec((tkt, tnf), lambda i, j, k: (k, j))]
    wgrad_out = pl.BlockSpec((None, Fs, tnf), lambda i, j, k: (i, 0, j))
    g_gate = _mm("d_w_gate", [dgt, h2], wgrad_specs, jax.ShapeDtypeStruct((NCH, Fs, D), PARTIAL), wgrad_out,
                 (Fs, tnf), (NCH, D // tnf, T // tkt), TN)
    reduce_start("w_gate", g_gate)
    reduce_update("w_ple_gate")
    g_up = _mm("d_w_up", [dup, h2], wgrad_specs, jax.ShapeDtypeStruct((NCH, Fs, D), PARTIAL), wgrad_out,
               (Fs, tnf), (NCH, D // tnf, T // tkt), TN)
    reduce_start("w_up", g_up)
    reduce_exchange("w_gate")
    tmh = _tile(T, 1024)
    a_sp = pl.BlockSpec((None, tmh, Fs), lambda i, j, k: (k, i, 0))
    b_sp = pl.BlockSpec((None, Fs, tnf), lambda i, j, k: (k, 0, j))
    dh2 = _mm("d_h2", [dgt, gw_gate, dup, gw_up], [a_sp, b_sp, a_sp, b_sp],
              jax.ShapeDtypeStruct((T, D), F32), pl.BlockSpec((tmh, tnf), lambda i, j, k: (i, j)),
              (tmh, tnf), (T // tmh, D // tnf, NCH), NN)
    reduce_exchange("w_up")
    reduce_finish("w_down")
    dx1, dmix, dw_pf, dw_pm = _bwd_mid(dx2, dh2, x1, pre_ffn_w, mix, post_mix_w)
    reduce_update("w_down")
    tw = _tile(Wd, 1024)
    nw = Wd // tw
    nwq = Wd // tq
    g_out = _mm("d_w_out", [cat, dmix],
                [pl.BlockSpec((None, tkt, tq), lambda i, j, k: (i // nwq, k, i % nwq)),
                 pl.BlockSpec((tkt, tn), lambda i, j, k: (k, j))],
                jax.ShapeDtypeStruct((NCH, Dq, D), PARTIAL), pl.BlockSpec((None, tq, tn), lambda i, j, k: (i // nq, i % nq, j)),
                (tq, tn), (D // tq, D // tn, T // tkt), TN)
    reduce_start("w_out", g_out)
    reduce_finish("w_gate")
    dcat = _mm("d_cat", [dmix, gw_out],
               [pl.BlockSpec((tm, tk), lambda i, j, k: (i, k)), pl.BlockSpec((tw, tk), lambda i, j, k: (j, k))],
               jax.ShapeDtypeStruct((2, T, Wd), F32), pl.BlockSpec((None, tm, tw), lambda i, j, k: (j // nw, i, j % nw)),
               (tm, tw), (T // tm, D // tw, D // tk), NT)
    reduce_exchange("w_out")
    reduce_update("w_gate")
    dproj6, dw_an, dlb = _hgrn2_bwd(proj6, o_a, states, dcat, lb_param, a_norm_w)
    reduce_finish("w_up")
    dproj6, dw_lnw, dw_lnb, dw_sp, dw_bst = _gmlp_bwd(proj6, dproj6, dcat, gmlp_ln_w, gmlp_ln_b, w_spatial[0], b_st)
    reduce_update("w_up")
    g_in = _mm("d_w_in", [h1, dproj6],
               [pl.BlockSpec((tkt, tmw), lambda i, j, k: (k, i)),
                pl.BlockSpec((None, tkt, ts), lambda i, j, k: (j // n_sec, k, j % n_sec))],
               jax.ShapeDtypeStruct((NCH, D, Ci), PARTIAL),
               pl.BlockSpec((None, tmw, ts), lambda i, j, k: (j // n_shd, i, j % n_shd)),
               (tmw, ts), (D // tmw, 6 * n_sec, T // tkt), TN)
    reduce_start("w_in", g_in)
    reduce_finish("w_out")
    reduce_exchange("w_in")
    dh1 = _mm("d_h1", [dproj6, gw_in],
              [pl.BlockSpec((None, tm, ts), lambda i, j, k: (k // n_sec, i, k % n_sec)),
               pl.BlockSpec((None, tn, ts), lambda i, j, k: (k // n_shd, j, k % n_shd))],
              jax.ShapeDtypeStruct((T, D), F32), pl.BlockSpec((tm, tn), lambda i, j, k: (i, j)),
              (tm, tn), (T // tm, D // tn, 6 * n_sec), NT)
    reduce_update("w_out")
    grad_x, dw_pre = _bwd_in(dx1, dh1, xs, pre_mix_w)

    small = [("pre_mix_w", dw_pre), ("lb", dlb), ("a_norm_w", dw_an), ("gmlp_ln_w", dw_lnw), ("gmlp_ln_b", dw_lnb),
             ("w_spatial", dw_sp), ("b_spatial_t", dw_bst), ("post_mix_w", dw_pm), ("pre_ffn_w", dw_pf),
             ("post_ffn_w", dw_pff), ("post_ple_w", dw_pp), ("loss", loss_vec)]
    rows = [a.size // LANES for _, a in small]
    total = sum(rows)
    pad = (-total) % 8
    packed = jnp.concatenate([a.reshape(-1, LANES) for _, a in small] + [jnp.zeros((pad, LANES), F32)], axis=0)
    summed = _sum_all_devices(packed)
    off, piece = 0, {}
    for (nm, a), r in zip(small, rows):
        piece[nm] = summed[off:off + r].reshape(a.shape)
        off += r
    loss = jnp.sum(piece["loss"])
    g_small = {
        "pre_mix_w": piece["pre_mix_w"], "lb_param": _lb_grad(piece["lb"], lb_param), "a_norm_w": piece["a_norm_w"],
        "gmlp_ln_w": piece["gmlp_ln_w"], "gmlp_ln_b": piece["gmlp_ln_b"], "w_spatial": piece["w_spatial"][None],
        "b_spatial": piece["b_spatial_t"][:, :G].T[None], "post_mix_w": piece["post_mix_w"],
        "pre_ffn_w": piece["pre_ffn_w"], "post_ffn_w": piece["post_ffn_w"], "post_ple_w": piece["post_ple_w"],
    }
    w_small = dict(pre_mix_w=(pre_mix_w, m_pre_mix_w, v_pre_mix_w), lb_param=(lb_param, m_lb_param, v_lb_param),
                   a_norm_w=(a_norm_w, m_a_norm_w, v_a_norm_w), gmlp_ln_w=(gmlp_ln_w, m_gmlp_ln_w, v_gmlp_ln_w),
                   gmlp_ln_b=(gmlp_ln_b, m_gmlp_ln_b, v_gmlp_ln_b), w_spatial=(w_spatial, m_w_spatial, v_w_spatial),
                   b_spatial=(b_spatial, m_b_spatial, v_b_spatial), post_mix_w=(post_mix_w, m_post_mix_w, v_post_mix_w),
                   pre_ffn_w=(pre_ffn_w, m_pre_ffn_w, v_pre_ffn_w), post_ffn_w=(post_ffn_w, m_post_ffn_w, v_post_ffn_w),
                   post_ple_w=(post_ple_w, m_post_ple_w, v_post_ple_w))
    names_small = list(w_small)

    def pack(arrs):
        flat = jnp.concatenate([a.reshape(-1, LANES) for a in arrs], axis=0)
        return jnp.pad(flat, ((0, (-flat.shape[0]) % 16), (0, 0)))

    pk = [pack([w_small[n][i] for n in names_small]) for i in range(3)]
    pg = pack([g_small[n] for n in names_small])
    _, sd, sm, sv = _adamw("adamw_small", pk[0], pg, pk[1], pk[2])
    off = 0
    for n in names_small:
        shp = w_small[n][0].shape
        r = w_small[n][0].size // LANES
        res[n] = (g_small[n].reshape(shp), sd[off:off + r].reshape(shp), sm[off:off + r].reshape(shp),
                  sv[off:off + r].reshape(shp))
        off += r

    reduce_finish("w_in")
    reduce_update("w_in")

    order =["pre_mix_w", "w_in", "lb_param", "a_norm_w", "gmlp_ln_w", "gmlp_ln_b", "w_spatial", "b_spatial", "w_out",
             "post_mix_w", "pre_ffn_w", "w_gate", "w_up", "w_down", "post_ffn_w", "w_ple", "w_ple_gate", "post_ple_w"]
    return (loss, grad_x[None], *[res[n][0] for n in order], *[res[n][1] for n in order],
            *[res[n][2] for n in order], *[res[n][3] for n in order])
```

```python
import functools
import math

import jax
import jax.numpy as jnp
from jax import lax
from jax.experimental import pallas as pl
from jax.experimental.pallas import tpu as pltpu

F32 = jnp.float32
BF16 = jnp.bfloat16
PARTIAL = jnp.bfloat16
EPS = 1e-6
HEAD = 128
GLA_CHUNK = 64
HEADS_PER_STEP = 2
GMLP_CHUNK = 128
LANES = 128
VMEM_LIMIT = 48 * 1024 * 1024
ADAM_LR, ADAM_B1, ADAM_B2, ADAM_EPS, ADAM_WD, ADAM_STEP = 0.001, 0.9, 0.999, 1e-08, 0.01, 10
MESH = pl.DeviceIdType.MESH
HBM_SPEC = pl.BlockSpec(memory_space=pltpu.HBM)
VMEM_SPEC = pl.BlockSpec(memory_space=pltpu.VMEM)
SEM_SPEC = pl.BlockSpec(memory_space=pltpu.SEMAPHORE)
SIDE_EFFECT = pltpu.SideEffectType.DATAFLOW_SIDE_EFFECTING

NN = ((1,), (0,))
NT = ((1,), (1,))
TN = ((0,), (0,))


def _tile(n, pref, mult=LANES):
    if n <= pref:
        return n
    t = (pref // mult) * mult
    while t >= mult:
        if n % t == 0:
            return t
        t -= mult
    return n


def _params(sem):
    return pltpu.CompilerParams(dimension_semantics=sem, vmem_limit_bytes=VMEM_LIMIT)


class _Order:
    last = None


def _pcall(body, args, *, token=0, prefetch=0, grid=(), in_specs, out_specs, scratch_shapes=(), same=(), **kw):
    args, in_specs = list(args), list(in_specs)
    n = len(args)
    run = body
    if _Order.last is not None and not any(_Order.last is a for a in [*args, *same]):
        def run(*refs):
            return body(*refs[:n], *refs[n + 1:])
        args.append(_Order.last)
        in_specs.append(pl.BlockSpec(memory_space=pl.ANY))
    if prefetch:
        kw["grid_spec"] = pltpu.PrefetchScalarGridSpec(num_scalar_prefetch=prefetch, grid=grid, in_specs=in_specs,
                                                       out_specs=out_specs, scratch_shapes=list(scratch_shapes))
    else:
        kw.update(grid=grid, in_specs=in_specs, out_specs=out_specs, scratch_shapes=list(scratch_shapes))
    outs = pl.pallas_call(run, **kw)(*args)
    _Order.last = outs[token] if isinstance(outs, (tuple, list)) else outs
    return outs


def _dot(a, b, dims, precision=None):
    return lax.dot_general(a, b, (dims, ((), ())), preferred_element_type=F32, precision=precision)


def _bdot(a, b, dims):
    return _dot(a.astype(BF16), b.astype(BF16), dims)


def _split_bf16(v):
    hi = v.astype(BF16)
    return hi, (v - hi.astype(F32)).astype(BF16)


def _fdot(a, b, dims):
    a_hi, a_lo = _split_bf16(a)
    b_hi, b_lo = _split_bf16(b)
    return _dot(a_hi, b_hi, dims) + (_dot(a_hi, b_lo, dims) + _dot(a_lo, b_hi, dims))


def _mask_dot(mask, v):
    m = mask.astype(BF16)
    hi, lo = _split_bf16(v)
    lo2 = (v - hi.astype(F32) - lo.astype(F32)).astype(BF16)
    return _dot(m, hi, NN) + (_dot(m, lo, NN) + _dot(m, lo2, NN))


def _sigmoid(v):
    return 1.0 / (1.0 + jnp.exp(-v))


def _mm(name, operands, specs, out_shape, out_spec, acc_shape, grid, dims, add=None, add_spec=None):
    n_pairs = len(operands) // 2
    nk = grid[2]

    def body(*refs):
        ins = refs[:2 * n_pairs]
        r_ref = refs[2 * n_pairs] if add is not None else None
        o_ref, acc = refs[-2], refs[-1]
        k = pl.program_id(2)

        @pl.when(k == 0)
        def _():
            acc[...] = jnp.zeros_like(acc)

        part = _bdot(ins[0][...], ins[1][...], dims)
        for q in range(1, n_pairs):
            part = part + _bdot(ins[2 * q][...], ins[2 * q + 1][...], dims)
        acc[...] += part

        @pl.when(k == nk - 1)
        def _():
            r = acc[...]
            if r_ref is not None:
                r = r + r_ref[...].astype(F32)
            o_ref[...] = r.astype(o_ref.dtype)

    in_specs = list(specs) + ([add_spec] if add is not None else [])
    args = list(operands) + ([add] if add is not None else [])
    return _pcall(
        body, args, name=name, grid=grid, in_specs=in_specs, out_specs=out_spec, out_shape=out_shape,
        scratch_shapes=[pltpu.VMEM(acc_shape, F32)],
        compiler_params=_params(("parallel", "parallel", "arbitrary")),
    )


def _rms(v, w):
    r = lax.rsqrt(jnp.mean(v * v, axis=-1, keepdims=True) + EPS)
    return v * r * w


def _rms_bwd(v, w, dy):
    r = lax.rsqrt(jnp.mean(v * v, axis=-1, keepdims=True) + EPS)
    vh = v * r
    gy = dy * w
    dv = r * (gy - vh * jnp.mean(gy * vh, axis=-1, keepdims=True))
    return dv, jnp.sum(dy * vh, axis=0, keepdims=True)


def _rowwise(name, body, ins, in_kinds, outs, out_kinds, T, D):
    tr = _tile(T, 128, 16)
    row = pl.BlockSpec((tr, D), lambda i: (i, 0))
    vec = pl.BlockSpec((1, D), lambda i: (0, 0))
    pick = {"row": row, "vec": vec}
    return _pcall(
        body, ins, name=name, grid=(T // tr,),
        in_specs=[pick[k] for k in in_kinds], out_specs=[pick[k] for k in out_kinds], out_shape=outs,
        compiler_params=_params(("arbitrary",)),
    )


def _acc_vec(ref, val):
    @pl.when(pl.program_id(0) == 0)
    def _():
        ref[...] = jnp.zeros_like(ref)
    ref[...] += val


def _norm_in(x, w):
    T, D = x.shape

    def body(x_ref, w_ref, h_ref):
        h_ref[...] = _rms(x_ref[...], w_ref[...]).astype(BF16)

    return _rowwise("norm_in", body, [x, w], ["row", "vec"], [jax.ShapeDtypeStruct((T, D), BF16)], ["row"], T, D)[0]


def _mid1(x, mix, w_pm, w_pf):
    T, D = x.shape

    def body(x_ref, mix_ref, wpm_ref, wpf_ref, x1_ref, h2_ref):
        x1 = x_ref[...] + _rms(mix_ref[...], wpm_ref[...])
        x1_ref[...] = x1
        h2_ref[...] = _rms(x1, wpf_ref[...]).astype(BF16)

    return _rowwise("mid1", body, [x, mix, w_pm, w_pf], ["row", "row", "vec", "vec"],
                    [jax.ShapeDtypeStruct((T, D), F32), jax.ShapeDtypeStruct((T, D), BF16)], ["row", "row"], T, D)


def _mid2(x1, ff, w_pff):
    T, D = x1.shape

    def body(x1_ref, ff_ref, w_ref, x2_ref, x2b_ref):
        x2 = x1_ref[...] + _rms(ff_ref[...], w_ref[...])
        x2_ref[...] = x2
        x2b_ref[...] = x2.astype(BF16)

    return _rowwise("mid2", body, [x1, ff, w_pff], ["row", "row", "vec"],
                    [jax.ShapeDtypeStruct((T, D), F32), jax.ShapeDtypeStruct((T, D), BF16)], ["row", "row"], T, D)


def _head(x2, gl, e, w_pp, tgt):
    T, D = x2.shape

    def body(x2_ref, gl_ref, e_ref, w_ref, t_ref, d3_ref, de_ref, dgl_ref, dw_ref, loss_ref):
        ev = e_ref[...]
        gate = _sigmoid(gl_ref[...])
        pe = ev * gate
        w = w_ref[...]
        err = x2_ref[...] + _rms(pe, w) - t_ref[...]
        d3 = err * (1.0 / D)
        d3_ref[...] = d3
        dpe, dw = _rms_bwd(pe, w, d3)
        de_ref[...] = (dpe * gate).astype(BF16)
        dgl_ref[...] = (dpe * ev * gate * (1.0 - gate)).astype(BF16)
        _acc_vec(dw_ref, dw)
        _acc_vec(loss_ref, jnp.sum(err * err, axis=0, keepdims=True) * (0.5 / D))

    return _rowwise("head", body, [x2, gl, e, w_pp, tgt], ["row", "row", "row", "vec", "row"],
                    [jax.ShapeDtypeStruct((T, D), F32), jax.ShapeDtypeStruct((T, D), BF16),
                     jax.ShapeDtypeStruct((T, D), BF16), jax.ShapeDtypeStruct((1, D), F32),
                     jax.ShapeDtypeStruct((1, D), F32)], ["row", "row", "row", "vec", "vec"], T, D)


def _bwd_ffn_norm(dx2, ff, w_pff):
    T, D = dx2.shape

    def body(d_ref, ff_ref, w_ref, dff_ref, dw_ref):
        dff, dw = _rms_bwd(ff_ref[...], w_ref[...], d_ref[...])
        dff_ref[...] = dff.astype(BF16)
        _acc_vec(dw_ref, dw)

    return _rowwise("bwd_ffn_norm", body, [dx2, ff, w_pff], ["row", "row", "vec"],
                    [jax.ShapeDtypeStruct((T, D), BF16), jax.ShapeDtypeStruct((1, D), F32)], ["row", "vec"], T, D)


def _bwd_mid(dx2, dh2, x1, w_pf, mix, w_pm):
    T, D = dx2.shape

    def body(dx2_ref, dh2_ref, x1_ref, wpf_ref, mix_ref, wpm_ref, dx1_ref, dmix_ref, dwpf_ref, dwpm_ref):
        d1, dwpf = _rms_bwd(x1_ref[...], wpf_ref[...], dh2_ref[...])
        dx1 = dx2_ref[...] + d1
        dx1_ref[...] = dx1
        dmix, dwpm = _rms_bwd(mix_ref[...], wpm_ref[...], dx1)
        dmix_ref[...] = dmix.astype(BF16)
        _acc_vec(dwpf_ref, dwpf)
        _acc_vec(dwpm_ref, dwpm)

    return _rowwise("bwd_mid", body, [dx2, dh2, x1, w_pf, mix, w_pm], ["row", "row", "row", "vec", "row", "vec"],
                    [jax.ShapeDtypeStruct((T, D), F32), jax.ShapeDtypeStruct((T, D), BF16),
                     jax.ShapeDtypeStruct((1, D), F32), jax.ShapeDtypeStruct((1, D), F32)],
                    ["row", "row", "vec", "vec"], T, D)


def _bwd_in(dx1, dh1, x, w_pre):
    T, D = dx1.shape

    def body(dx1_ref, dh1_ref, x_ref, w_ref, gx_ref, dw_ref):
        d0, dw = _rms_bwd(x_ref[...], w_ref[...], dh1_ref[...])
        gx_ref[...] = dx1_ref[...] + d0
        _acc_vec(dw_ref, dw)

    return _rowwise("bwd_in", body, [dx1, dh1, x, w_pre], ["row", "row", "row", "vec"],
                    [jax.ShapeDtypeStruct((T, D), F32), jax.ShapeDtypeStruct((1, D), F32)], ["row", "vec"], T, D)


def _ffn_up(h2, wg, wu):
    T, D = h2.shape
    S, Fs, _ = wg.shape
    tm, tk = _tile(T, 512), _tile(D, 512)
    nk = D // tk

    def body(h_ref, wg_ref, wu_ref, gt_ref, up_ref, act_ref, accg, accu):
        k = pl.program_id(2)

        @pl.when(k == 0)
        def _():
            accg[...] = jnp.zeros_like(accg)
            accu[...] = jnp.zeros_like(accu)

        h = h_ref[...]
        accg[...] += _dot(h, wg_ref[...], NT)
        accu[...] += _dot(h, wu_ref[...], NT)

        @pl.when(k == nk - 1)
        def _():
            g, u = accg[...], accu[...]
            gt_ref[...] = g.astype(BF16)
            up_ref[...] = u.astype(BF16)
            act_ref[...] = (g * _sigmoid(g) * u).astype(BF16)

    w_spec = pl.BlockSpec((None, Fs, tk), lambda i, j, k: (j, 0, k))
    o_spec = pl.BlockSpec((None, tm, Fs), lambda i, j, k: (j, i, 0))
    return _pcall(
        body, [h2, wg, wu], name="ffn_up", grid=(T // tm, S, nk),
        in_specs=[pl.BlockSpec((tm, tk), lambda i, j, k: (i, k)), w_spec, w_spec],
        out_specs=[o_spec, o_spec, o_spec],
        out_shape=[jax.ShapeDtypeStruct((S, T, Fs), BF16)] * 3,
        scratch_shapes=[pltpu.VMEM((tm, Fs), F32), pltpu.VMEM((tm, Fs), F32)],
        compiler_params=_params(("parallel", "parallel", "arbitrary")),
    )


def _ffn_dact(dff, wd, gt, up):
    T, D = dff.shape
    S, Fs, _ = wd.shape
    tm, tk = _tile(T, 512), _tile(D, 1024)
    nk = D // tk
    te = _tile(tm, 128, 16)

    def body(d_ref, w_ref, gt_ref, up_ref, dgt_ref, dup_ref, acc):
        k = pl.program_id(2)

        @pl.when(k == 0)
        def _():
            acc[...] = jnp.zeros_like(acc)

        acc[...] += _dot(d_ref[...], w_ref[...], NT)

        @pl.when(k == nk - 1)
        def _():
            for r in range(0, tm, te):
                rows = slice(r, r + te)
                da, g, u = acc[rows, :], gt_ref[rows, :].astype(F32), up_ref[rows, :].astype(F32)
                sg = _sigmoid(g)
                dgt_ref[rows, :] = (da * u * sg * (1.0 + g * (1.0 - sg))).astype(BF16)
                dup_ref[rows, :] = (da * g * sg).astype(BF16)

    o_spec = pl.BlockSpec((None, tm, Fs), lambda i, j, k: (j, i, 0))
    return _pcall(
        body, [dff, wd, gt, up], name="ffn_dact", grid=(T // tm, S, nk),
        in_specs=[pl.BlockSpec((tm, tk), lambda i, j, k: (i, k)),
                  pl.BlockSpec((None, Fs, tk), lambda i, j, k: (j, 0, k)), o_spec, o_spec],
        out_specs=[o_spec, o_spec],
        out_shape=[jax.ShapeDtypeStruct((S, T, Fs), BF16), jax.ShapeDtypeStruct((S, T, Fs), BF16)],
        scratch_shapes=[pltpu.VMEM((tm, Fs), F32)],
        compiler_params=_params(("parallel", "parallel", "arbitrary")),
    )


def _hgrn2_gates(q, f, lb):
    C = GLA_CHUNK
    sq = _sigmoid(q)
    qs = q * sq
    sf = _sigmoid(f)
    fg = lb + (1.0 - lb) * sf
    lf = jnp.log(jnp.maximum(fg, 1e-30))
    row = lax.broadcasted_iota(jnp.int32, (C, C), 0)
    col = lax.broadcasted_iota(jnp.int32, (C, C), 1)
    tril = row >= col
    b = _mask_dot(jnp.where(tril, 1.0, 0.0), lf)
    bmid = b[C // 2 - 1:C // 2, :]
    bend = b[C - 1:C, :]
    ea = jnp.exp(b - bmid)
    ena = jnp.exp(bmid - b)
    eb = ea * jnp.exp(bmid)
    ee = ena * jnp.exp(bend - bmid)
    return sq, qs, sf, fg, tril, ea, ena, eb, ee, jnp.exp(bend)


def _lower_bound(lbp):
    return _sigmoid(lbp[0:1, :] - lbp[1:2, :])


def _hgrn2_fwd(proj6, lb_param, a_norm_w):
    _, T, Wd = proj6.shape
    H, C = Wd // HEAD, GLA_CHUNK
    TB = _tile(T, 512, C)
    NB, NCB = T // TB, TB // C

    HS, W2 = HEADS_PER_STEP, HEADS_PER_STEP * HEAD

    def body(p_ref, lbp_ref, w_ref, cat_ref, o_ref, st_ref, S):
        @pl.when(pl.program_id(1) == 0)
        def _():
            S[...] = jnp.zeros_like(S)

        lanes = [slice(hh * HEAD, (hh + 1) * HEAD) for hh in range(HS)]
        lbs = [_lower_bound(lbp_ref[:, ls]) for ls in lanes]
        ws = [w_ref[:, ls] for ls in lanes]

        def chunk(c, carry):
            rows = pl.ds(pl.multiple_of(c * C, C), C)
            loaded = [([p_ref[s, rows, ls] for s in range(4)], S[hh]) for hh, ls in enumerate(lanes)]
            done = []
            for hh, ((q, f, v, g), st) in enumerate(loaded):
                _, qs, _, fg, tril, ea, ena, eb, ee, ebend = _hgrn2_gates(q, f, lbs[hh])
                k = 1.0 - fg
                p = jnp.where(tril, _bdot(qs * ea, k * ena, NT), 0.0)
                o = _bdot(qs * eb, st, NT) + _bdot(p, v, NN)
                r = lax.rsqrt(jnp.mean(o * o, axis=-1, keepdims=True) + EPS)
                done.append((st, st * ebend + _bdot(v, k * ee, TN), o,
                             (o * r * ws[hh] * (g * _sigmoid(g))).astype(BF16)))
            for hh, ls in enumerate(lanes):
                st, st_new, o, a_out = done[hh]
                st_ref[hh, c] = st
                S[hh] = st_new
                o_ref[rows, ls] = o
                cat_ref[rows, ls] = a_out
            return carry

        lax.fori_loop(0, NCB, chunk, 0)

    return _pcall(
        body, [proj6, lb_param, a_norm_w], name="hgrn2_fwd", grid=(H // HS, NB),
        in_specs=[pl.BlockSpec((4, TB, W2), lambda h, t: (0, t, h)),
                  pl.BlockSpec((2, W2), lambda h, t: (0, h)),
                  pl.BlockSpec((1, W2), lambda h, t: (0, h))],
        out_specs=[pl.BlockSpec((None, TB, W2), lambda h, t: (0, t, h)),
                   pl.BlockSpec((TB, W2), lambda h, t: (t, h)),
                   pl.BlockSpec((HS, NCB, HEAD, HEAD), lambda h, t: (h, t, 0, 0))],
        out_shape=[jax.ShapeDtypeStruct((2, T, Wd), BF16), jax.ShapeDtypeStruct((T, Wd), F32),
                   jax.ShapeDtypeStruct((H, T // C, HEAD, HEAD), F32)],
        scratch_shapes=[pltpu.VMEM((HS, HEAD, HEAD), F32)],
        compiler_params=_params(("parallel", "arbitrary")),
    )


def _hgrn2_bwd(proj6, o, states, dcat, lb_param, a_norm_w):
    _, T, Wd = proj6.shape
    H, C = Wd // HEAD, GLA_CHUNK
    TB = _tile(T, 512, C)
    NB, NCB = T // TB, TB // C
    HS, W2 = HEADS_PER_STEP, HEADS_PER_STEP * HEAD

    def body(p_ref, o_ref, st_ref, da_ref, lbp_ref, w_ref, dp_ref, dw_ref, dlb_ref, dS):
        @pl.when(pl.program_id(1) == 0)
        def _():
            dS[...] = jnp.zeros_like(dS)
            dw_ref[...] = jnp.zeros_like(dw_ref)
            dlb_ref[...] = jnp.zeros_like(dlb_ref)

        lanes = [slice(hh * HEAD, (hh + 1) * HEAD) for hh in range(HS)]
        lbs = [_lower_bound(lbp_ref[:, ls]) for ls in lanes]
        ws = [w_ref[:, ls] for ls in lanes]

        def chunk(i, carry):
            c = NCB - 1 - i
            rows = pl.ds(pl.multiple_of(c * C, C), C)
            loaded = [([p_ref[s, rows, ls] for s in range(4)], st_ref[hh, c], dS[hh], o_ref[rows, ls], da_ref[rows, ls],
                       dw_ref[:, ls], dlb_ref[:, ls]) for hh, ls in enumerate(lanes)]
            done = []
            for hh, ((q, f, v, g), st, dst, ov, da, dw_old, dlb_old) in enumerate(loaded):
                lb, w = lbs[hh], ws[hh]
                sq, qs, sf, fg, tril, ea, ena, eb, ee, ebend = _hgrn2_gates(q, f, lb)
                k = 1.0 - fg
                qt, kt, qb, ke = qs * ea, k * ena, qs * eb, k * ee
                p = jnp.where(tril, _bdot(qt, kt, NT), 0.0)
                r = lax.rsqrt(jnp.mean(ov * ov, axis=-1, keepdims=True) + EPS)
                oh = ov * r
                sg = _sigmoid(g)
                gs = g * sg
                dw_new = dw_old + jnp.sum(da * oh * gs, axis=0, keepdims=True)
                dg = da * oh * w * sg * (1.0 + g * (1.0 - sg))
                doh = da * w * gs
                do = r * (doh - oh * jnp.mean(doh * oh, axis=-1, keepdims=True))
                dqb = _fdot(do, st, NN)
                dp = jnp.where(tril, _bdot(do, v, NT), 0.0)
                dv = _bdot(p, do, TN) + _bdot(ke, dst, NT)
                dqt = _fdot(dp, kt, NN)
                dkt = _fdot(dp, qt, TN)
                dke = _fdot(v, dst, NN)
                dqs = dqt * ea + dqb * eb
                dk = dkt * ena + dke * ee
                db = dqt * qt + dqb * qb - dkt * kt - dke * ke
                db_end = (jnp.sum(dke * ke, axis=0, keepdims=True)
                          + ebend * jnp.sum(dst * st, axis=0, keepdims=True))
                triu = lax.broadcasted_iota(jnp.int32, (C, C), 0) <= lax.broadcasted_iota(jnp.int32, (C, C), 1)
                dlf = _mask_dot(jnp.where(triu, 1.0, 0.0), db) + db_end
                dfg = jnp.where(fg > 1e-30, dlf / fg, 0.0) - dk
                done.append((dw_new, dlb_old + jnp.sum(dfg * (1.0 - sf), axis=0, keepdims=True),
                             [(dqs * sq * (1.0 + q * (1.0 - sq))).astype(BF16),
                              (dfg * (1.0 - lb) * sf * (1.0 - sf)).astype(BF16), dv.astype(BF16), dg.astype(BF16)],
                             dst * ebend + _bdot(do, qb, TN)))
            for hh, ls in enumerate(lanes):
                dw_new, dlb_new, dsec, dst_new = done[hh]
                dw_ref[:, ls] = dw_new
                dlb_ref[:, ls] = dlb_new
                for s in range(4):
                    dp_ref[s, rows, ls] = dsec[s]
                dS[hh] = dst_new
            return carry

        lax.fori_loop(0, NCB, chunk, 0)

    rev = lambda t: NB - 1 - t
    return _pcall(
        body, [proj6, o, states, dcat, lb_param, a_norm_w], name="hgrn2_bwd", grid=(H // HS, NB),
        in_specs=[pl.BlockSpec((4, TB, W2), lambda h, t: (0, rev(t), h)),
                  pl.BlockSpec((TB, W2), lambda h, t: (rev(t), h)),
                  pl.BlockSpec((HS, NCB, HEAD, HEAD), lambda h, t: (h, rev(t), 0, 0)),
                  pl.BlockSpec((None, TB, W2), lambda h, t: (0, rev(t), h)),
                  pl.BlockSpec((2, W2), lambda h, t: (0, h)),
                  pl.BlockSpec((1, W2), lambda h, t: (0, h))],
        out_specs=[pl.BlockSpec((4, TB, W2), lambda h, t: (0, rev(t), h)),
                   pl.BlockSpec((1, W2), lambda h, t: (0, h)),
                   pl.BlockSpec((1, W2), lambda h, t: (0, h))],
        out_shape=[jax.ShapeDtypeStruct((6, T, Wd), BF16), jax.ShapeDtypeStruct((1, Wd), F32),
                   jax.ShapeDtypeStruct((1, Wd), F32)],
        scratch_shapes=[pltpu.VMEM((HS, HEAD, HEAD), F32)],
        compiler_params=_params(("parallel", "arbitrary")),
    )


_INV_SQRT2 = 1.0 / math.sqrt(2.0)
_INV_SQRT2PI = 1.0 / math.sqrt(2.0 * math.pi)


def _gelu(v):
    return 0.5 * v * (1.0 + lax.erf(v * _INV_SQRT2))


def _gelu_grad(v):
    return 0.5 * (1.0 + lax.erf(v * _INV_SQRT2)) + v * jnp.exp(-0.5 * v * v) * _INV_SQRT2PI


def _gmlp_norm(v, ln_w, ln_b):
    vf = _gelu(v)
    mu = jnp.mean(vf, axis=-1, keepdims=True)
    cen = vf - mu
    rstd = lax.rsqrt(jnp.mean(cen * cen, axis=-1, keepdims=True) + EPS)
    vh = cen * rstd
    return vh, rstd, vh * ln_w + ln_b


def _tril_mask():
    n = GMLP_CHUNK
    return lax.broadcasted_iota(jnp.int32, (n, n), 0) >= lax.broadcasted_iota(jnp.int32, (n, n), 1)


def _gmlp_fwd(proj6, cat, ln_w, ln_b, w_s, b_st):
    _, T, Wd = proj6.shape
    G, N = Wd // HEAD, GMLP_CHUNK

    def body(p_ref, cat_in, lnw_ref, lnb_ref, ws_ref, bst_ref, out_ref):
        del cat_in
        _, _, vn = _gmlp_norm(p_ref[1], lnw_ref[...], lnb_ref[...])
        tril = _tril_mask()
        for g in range(G):
            cols = slice(g * HEAD, (g + 1) * HEAD)
            z = _bdot(jnp.where(tril, ws_ref[g], 0.0), vn[:, cols], NN) + bst_ref[:, g:g + 1]
            out_ref[:, cols] = (_gelu(p_ref[0, :, cols]) * z).astype(BF16)

    return _pcall(
        body, [proj6, cat, ln_w, ln_b, w_s, b_st], name="gmlp_fwd", grid=(T // N,),
        in_specs=[pl.BlockSpec((2, N, Wd), lambda t: (2, t, 0)), pl.BlockSpec(memory_space=pl.ANY),
                  pl.BlockSpec((1, Wd), lambda t: (0, 0)), pl.BlockSpec((1, Wd), lambda t: (0, 0)),
                  pl.BlockSpec((G, N, N), lambda t: (0, 0, 0)), pl.BlockSpec((N, LANES), lambda t: (0, 0))],
        out_specs=pl.BlockSpec((None, N, Wd), lambda t: (1, t, 0)),
        out_shape=jax.ShapeDtypeStruct(cat.shape, cat.dtype),
        input_output_aliases={1: 0},
        compiler_params=_params(("arbitrary",)),
    )


def _gmlp_bwd(proj6, dproj6, dcat, ln_w, ln_b, w_s, b_st):
    _, T, Wd = proj6.shape
    G, N = Wd // HEAD, GMLP_CHUNK

    def body(p_ref, dp_in, db_ref, lnw_ref, lnb_ref, ws_ref, bst_ref, dp_ref, dlnw_ref, dlnb_ref, dws_ref, dbst_ref, dvn):
        del dp_in

        @pl.when(pl.program_id(0) == 0)
        def _():
            dlnw_ref[...] = jnp.zeros_like(dlnw_ref)
            dlnb_ref[...] = jnp.zeros_like(dlnb_ref)
            dws_ref[...] = jnp.zeros_like(dws_ref)
            dbst_ref[...] = jnp.zeros_like(dbst_ref)

        lnw = lnw_ref[...]
        v = p_ref[1]
        vh, rstd, vn = _gmlp_norm(v, lnw, lnb_ref[...])
        tril = _tril_mask()
        lane = lax.broadcasted_iota(jnp.int32, (N, LANES), 1)
        dbst = jnp.zeros((N, LANES), F32)
        for g in range(G):
            cols = slice(g * HEAD, (g + 1) * HEAD)
            wt = jnp.where(tril, ws_ref[g], 0.0)
            vn_g = vn[:, cols]
            z = _bdot(wt, vn_g, NN) + bst_ref[:, g:g + 1]
            u = p_ref[0, :, cols]
            db = db_ref[:, cols]
            dp_ref[0, :, cols] = (db * z * _gelu_grad(u)).astype(BF16)
            dz = db * _gelu(u)
            dbst = dbst + jnp.where(lane == g, jnp.sum(dz, axis=1, keepdims=True), 0.0)
            dws_ref[g] += jnp.where(tril, _bdot(dz, vn_g, NT), 0.0)
            dvn[:, cols] = _bdot(wt, dz, TN)
        dbst_ref[...] += dbst
        dv = dvn[...]
        dlnw_ref[...] += jnp.sum(dv * vh, axis=0, keepdims=True)
        dlnb_ref[...] += jnp.sum(dv, axis=0, keepdims=True)
        dvh = dv * lnw
        dvf = rstd * (dvh - jnp.mean(dvh, axis=-1, keepdims=True) - vh * jnp.mean(dvh * vh, axis=-1, keepdims=True))
        dp_ref[1] = (dvf * _gelu_grad(v)).astype(BF16)

    vec = pl.BlockSpec((1, Wd), lambda t: (0, 0))
    return _pcall(
        body, [proj6, dproj6, dcat, ln_w, ln_b, w_s, b_st], name="gmlp_bwd", grid=(T // N,),
        in_specs=[pl.BlockSpec((2, N, Wd), lambda t: (2, t, 0)), pl.BlockSpec(memory_space=pl.ANY),
                  pl.BlockSpec((None, N, Wd), lambda t: (1, t, 0)), vec, vec,
                  pl.BlockSpec((G, N, N), lambda t: (0, 0, 0)), pl.BlockSpec((N, LANES), lambda t: (0, 0))],
        out_specs=[pl.BlockSpec((2, N, Wd), lambda t: (2, t, 0)), vec, vec,
                   pl.BlockSpec((G, N, N), lambda t: (0, 0, 0)), pl.BlockSpec((N, LANES), lambda t: (0, 0))],
        out_shape=[jax.ShapeDtypeStruct(dproj6.shape, dproj6.dtype), jax.ShapeDtypeStruct((1, Wd), F32),
                   jax.ShapeDtypeStruct((1, Wd), F32), jax.ShapeDtypeStruct((G, N, N), F32),
                   jax.ShapeDtypeStruct((N, LANES), F32)],
        scratch_shapes=[pltpu.VMEM((N, Wd), F32)],
        input_output_aliases={1: 0},
        compiler_params=_params(("arbitrary",)),
    )


def _place():
    return lax.axis_index("x"), lax.axis_index("y"), lax.axis_index("c")


def _other_chips(x, y):
    return [(1 - x, y), (x, 1 - y), (1 - x, 1 - y)]


def _rcopy(src, dst, ssem, rsem, k, to):
    return pltpu.make_async_remote_copy(src_ref=src, dst_ref=dst, send_sem=ssem.at[k], recv_sem=rsem.at[k],
                                        device_id=to, device_id_type=MESH)


def _cast_slot(name, w, place):
    _, R, C = w.shape
    tr = _row_block(R, C)

    def body(p_ref, w_ref, o_ref):
        del p_ref
        o_ref[...] = w_ref[...].astype(BF16)

    return _pcall(
        body, [place[0], w], name=name, prefetch=1, grid=(R // tr,),
        in_specs=[pl.BlockSpec((None, tr, C), lambda i, p: (0, i, 0))],
        out_specs=pl.BlockSpec((None, tr, C), lambda i, p: (p[0], i, 0)),
        out_shape=jax.ShapeDtypeStruct((4, R, C), BF16),
        compiler_params=_params(("parallel",)))


def _half(ref, px, py, pc, Rh):
    return ref.at[2 * px + py, pl.ds(pc * Rh, Rh), :]


def _split_start(name, bufs, n_sem, copies):
    nb = len(bufs)

    def body(*refs):
        for cp in copies(refs[:nb], refs[nb], refs[nb + 1], False):
            cp.start()
        refs[-1][...] = jnp.zeros_like(refs[-1])

    outs = _pcall(
        body, [pltpu.with_memory_space_constraint(b, pltpu.HBM) for b in bufs], same=bufs, token=-1, name=name,
        in_specs=[HBM_SPEC] * nb, out_specs=(SEM_SPEC, SEM_SPEC, *[HBM_SPEC] * nb, VMEM_SPEC),
        out_shape=(pltpu.SemaphoreType.DMA((n_sem,)), pltpu.SemaphoreType.DMA((n_sem,)),
                   *[pltpu.HBM(b.shape, b.dtype) for b in bufs], jax.ShapeDtypeStruct((8, LANES), F32)),
        input_output_aliases={i: 2 + i for i in range(nb)},
        compiler_params=pltpu.CompilerParams(has_side_effects=SIDE_EFFECT))
    return outs[:-1]


def _split_wait(name, started, copies):
    ssem, rsem, *bufs = started
    nb = len(bufs)

    def body(*refs):
        for cp in copies(refs[:nb], refs[nb], refs[nb + 1], True):
            cp.wait_send()
            cp.wait_recv()

    outs = _pcall(
        body, [*bufs, ssem, rsem], name=name, in_specs=[HBM_SPEC] * nb + [SEM_SPEC, SEM_SPEC],
        out_specs=tuple([HBM_SPEC] * nb), out_shape=tuple(pltpu.HBM(b.shape, b.dtype) for b in bufs),
        input_output_aliases={i: i for i in range(nb)},
        compiler_params=pltpu.CompilerParams(has_side_effects=SIDE_EFFECT))
    return outs


def _gather_copies(Rh):
    def copies(refs, ssem, rsem, waiting):
        x, y, c = _place()
        mine = _half(refs[0], x, y, c, Rh)
        return [_rcopy(mine, _half(refs[0], px, py, c, Rh) if waiting else mine, ssem, rsem, j, (px, py, c))
                for j, (px, py) in enumerate(_other_chips(x, y))]
    return copies


def _swap_copies(Rh):
    def copies(refs, ssem, rsem, waiting):
        x, y, c = _place()
        return [_rcopy(refs[0].at[:, pl.ds((1 - c) * Rh, Rh), :], refs[1], ssem, rsem, 0, (x, y, 1 - c))]
    return copies


def _exchange_copies(refs, ssem, rsem, waiting):
    x, y, c = _place()
    return [_rcopy(refs[0].at[2 * px + py], refs[1].at[j], ssem, rsem, j, (px, py, c))
            for j, (px, py) in enumerate(_other_chips(x, y))]


def _landing(shape, dtype):
    return lax.empty(shape, dtype)


def _forward_copies(Rh):
    def copies(refs, ssem, rsem, waiting):
        x, y, c = _place()
        return [_rcopy(_half(refs[0], px, py, c, Rh), _half(refs[0], px, py, 1 - c if waiting else c, Rh),
                       ssem, rsem, j, (x, y, 1 - c)) for j, (px, py) in enumerate(_other_chips(x, y))]
    return copies


def _share_copies(Rh):
    def copies(refs, ssem, rsem, waiting):
        x, y, c = _place()
        mine = refs[0].at[pl.ds(c * Rh, Rh), :]
        return [_rcopy(mine, refs[0].at[pl.ds((1 - c) * Rh, Rh), :] if waiting else mine, ssem, rsem, 0, (x, y, 1 - c))]
    return copies


def _sum_all_devices(v):
    R, L = v.shape

    def body(v_ref, out_ref, buf, ssem, rsem):
        x, y, c = _place()
        me = 4 * x + 2 * y + c
        buf[me] = v_ref[...]
        copies = []
        for m in range(1, 8):
            to = (x ^ (m >> 2), y ^ ((m >> 1) & 1), c ^ (m & 1))
            copies.append(_rcopy(v_ref, buf.at[me], ssem, rsem, m - 1, to))
        for cp in copies:
            cp.start()
        for cp in copies:
            cp.wait()
        acc = buf[0]
        for d in range(1, 8):
            acc = acc + buf[d]
        out_ref[...] = acc

    return _pcall(
        body, [v], name="sum_all_devices", in_specs=[VMEM_SPEC], out_specs=VMEM_SPEC,
        out_shape=jax.ShapeDtypeStruct((R, L), F32),
        scratch_shapes=[pltpu.VMEM((8, R, L), F32), pltpu.SemaphoreType.DMA((7,)), pltpu.SemaphoreType.DMA((7,))],
        compiler_params=pltpu.CompilerParams(vmem_limit_bytes=VMEM_LIMIT))


def _elementwise(name, body, ins, in_specs, outs, out_specs, grid):
    return _pcall(body, ins, name=name, grid=grid, in_specs=in_specs, out_specs=out_specs, out_shape=outs,
                  compiler_params=_params(("parallel",) * len(grid)))


def _row_block(R, C, bytes_per_row_elem=4, target=1 << 20):
    return _tile(R, max(16, target // (C * bytes_per_row_elem)), 16)


def _add_halves(name, g4, land, c_idx):
    S, R, C = g4.shape
    Rh = R // 2
    tr = _row_block(Rh, C)
    nb = Rh // tr

    def body(c_ref, g_ref, l_ref, o_ref):
        del c_ref
        o_ref[...] = (g_ref[...].astype(F32) + l_ref[...].astype(F32)).astype(BF16)

    return _pcall(
        body, [c_idx, g4, land], name=name, prefetch=1, grid=(S, nb),
        in_specs=[pl.BlockSpec((None, tr, C), lambda s, i, c: (s, c[0] * nb + i, 0)),
                  pl.BlockSpec((None, tr, C), lambda s, i, c: (s, i, 0))],
        out_specs=pl.BlockSpec((None, tr, C), lambda s, i, c: (s, i, 0)),
        out_shape=jax.ShapeDtypeStruct((S, Rh, C), BF16),
        compiler_params=_params(("parallel", "parallel")))


def _sum_chips(name, land, pb, place):
    S, Rh, C = land.shape
    tr = _row_block(Rh, C)
    nb = Rh // tr

    def body(p_ref, c_ref, l_ref, own_ref, o_ref):
        del p_ref, c_ref
        acc = own_ref[...].astype(F32)
        for s in range(S):
            acc = acc + l_ref[s].astype(F32)
        o_ref[...] = acc

    return _pcall(
        body, [place[0], place[1], land, pb], name=name, prefetch=2, grid=(nb,),
        in_specs=[pl.BlockSpec((S, tr, C), lambda i, p, c: (0, i, 0)),
                  pl.BlockSpec((None, tr, C), lambda i, p, c: (p[0], i, 0))],
        out_specs=pl.BlockSpec((tr, C), lambda i, p, c: (c[0] * nb + i, 0)),
        out_shape=jax.ShapeDtypeStruct((2 * Rh, C), F32),
        compiler_params=_params(("parallel",)))


def _adamw(name, w, g, m, v):
    R, C = g.shape
    tr = _row_block(R, C, target=1 << 19)
    c1 = 1.0 - ADAM_B1 ** ADAM_STEP
    c2 = 1.0 - ADAM_B2 ** ADAM_STEP

    def body(w_ref, g_ref, m_ref, v_ref, go_ref, d_ref, nm_ref, nv_ref):
        gv = g_ref[...]
        go_ref[...] = gv
        nm = ADAM_B1 * m_ref[...] + (1.0 - ADAM_B1) * gv
        nv = ADAM_B2 * v_ref[...] + (1.0 - ADAM_B2) * (gv * gv)
        nm_ref[...] = nm
        nv_ref[...] = nv
        d_ref[...] = -ADAM_LR * ((nm / c1) / (jnp.sqrt(nv / c2) + ADAM_EPS) + ADAM_WD * w_ref[...])

    spec = pl.BlockSpec((tr, C), lambda i: (i, 0))
    wspec = pl.BlockSpec((None, tr, C), lambda i: (0, i, 0)) if w.ndim == 3 else spec
    shp = jax.ShapeDtypeStruct((R, C), F32)
    return _elementwise(name, body, [w, g, m, v], [wspec, spec, wspec, wspec], [shp] * 4, [spec] * 4, (R // tr,))


def _lb_grad(dlb, lb_param):
    def body(d_ref, p_ref, o_ref):
        lb = _lower_bound(p_ref[...])
        t = d_ref[...] * lb * (1.0 - lb)
        o_ref[0:1, :] = t
        o_ref[1:2, :] = -t

    return _pcall(body, [dlb, lb_param], name="lb_grad", in_specs=[VMEM_SPEC, VMEM_SPEC], out_specs=VMEM_SPEC,
                  out_shape=jax.ShapeDtypeStruct(lb_param.shape, F32))


class _Reduction:
    def __init__(self, tag, g4, place):
        self.tag, self.place = tag, place
        _, R, C = g4.shape
        self.Rh = R // 2
        self.state = _split_start("swap_start_" + tag, [g4, _landing((4, self.Rh, C), g4.dtype)], 1, _swap_copies(self.Rh))

    def exchange(self):
        g4, land = _split_wait("swap_wait_" + self.tag, self.state, _swap_copies(self.Rh))
        pb = _add_halves("addh_" + self.tag, g4, land, self.place[1])
        self.state = _split_start("xchg_start_" + self.tag, [pb, _landing((3,) + pb.shape[1:], BF16)], 3, _exchange_copies)

    def finish(self):
        pb, got = _split_wait("xchg_wait_" + self.tag, self.state, _exchange_copies)
        full = _sum_chips("sumc_" + self.tag, got, pb, self.place)
        self.state = _split_start("share_start_" + self.tag, [full], 1, _share_copies(self.Rh))

    def update(self, w, m, v):
        (full,) = _split_wait("share_wait_" + self.tag, self.state, _share_copies(self.Rh))
        return tuple(a[None] for a in _adamw("adamw_" + self.tag, w, full, m, v))


def kernel(x, p, pre_mix_w, w_in, lb_param, a_norm_w, gmlp_ln_w, gmlp_ln_b, w_spatial, b_spatial, w_out, post_mix_w, pre_ffn_w, w_gate, w_up, w_down, post_ffn_w, w_ple, w_ple_gate, post_ple_w, loss_target, m_pre_mix_w, m_w_in, m_lb_param, m_a_norm_w, m_gmlp_ln_w, m_gmlp_ln_b, m_w_spatial, m_b_spatial, m_w_out, m_post_mix_w, m_pre_ffn_w, m_w_gate, m_w_up, m_w_down, m_post_ffn_w, m_w_ple, m_w_ple_gate, m_post_ple_w, v_pre_mix_w, v_w_in, v_lb_param, v_a_norm_w, v_gmlp_ln_w, v_gmlp_ln_b, v_w_spatial, v_b_spatial, v_w_out, v_post_mix_w, v_pre_ffn_w, v_w_gate, v_w_up, v_w_down, v_post_ffn_w, v_w_ple, v_w_ple_gate, v_post_ple_w):
    T, D = x.shape[1], x.shape[2]
    Wd = D // 2
    G = Wd // HEAD
    NCH = 4
    Ci = w_in.shape[2]
    Fs = w_gate.shape[2]
    Dq = w_out.shape[1]
    Pd = w_ple.shape[1]
    xs, ps, tgt = x[0], p[0, 0], loss_target[0]
    c_idx = lax.axis_index("c").astype(jnp.int32).reshape((1,))
    place = ((2 * lax.axis_index("x") + lax.axis_index("y")).astype(jnp.int32).reshape((1,)), c_idx)

    swap = lambda a: jnp.swapaxes(a, 1, 2)
    transposed = ("w_gate", "w_up")
    weights = dict(w_in=w_in, w_out=w_out, w_gate=swap(w_gate), w_up=swap(w_up), w_down=w_down,
                   w_ple_gate=w_ple_gate, w_ple=w_ple)
    _Order.last = None
    gathering = {}
    for tag, w in weights.items():
        buf = _cast_slot("cast_" + tag, w, place)
        gathering[tag] = _split_start("gather_start_" + tag, [buf], 3, _gather_copies(buf.shape[1] // 2))

    forwarding = {}

    def forward_start(tag):
        Rh = weights[tag].shape[1] // 2
        (buf,) = _split_wait("gather_wait_" + tag, gathering.pop(tag), _gather_copies(Rh))
        forwarding[tag] = _split_start("forward_start_" + tag, [buf], 3, _forward_copies(Rh))

    def gathered(tag):
        (buf,) = _split_wait("forward_wait_" + tag, forwarding.pop(tag), _forward_copies(weights[tag].shape[1] // 2))
        return buf

    res = {}
    moments = dict(w_in=(m_w_in, v_w_in), w_out=(m_w_out, v_w_out), w_gate=(swap(m_w_gate), swap(v_w_gate)),
                   w_up=(swap(m_w_up), swap(v_w_up)), w_down=(m_w_down, v_w_down), w_ple=(m_w_ple, v_w_ple),
                   w_ple_gate=(m_w_ple_gate, v_w_ple_gate))
    reducing = {}

    def reduce_start(tag, g4):
        reducing[tag] = _Reduction(tag, g4, place)

    def reduce_exchange(tag):
        reducing[tag].exchange()

    def reduce_finish(tag):
        reducing[tag].finish()

    def reduce_update(tag):
        outs = reducing.pop(tag).update(weights[tag], *moments[tag])
        res[tag] = tuple(swap(a) for a in outs) if tag in transposed else outs

    tm, tn, tk = _tile(T, 1024), _tile(D, 1024), _tile(Wd, 1024)
    ts = _tile(math.gcd(Wd, Ci), 1024)
    tkt = _tile(T, 1024)
    n_sec, n_shd = Wd // ts, Ci // ts
    b_st = jnp.pad(b_spatial[0].T, ((0, 0), (0, LANES - G)))

    forward_start("w_in")
    h1 = _norm_in(xs, pre_mix_w)
    gw_in = gathered("w_in")
    proj6 = _mm("proj", [h1, gw_in],
                [pl.BlockSpec((tm, tk), lambda i, j, k: (i, k)),
                 pl.BlockSpec((None, tk, ts), lambda i, j, k: (j // n_shd, k, j % n_shd))],
                jax.ShapeDtypeStruct((6, T, Wd), F32),
                pl.BlockSpec((None, tm, ts), lambda i, j, k: (j // n_sec, i, j % n_sec)),
                (tm, ts), (T // tm, 6 * n_sec, D // tk), NN)
    forward_start("w_out")
    cat, o_a, states = _hgrn2_fwd(proj6, lb_param, a_norm_w)
    forward_start("w_gate")
    cat = _gmlp_fwd(proj6, cat, gmlp_ln_w, gmlp_ln_b, w_spatial[0], b_st)
    nkc = Wd // tk
    gw_out = gathered("w_out").reshape(D, D)
    mix = _mm("mix", [cat, gw_out],
              [pl.BlockSpec((None, tm, tk), lambda i, j, k: (k // nkc, i, k % nkc)),
               pl.BlockSpec((tk, tn), lambda i, j, k: (k, j))],
              jax.ShapeDtypeStruct((T, D), F32), pl.BlockSpec((tm, tn), lambda i, j, k: (i, j)),
              (tm, tn), (T // tm, D // tn, D // tk), NN)
    forward_start("w_up")
    x1, h2 = _mid1(xs, mix, post_mix_w, pre_ffn_w)
    gw_gate = gathered("w_gate")
    gw_up = gathered("w_up")
    gt, up, act = _ffn_up(h2, gw_gate, gw_up)
    tnf = _tile(D, 512)
    forward_start("w_down")
    forward_start("w_ple_gate")
    forward_start("w_ple")
    gw_down = gathered("w_down")
    ff = _mm("ffn_down", [act, gw_down],
             [pl.BlockSpec((None, tm, Fs), lambda i, j, k: (k, i, 0)),
              pl.BlockSpec((None, Fs, tnf), lambda i, j, k: (k, 0, j))],
             jax.ShapeDtypeStruct((T, D), F32), pl.BlockSpec((tm, tnf), lambda i, j, k: (i, j)),
             (tm, tnf), (T // tm, D // tnf, NCH), NN)
    x2, x2b = _mid2(x1, ff, post_ffn_w)
    gw_pg = gathered("w_ple_gate").reshape(D, D)
    gl = _mm("ple_gate", [x2b, gw_pg],
             [pl.BlockSpec((tm, tk), lambda i, j, k: (i, k)), pl.BlockSpec((tk, tn), lambda i, j, k: (k, j))],
             jax.ShapeDtypeStruct((T, D), F32), pl.BlockSpec((tm, tn), lambda i, j, k: (i, j)),
             (tm, tn), (T // tm, D // tn, D // tk), NN)
    tq = _tile(Dq, 1024)
    nq = Dq // tq
    gw_ple = gathered("w_ple")
    e = _mm("ple_embed", [ps, gw_ple],
            [pl.BlockSpec((tm, Pd), lambda i, j, k: (i, 0)),
             pl.BlockSpec((None, Pd, tq), lambda i, j, k: (j // nq, 0, j % nq))],
            jax.ShapeDtypeStruct((T, D), F32), pl.BlockSpec((tm, tq), lambda i, j, k: (i, j)),
            (tm, tq), (T // tm, D // tq, 1), NN)

    d3, de, dgl, dw_pp, loss_vec = _head(x2, gl, e, post_ple_w, tgt)
    g_ple = _mm("d_w_ple", [ps, de],
                [pl.BlockSpec((tkt, Pd), lambda i, j, k: (k, 0)), pl.BlockSpec((tkt, tq), lambda i, j, k: (k, j))],
                jax.ShapeDtypeStruct((NCH, Pd, Dq), PARTIAL),
                pl.BlockSpec((None, Pd, tq), lambda i, j, k: (j // nq, 0, j % nq)),
                (Pd, tq), (1, D // tq, T // tkt), TN)
    reduce_start("w_ple", g_ple)
    tmw = _tile(D, 1024)
    g_pg = _mm("d_w_ple_gate", [x2b, dgl],
               [pl.BlockSpec((tkt, tq), lambda i, j, k: (k, i)), pl.BlockSpec((tkt, tn), lambda i, j, k: (k, j))],
               jax.ShapeDtypeStruct((NCH, Dq, D), PARTIAL), pl.BlockSpec((None, tq, tn), lambda i, j, k: (i // nq, i % nq, j)),
               (tq, tn), (D // tq, D // tn, T // tkt), TN)
    reduce_start("w_ple_gate", g_pg)
    reduce_exchange("w_ple")
    dx2 = _mm("d_x2", [dgl, gw_pg],
              [pl.BlockSpec((tm, tk), lambda i, j, k: (i, k)), pl.BlockSpec((tn, tk), lambda i, j, k: (j, k))],
              jax.ShapeDtypeStruct((T, D), F32), pl.BlockSpec((tm, tn), lambda i, j, k: (i, j)),
              (tm, tn), (T // tm, D // tn, D // tk), NT,
              add=d3, add_spec=pl.BlockSpec((tm, tn), lambda i, j, k: (i, j)))
    reduce_exchange("w_ple_gate")
    dff, dw_pff = _bwd_ffn_norm(dx2, ff, post_ffn_w)
    g_down = _mm("d_w_down", [act, dff],
                 [pl.BlockSpec((None, tkt, Fs), lambda i, j, k: (i, k, 0)),
                  pl.BlockSpec((tkt, tnf), lambda i, j, k: (k, j))],
                 jax.ShapeDtypeStruct((NCH, Fs, D), PARTIAL), pl.BlockSpec((None, Fs, tnf), lambda i, j, k: (i, 0, j)),
                 (Fs, tnf), (NCH, D // tnf, T // tkt), TN)
    reduce_start("w_down", g_down)
    reduce_finish("w_ple")
    dgt, dup = _ffn_dact(dff, gw_down, gt, up)
    reduce_exchange("w_down")
    reduce_finish("w_ple_gate")
    reduce_update("w_ple")
    wgrad_specs = [pl.BlockSpec((None, tkt, Fs), lambda i, j, k: (i, k, 0)),
                   pl.BlockSpec((tkt, tnf), lambda i, j, k: (k, j))]
    wgrad_out = pl.BlockSpec((None, Fs, tnf), lambda i, j, k: (i, 0, j))
    g_gate = _mm("d_w_gate", [dgt, h2], wgrad_specs, jax.ShapeDtypeStruct((NCH, Fs, D), PARTIAL), wgrad_out,
                 (Fs, tnf), (NCH, D // tnf, T // tkt), TN)
    reduce_start("w_gate", g_gate)
    reduce_update("w_ple_gate")
    g_up = _mm("d_w_up", [dup, h2], wgrad_specs, jax.ShapeDtypeStruct((NCH, Fs, D), PARTIAL), wgrad_out,
               (Fs, tnf), (NCH, D // tnf, T // tkt), TN)
    reduce_start("w_up", g_up)
    reduce_exchange("w_gate")
    tmh = _tile(T, 1024)
    a_sp = pl.BlockSpec((None, tmh, Fs), lambda i, j, k: (k, i, 0))
    b_sp = pl.BlockSpec((None, Fs, tnf), lambda i, j, k: (k, 0, j))
    dh2 = _mm("d_h2", [dgt, gw_gate, dup, gw_up], [a_sp, b_sp, a_sp, b_sp],
              jax.ShapeDtypeStruct((T, D), F32), pl.BlockSpec((tmh, tnf), lambda i, j, k: (i, j)),
              (tmh, tnf), (T // tmh, D // tnf, NCH), NN)
    reduce_exchange("w_up")
    reduce_finish("w_down")
    dx1, dmix, dw_pf, dw_pm = _bwd_mid(dx2, dh2, x1, pre_ffn_w, mix, post_mix_w)
    reduce_update("w_down")
    tw = _tile(Wd, 1024)
    nw = Wd // tw
    nwq = Wd // tq
    g_out = _mm("d_w_out", [cat, dmix],
                [pl.BlockSpec((None, tkt, tq), lambda i, j, k: (i // nwq, k, i % nwq)),
                 pl.BlockSpec((tkt, tn), lambda i, j, k: (k, j))],
                jax.ShapeDtypeStruct((NCH, Dq, D), PARTIAL), pl.BlockSpec((None, tq, tn), lambda i, j, k: (i // nq, i % nq, j)),
                (tq, tn), (D // tq, D // tn, T // tkt), TN)
    reduce_start("w_out", g_out)
    reduce_finish("w_gate")
    dcat = _mm("d_cat", [dmix, gw_out],
               [pl.BlockSpec((tm, tk), lambda i, j, k: (i, k)), pl.BlockSpec((tw, tk), lambda i, j, k: (j, k))],
               jax.ShapeDtypeStruct((2, T, Wd), F32), pl.BlockSpec((None, tm, tw), lambda i, j, k: (j // nw, i, j % nw)),
               (tm, tw), (T // tm, D // tw, D // tk), NT)
    reduce_exchange("w_out")
    reduce_update("w_gate")
    dproj6, dw_an, dlb = _hgrn2_bwd(proj6, o_a, states, dcat, lb_param, a_norm_w)
    reduce_finish("w_up")
    dproj6, dw_lnw, dw_lnb, dw_sp, dw_bst = _gmlp_bwd(proj6, dproj6, dcat, gmlp_ln_w, gmlp_ln_b, w_spatial[0], b_st)
    reduce_update("w_up")
    g_in = _mm("d_w_in", [h1, dproj6],
               [pl.BlockSpec((tkt, tmw), lambda i, j, k: (k, i)),
                pl.BlockSpec((None, tkt, ts), lambda i, j, k: (j // n_sec, k, j % n_sec))],
               jax.ShapeDtypeStruct((NCH, D, Ci), PARTIAL),
               pl.BlockSpec((None, tmw, ts), lambda i, j, k: (j // n_shd, i, j % n_shd)),
               (tmw, ts), (D // tmw, 6 * n_sec, T // tkt), TN)
    reduce_start("w_in", g_in)
    reduce_finish("w_out")
    reduce_exchange("w_in")
    dh1 = _mm("d_h1", [dproj6, gw_in],
              [pl.BlockSpec((None, tm, ts), lambda i, j, k: (k // n_sec, i, k % n_sec)),
               pl.BlockSpec((None, tn, ts), lambda i, j, k: (k // n_shd, j, k % n_shd))],
              jax.ShapeDtypeStruct((T, D), F32), pl.BlockSpec((tm, tn), lambda i, j, k: (i, j)),
              (tm, tn), (T // tm, D // tn, 6 * n_sec), NT)
    reduce_update("w_out")
    grad_x, dw_pre = _bwd_in(dx1, dh1, xs, pre_mix_w)

    small = [("pre_mix_w", dw_pre), ("lb", dlb), ("a_norm_w", dw_an), ("gmlp_ln_w", dw_lnw), ("gmlp_ln_b", dw_lnb),
             ("w_spatial", dw_sp), ("b_spatial_t", dw_bst), ("post_mix_w", dw_pm), ("pre_ffn_w", dw_pf),
             ("post_ffn_w", dw_pff), ("post_ple_w", dw_pp), ("loss", loss_vec)]
    rows = [a.size // LANES for _, a in small]
    total = sum(rows)
    pad = (-total) % 8
    packed = jnp.concatenate([a.reshape(-1, LANES) for _, a in small] + [jnp.zeros((pad, LANES), F32)], axis=0)
    summed = _sum_all_devices(packed)
    off, piece = 0, {}
    for (nm, a), r in zip(small, rows):
        piece[nm] = summed[off:off + r].reshape(a.shape)
        off += r
    loss = jnp.sum(piece["loss"])
    g_small = {
        "pre_mix_w": piece["pre_mix_w"], "lb_param": _lb_grad(piece["lb"], lb_param), "a_norm_w": piece["a_norm_w"],
        "gmlp_ln_w": piece["gmlp_ln_w"], "gmlp_ln_b": piece["gmlp_ln_b"], "w_spatial": piece["w_spatial"][None],
        "b_spatial": piece["b_spatial_t"][:, :G].T[None], "post_mix_w": piece["post_mix_w"],
        "pre_ffn_w": piece["pre_ffn_w"], "post_ffn_w": piece["post_ffn_w"], "post_ple_w": piece["post_ple_w"],
    }
    w_small = dict(pre_mix_w=(pre_mix_w, m_pre_mix_w, v_pre_mix_w), lb_param=(lb_param, m_lb_param, v_lb_param),
                   a_norm_w=(a_norm_w, m_a_norm_w, v_a_norm_w), gmlp_ln_w=(gmlp_ln_w, m_gmlp_ln_w, v_gmlp_ln_w),
                   gmlp_ln_b=(gmlp_ln_b, m_gmlp_ln_b, v_gmlp_ln_b), w_spatial=(w_spatial, m_w_spatial, v_w_spatial),
                   b_spatial=(b_spatial, m_b_spatial, v_b_spatial), post_mix_w=(post_mix_w, m_post_mix_w, v_post_mix_w),
                   pre_ffn_w=(pre_ffn_w, m_pre_ffn_w, v_pre_ffn_w), post_ffn_w=(post_ffn_w, m_post_ffn_w, v_post_ffn_w),
                   post_ple_w=(post_ple_w, m_post_ple_w, v_post_ple_w))
    names_small = list(w_small)

    def pack(arrs):
        flat = jnp.concatenate([a.reshape(-1, LANES) for a in arrs], axis=0)
        return jnp.pad(flat, ((0, (-flat.shape[0]) % 16), (0, 0)))

    pk = [pack([w_small[n][i] for n in names_small]) for i in range(3)]
    pg = pack([g_small[n] for n in names_small])
    _, sd, sm, sv = _adamw("adamw_small", pk[0], pg, pk[1], pk[2])
    off = 0
    for n in names_small:
        shp = w_small[n][0].shape
        r = w_small[n][0].size // LANES
        res[n] = (g_small[n].reshape(shp), sd[off:off + r].reshape(shp), sm[off:off + r].reshape(shp),
                  sv[off:off + r].reshape(shp))
        off += r

    reduce_finish("w_in")
    reduce_update("w_in")

    order =["pre_mix_w", "w_in", "lb_param", "a_norm_w", "gmlp_ln_w", "gmlp_ln_b", "w_spatial", "b_spatial", "w_out",
             "post_mix_w", "pre_ffn_w", "w_gate", "w_up", "w_down", "post_ffn_w", "w_ple", "w_ple_gate", "post_ple_w"]
    return (loss, grad_x[None], *[res[n][0] for n in order], *[res[n][1] for n in order],
            *[res[n][2] for n in order], *[res[n][3] for n in order])
```

```python
import functools
import math

import jax
import jax.numpy as jnp
from jax import lax
from jax.experimental import pallas as pl
from jax.experimental.pallas import tpu as pltpu

F32 = jnp.float32
BF16 = jnp.bfloat16
PARTIAL = jnp.bfloat16
EPS = 1e-6
HEAD = 128
GLA_CHUNK = 64
HEADS_PER_STEP = 4
GMLP_CHUNK = 128
LANES = 128
VMEM_LIMIT = 48 * 1024 * 1024
FFN_UP_VMEM_LIMIT = 58 * 1024 * 1024
ADAM_LR, ADAM_B1, ADAM_B2, ADAM_EPS, ADAM_WD, ADAM_STEP = 0.001, 0.9, 0.999, 1e-08, 0.01, 10
MESH = pl.DeviceIdType.MESH
HBM_SPEC = pl.BlockSpec(memory_space=pltpu.HBM)
VMEM_SPEC = pl.BlockSpec(memory_space=pltpu.VMEM)
SEM_SPEC = pl.BlockSpec(memory_space=pltpu.SEMAPHORE)
SIDE_EFFECT = pltpu.SideEffectType.DATAFLOW_SIDE_EFFECTING

NN = ((1,), (0,))
NT = ((1,), (1,))
TN = ((0,), (0,))


def _tile(n, pref, mult=LANES):
    if n <= pref:
        return n
    t = (pref // mult) * mult
    while t >= mult:
        if n % t == 0:
            return t
        t -= mult
    return n


def _params(sem, vmem_limit=VMEM_LIMIT):
    return pltpu.CompilerParams(dimension_semantics=sem, vmem_limit_bytes=vmem_limit)


class _Order:
    last = None


def _pcall(body, args, *, token=0, prefetch=0, grid=(), in_specs, out_specs, scratch_shapes=(), same=(), **kw):
    args, in_specs = list(args), list(in_specs)
    n = len(args)
    run = body
    if _Order.last is not None and not any(_Order.last is a for a in [*args, *same]):
        def run(*refs):
            return body(*refs[:n], *refs[n + 1:])
        args.append(_Order.last)
        in_specs.append(pl.BlockSpec(memory_space=pl.ANY))
    if prefetch:
        kw["grid_spec"] = pltpu.PrefetchScalarGridSpec(num_scalar_prefetch=prefetch, grid=grid, in_specs=in_specs,
                                                       out_specs=out_specs, scratch_shapes=list(scratch_shapes))
    else:
        kw.update(grid=grid, in_specs=in_specs, out_specs=out_specs, scratch_shapes=list(scratch_shapes))
    outs = pl.pallas_call(run, **kw)(*args)
    _Order.last = outs[token] if isinstance(outs, (tuple, list)) else outs
    return outs


def _dot(a, b, dims, precision=None):
    return lax.dot_general(a, b, (dims, ((), ())), preferred_element_type=F32, precision=precision)


def _bdot(a, b, dims):
    return _dot(a.astype(BF16), b.astype(BF16), dims)


def _split_bf16(v):
    hi = v.astype(BF16)
    return hi, (v - hi.astype(F32)).astype(BF16)


def _fdot(a, b, dims):
    a_hi, a_lo = _split_bf16(a)
    b_hi, b_lo = _split_bf16(b)
    return _dot(a_hi, b_hi, dims) + (_dot(a_hi, b_lo, dims) + _dot(a_lo, b_hi, dims))


def _mask_dot(mask, v):
    m = mask.astype(BF16)
    hi, lo = _split_bf16(v)
    lo2 = (v - hi.astype(F32) - lo.astype(F32)).astype(BF16)
    return _dot(m, hi, NN) + (_dot(m, lo, NN) + _dot(m, lo2, NN))


def _sigmoid(v):
    return 1.0 / (1.0 + jnp.exp(-v))


def _mm(name, operands, specs, out_shape, out_spec, acc_shape, grid, dims, add=None, add_spec=None):
    n_pairs = len(operands) // 2
    nk = grid[2]

    def body(*refs):
        ins = refs[:2 * n_pairs]
        r_ref = refs[2 * n_pairs] if add is not None else None
        o_ref, acc = refs[-2], refs[-1]
        k = pl.program_id(2)

        @pl.when(k == 0)
        def _():
            acc[...] = jnp.zeros_like(acc)

        part = _bdot(ins[0][...], ins[1][...], dims)
        for q in range(1, n_pairs):
            part = part + _bdot(ins[2 * q][...], ins[2 * q + 1][...], dims)
        acc[...] += part

        @pl.when(k == nk - 1)
        def _():
            r = acc[...]
            if r_ref is not None:
                r = r + r_ref[...].astype(F32)
            o_ref[...] = r.astype(o_ref.dtype)

    in_specs = list(specs) + ([add_spec] if add is not None else [])
    args = list(operands) + ([add] if add is not None else [])
    return _pcall(
        body, args, name=name, grid=grid, in_specs=in_specs, out_specs=out_spec, out_shape=out_shape,
        scratch_shapes=[pltpu.VMEM(acc_shape, F32)],
        compiler_params=_params(("parallel", "parallel", "arbitrary")),
    )


def _rms(v, w):
    r = lax.rsqrt(jnp.mean(v * v, axis=-1, keepdims=True) + EPS)
    return v * r * w


def _rms_bwd(v, w, dy):
    r = lax.rsqrt(jnp.mean(v * v, axis=-1, keepdims=True) + EPS)
    vh = v * r
    gy = dy * w
    dv = r * (gy - vh * jnp.mean(gy * vh, axis=-1, keepdims=True))
    return dv, jnp.sum(dy * vh, axis=0, keepdims=True)


def _rowwise(name, body, ins, in_kinds, outs, out_kinds, T, D):
    tr = _tile(T, 128, 16)
    row = pl.BlockSpec((tr, D), lambda i: (i, 0))
    vec = pl.BlockSpec((1, D), lambda i: (0, 0))
    pick = {"row": row, "vec": vec}
    return _pcall(
        body, ins, name=name, grid=(T // tr,),
        in_specs=[pick[k] for k in in_kinds], out_specs=[pick[k] for k in out_kinds], out_shape=outs,
        compiler_params=_params(("arbitrary",)),
    )


def _acc_vec(ref, val):
    @pl.when(pl.program_id(0) == 0)
    def _():
        ref[...] = jnp.zeros_like(ref)
    ref[...] += val


def _norm_in(x, w):
    T, D = x.shape

    def body(x_ref, w_ref, h_ref):
        h_ref[...] = _rms(x_ref[...], w_ref[...]).astype(BF16)

    return _rowwise("norm_in", body, [x, w], ["row", "vec"], [jax.ShapeDtypeStruct((T, D), BF16)], ["row"], T, D)[0]


def _mid1(x, mix, w_pm, w_pf):
    T, D = x.shape

    def body(x_ref, mix_ref, wpm_ref, wpf_ref, x1_ref, h2_ref):
        x1 = x_ref[...] + _rms(mix_ref[...], wpm_ref[...])
        x1_ref[...] = x1
        h2_ref[...] = _rms(x1, wpf_ref[...]).astype(BF16)

    return _rowwise("mid1", body, [x, mix, w_pm, w_pf], ["row", "row", "vec", "vec"],
                    [jax.ShapeDtypeStruct((T, D), F32), jax.ShapeDtypeStruct((T, D), BF16)], ["row", "row"], T, D)


def _mid2(x1, ff, w_pff):
    T, D = x1.shape

    def body(x1_ref, ff_ref, w_ref, x2_ref, x2b_ref):
        x2 = x1_ref[...] + _rms(ff_ref[...], w_ref[...])
        x2_ref[...] = x2
        x2b_ref[...] = x2.astype(BF16)

    return _rowwise("mid2", body, [x1, ff, w_pff], ["row", "row", "vec"],
                    [jax.ShapeDtypeStruct((T, D), F32), jax.ShapeDtypeStruct((T, D), BF16)], ["row", "row"], T, D)


def _head(x2, gl, e, w_pp, tgt):
    T, D = x2.shape

    def body(x2_ref, gl_ref, e_ref, w_ref, t_ref, d3_ref, de_ref, dgl_ref, dw_ref, loss_ref):
        ev = e_ref[...]
        gate = _sigmoid(gl_ref[...])
        pe = ev * gate
        w = w_ref[...]
        err = x2_ref[...] + _rms(pe, w) - t_ref[...]
        d3 = err * (1.0 / D)
        d3_ref[...] = d3
        dpe, dw = _rms_bwd(pe, w, d3)
        de_ref[...] = (dpe * gate).astype(BF16)
        dgl_ref[...] = (dpe * ev * gate * (1.0 - gate)).astype(BF16)
        _acc_vec(dw_ref, dw)
        _acc_vec(loss_ref, jnp.sum(err * err, axis=0, keepdims=True) * (0.5 / D))

    return _rowwise("head", body, [x2, gl, e, w_pp, tgt], ["row", "row", "row", "vec", "row"],
                    [jax.ShapeDtypeStruct((T, D), F32), jax.ShapeDtypeStruct((T, D), BF16),
                     jax.ShapeDtypeStruct((T, D), BF16), jax.ShapeDtypeStruct((1, D), F32),
                     jax.ShapeDtypeStruct((1, D), F32)], ["row", "row", "row", "vec", "vec"], T, D)


def _bwd_ffn_norm(dx2, ff, w_pff):
    T, D = dx2.shape

    def body(d_ref, ff_ref, w_ref, dff_ref, dw_ref):
        dff, dw = _rms_bwd(ff_ref[...], w_ref[...], d_ref[...])
        dff_ref[...] = dff.astype(BF16)
        _acc_vec(dw_ref, dw)

    return _rowwise("bwd_ffn_norm", body, [dx2, ff, w_pff], ["row", "row", "vec"],
                    [jax.ShapeDtypeStruct((T, D), BF16), jax.ShapeDtypeStruct((1, D), F32)], ["row", "vec"], T, D)


def _bwd_mid(dx2, dh2, x1, w_pf, mix, w_pm):
    T, D = dx2.shape

    def body(dx2_ref, dh2_ref, x1_ref, wpf_ref, mix_ref, wpm_ref, dx1_ref, dmix_ref, dwpf_ref, dwpm_ref):
        d1, dwpf = _rms_bwd(x1_ref[...], wpf_ref[...], dh2_ref[...])
        dx1 = dx2_ref[...] + d1
        dx1_ref[...] = dx1
        dmix, dwpm = _rms_bwd(mix_ref[...], wpm_ref[...], dx1)
        dmix_ref[...] = dmix.astype(BF16)
        _acc_vec(dwpf_ref, dwpf)
        _acc_vec(dwpm_ref, dwpm)

    return _rowwise("bwd_mid", body, [dx2, dh2, x1, w_pf, mix, w_pm], ["row", "row", "row", "vec", "row", "vec"],
                    [jax.ShapeDtypeStruct((T, D), F32), jax.ShapeDtypeStruct((T, D), BF16),
                     jax.ShapeDtypeStruct((1, D), F32), jax.ShapeDtypeStruct((1, D), F32)],
                    ["row", "row", "vec", "vec"], T, D)


def _bwd_in(dx1, dh1, x, w_pre):
    T, D = dx1.shape

    def body(dx1_ref, dh1_ref, x_ref, w_ref, gx_ref, dw_ref):
        d0, dw = _rms_bwd(x_ref[...], w_ref[...], dh1_ref[...])
        gx_ref[...] = dx1_ref[...] + d0
        _acc_vec(dw_ref, dw)

    return _rowwise("bwd_in", body, [dx1, dh1, x, w_pre], ["row", "row", "row", "vec"],
                    [jax.ShapeDtypeStruct((T, D), F32), jax.ShapeDtypeStruct((1, D), F32)], ["row", "vec"], T, D)


def _ffn_up(h2, wg, wu):
    T, D = h2.shape
    S, Fs, _ = wg.shape
    tm, tk = _tile(T, 512), _tile(D, 1024)
    nk = D // tk
    te = _tile(tm, 128, 16)

    def body(h_ref, wg_ref, wu_ref, gt_ref, up_ref, act_ref, accg, accu):
        k = pl.program_id(2)

        @pl.when(k == 0)
        def _():
            accg[...] = jnp.zeros_like(accg)
            accu[...] = jnp.zeros_like(accu)

        h = h_ref[...]
        accg[...] += _dot(h, wg_ref[...], NT)
        accu[...] += _dot(h, wu_ref[...], NT)

        @pl.when(k == nk - 1)
        def _():
            for r in range(0, tm, te):
                rows = slice(r, r + te)
                g, u = accg[rows, :], accu[rows, :]
                gt_ref[rows, :] = g.astype(BF16)
                up_ref[rows, :] = u.astype(BF16)
                act_ref[rows, :] = (g * _sigmoid(g) * u).astype(BF16)

    w_spec = pl.BlockSpec((None, Fs, tk), lambda i, j, k: (j, 0, k))
    o_spec = pl.BlockSpec((None, tm, Fs), lambda i, j, k: (j, i, 0))
    return _pcall(
        body, [h2, wg, wu], name="ffn_up", grid=(T // tm, S, nk),
        in_specs=[pl.BlockSpec((tm, tk), lambda i, j, k: (i, k)), w_spec, w_spec],
        out_specs=[o_spec, o_spec, o_spec],
        out_shape=[jax.ShapeDtypeStruct((S, T, Fs), BF16)] * 3,
        scratch_shapes=[pltpu.VMEM((tm, Fs), F32), pltpu.VMEM((tm, Fs), F32)],
        compiler_params=_params(("parallel", "parallel", "arbitrary"), FFN_UP_VMEM_LIMIT),
    )


def _ffn_dact(dff, wd, gt, up):
    T, D = dff.shape
    S, Fs, _ = wd.shape
    tm, tk = _tile(T, 512), _tile(D, 1024)
    nk = D // tk
    te = _tile(tm, 128, 16)

    def body(d_ref, w_ref, gt_ref, up_ref, dgt_ref, dup_ref, acc):
        k = pl.program_id(2)

        @pl.when(k == 0)
        def _():
            acc[...] = jnp.zeros_like(acc)

        acc[...] += _dot(d_ref[...], w_ref[...], NT)

        @pl.when(k == nk - 1)
        def _():
            for r in range(0, tm, te):
                rows = slice(r, r + te)
                da, g, u = acc[rows, :], gt_ref[rows, :].astype(F32), up_ref[rows, :].astype(F32)
                sg = _sigmoid(g)
                dgt_ref[rows, :] = (da * u * sg * (1.0 + g * (1.0 - sg))).astype(BF16)
                dup_ref[rows, :] = (da * g * sg).astype(BF16)

    o_spec = pl.BlockSpec((None, tm, Fs), lambda i, j, k: (j, i, 0))
    return _pcall(
        body, [dff, wd, gt, up], name="ffn_dact", grid=(T // tm, S, nk),
        in_specs=[pl.BlockSpec((tm, tk), lambda i, j, k: (i, k)),
                  pl.BlockSpec((None, Fs, tk), lambda i, j, k: (j, 0, k)), o_spec, o_spec],
        out_specs=[o_spec, o_spec],
        out_shape=[jax.ShapeDtypeStruct((S, T, Fs), BF16), jax.ShapeDtypeStruct((S, T, Fs), BF16)],
        scratch_shapes=[pltpu.VMEM((tm, Fs), F32)],
        compiler_params=_params(("parallel", "parallel", "arbitrary")),
    )


def _hgrn2_gates(q, f, lb):
    C = GLA_CHUNK
    sq = _sigmoid(q)
    qs = q * sq
    sf = _sigmoid(f)
    fg = lb + (1.0 - lb) * sf
    lf = jnp.log(jnp.maximum(fg, 1e-30))
    row = lax.broadcasted_iota(jnp.int32, (C, C), 0)
    col = lax.broadcasted_iota(jnp.int32, (C, C), 1)
    tril = row >= col
    b = _mask_dot(jnp.where(tril, 1.0, 0.0), lf)
    bmid = b[C // 2 - 1:C // 2, :]
    bend = b[C - 1:C, :]
    ea = jnp.exp(b - bmid)
    ena = jnp.exp(bmid - b)
    eb = ea * jnp.exp(bmid)
    ee = ena * jnp.exp(bend - bmid)
    return sq, qs, sf, fg, tril, ea, ena, eb, ee, jnp.exp(bend)


def _lower_bound(lbp):
    return _sigmoid(lbp[0:1, :] - lbp[1:2, :])


def _hgrn2_fwd(proj6, lb_param, a_norm_w):
    _, T, Wd = proj6.shape
    H, C = Wd // HEAD, GLA_CHUNK
    TB = _tile(T, 512, C)
    NB, NCB = T // TB, TB // C

    HS, W2 = HEADS_PER_STEP, HEADS_PER_STEP * HEAD

    def body(p_ref, lbp_ref, w_ref, cat_ref, o_ref, st_ref, S):
        @pl.when(pl.program_id(1) == 0)
        def _():
            S[...] = jnp.zeros_like(S)

        lanes = [slice(hh * HEAD, (hh + 1) * HEAD) for hh in range(HS)]
        lbs = [_lower_bound(lbp_ref[:, ls]) for ls in lanes]
        ws = [w_ref[:, ls] for ls in lanes]

        def chunk(c, carry):
            rows = pl.ds(pl.multiple_of(c * C, C), C)
            loaded = [([p_ref[s, rows, ls] for s in range(4)], S[hh]) for hh, ls in enumerate(lanes)]
            done = []
            for hh, ((q, f, v, g), st) in enumerate(loaded):
                _, qs, _, fg, tril, ea, ena, eb, ee, ebend = _hgrn2_gates(q, f, lbs[hh])
                k = 1.0 - fg
                p = jnp.where(tril, _bdot(qs * ea, k * ena, NT), 0.0)
                o = _bdot(qs * eb, st, NT) + _bdot(p, v, NN)
                r = lax.rsqrt(jnp.mean(o * o, axis=-1, keepdims=True) + EPS)
                done.append((st, st * ebend + _bdot(v, k * ee, TN), o,
                             (o * r * ws[hh] * (g * _sigmoid(g))).astype(BF16)))
            for hh, ls in enumerate(lanes):
                st, st_new, o, a_out = done[hh]
                st_ref[hh, c] = st
                S[hh] = st_new
                o_ref[rows, ls] = o
                cat_ref[rows, ls] = a_out
            return carry

        lax.fori_loop(0, NCB, chunk, 0)

    return _pcall(
        body, [proj6, lb_param, a_norm_w], name="hgrn2_fwd", grid=(H // HS, NB),
        in_specs=[pl.BlockSpec((4, TB, W2), lambda h, t: (0, t, h)),
                  pl.BlockSpec((2, W2), lambda h, t: (0, h)),
                  pl.BlockSpec((1, W2), lambda h, t: (0, h))],
        out_specs=[pl.BlockSpec((None, TB, W2), lambda h, t: (0, t, h)),
                   pl.BlockSpec((TB, W2), lambda h, t: (t, h)),
                   pl.BlockSpec((HS, NCB, HEAD, HEAD), lambda h, t: (h, t, 0, 0))],
        out_shape=[jax.ShapeDtypeStruct((2, T, Wd), BF16), jax.ShapeDtypeStruct((T, Wd), F32),
                   jax.ShapeDtypeStruct((H, T // C, HEAD, HEAD), F32)],
        scratch_shapes=[pltpu.VMEM((HS, HEAD, HEAD), F32)],
        compiler_params=_params(("parallel", "arbitrary")),
    )


def _hgrn2_bwd(proj6, o, states, dcat, lb_param, a_norm_w):
    _, T, Wd = proj6.shape
    H, C = Wd // HEAD, GLA_CHUNK
    TB = _tile(T, 512, C)
    NB, NCB = T // TB, TB // C
    HS, W2 = HEADS_PER_STEP, HEADS_PER_STEP * HEAD

    def body(p_ref, o_ref, st_ref, da_ref, lbp_ref, w_ref, dp_ref, dw_ref, dlb_ref, dS):
        @pl.when(pl.program_id(1) == 0)
        def _():
            dS[...] = jnp.zeros_like(dS)
            dw_ref[...] = jnp.zeros_like(dw_ref)
            dlb_ref[...] = jnp.zeros_like(dlb_ref)

        lanes = [slice(hh * HEAD, (hh + 1) * HEAD) for hh in range(HS)]
        lbs = [_lower_bound(lbp_ref[:, ls]) for ls in lanes]
        ws = [w_ref[:, ls] for ls in lanes]

        def chunk(i, carry):
            c = NCB - 1 - i
            rows = pl.ds(pl.multiple_of(c * C, C), C)
            loaded = [([p_ref[s, rows, ls] for s in range(4)], st_ref[hh, c], dS[hh], o_ref[rows, ls], da_ref[rows, ls],
                       dw_ref[:, ls], dlb_ref[:, ls]) for hh, ls in enumerate(lanes)]
            done = []
            for hh, ((q, f, v, g), st, dst, ov, da, dw_old, dlb_old) in enumerate(loaded):
                lb, w = lbs[hh], ws[hh]
                sq, qs, sf, fg, tril, ea, ena, eb, ee, ebend = _hgrn2_gates(q, f, lb)
                k = 1.0 - fg
                qt, kt, qb, ke = qs * ea, k * ena, qs * eb, k * ee
                p = jnp.where(tril, _bdot(qt, kt, NT), 0.0)
                r = lax.rsqrt(jnp.mean(ov * ov, axis=-1, keepdims=True) + EPS)
                oh = ov * r
                sg = _sigmoid(g)
                gs = g * sg
                dw_new = dw_old + jnp.sum(da * oh * gs, axis=0, keepdims=True)
                dg = da * oh * w * sg * (1.0 + g * (1.0 - sg))
                doh = da * w * gs
                do = r * (doh - oh * jnp.mean(doh * oh, axis=-1, keepdims=True))
                dqb = _fdot(do, st, NN)
                dp = jnp.where(tril, _bdot(do, v, NT), 0.0)
                dv = _bdot(p, do, TN) + _bdot(ke, dst, NT)
                dqt = _fdot(dp, kt, NN)
                dkt = _fdot(dp, qt, TN)
                dke = _fdot(v, dst, NN)
                dqs = dqt * ea + dqb * eb
                dk = dkt * ena + dke * ee
                db = dqt * qt + dqb * qb - dkt * kt - dke * ke
                db_end = (jnp.sum(dke * ke, axis=0, keepdims=True)
                          + ebend * jnp.sum(dst * st, axis=0, keepdims=True))
                triu = lax.broadcasted_iota(jnp.int32, (C, C), 0) <= lax.broadcasted_iota(jnp.int32, (C, C), 1)
                dlf = _mask_dot(jnp.where(triu, 1.0, 0.0), db) + db_end
                dfg = jnp.where(fg > 1e-30, dlf / fg, 0.0) - dk
                done.append((dw_new, dlb_old + jnp.sum(dfg * (1.0 - sf), axis=0, keepdims=True),
                             [(dqs * sq * (1.0 + q * (1.0 - sq))).astype(BF16),
                              (dfg * (1.0 - lb) * sf * (1.0 - sf)).astype(BF16), dv.astype(BF16), dg.astype(BF16)],
                             dst * ebend + _bdot(do, qb, TN)))
            for hh, ls in enumerate(lanes):
                dw_new, dlb_new, dsec, dst_new = done[hh]
                dw_ref[:, ls] = dw_new
                dlb_ref[:, ls] = dlb_new
                for s in range(4):
                    dp_ref[s, rows, ls] = dsec[s]
                dS[hh] = dst_new
            return carry

        lax.fori_loop(0, NCB, chunk, 0)

    rev = lambda t: NB - 1 - t
    return _pcall(
        body, [proj6, o, states, dcat, lb_param, a_norm_w], name="hgrn2_bwd", grid=(H // HS, NB),
        in_specs=[pl.BlockSpec((4, TB, W2), lambda h, t: (0, rev(t), h)),
                  pl.BlockSpec((TB, W2), lambda h, t: (rev(t), h)),
                  pl.BlockSpec((HS, NCB, HEAD, HEAD), lambda h, t: (h, rev(t), 0, 0)),
                  pl.BlockSpec((None, TB, W2), lambda h, t: (0, rev(t), h)),
                  pl.BlockSpec((2, W2), lambda h, t: (0, h)),
                  pl.BlockSpec((1, W2), lambda h, t: (0, h))],
        out_specs=[pl.BlockSpec((4, TB, W2), lambda h, t: (0, rev(t), h)),
                   pl.BlockSpec((1, W2), lambda h, t: (0, h)),
                   pl.BlockSpec((1, W2), lambda h, t: (0, h))],
        out_shape=[jax.ShapeDtypeStruct((6, T, Wd), BF16), jax.ShapeDtypeStruct((1, Wd), F32),
                   jax.ShapeDtypeStruct((1, Wd), F32)],
        scratch_shapes=[pltpu.VMEM((HS, HEAD, HEAD), F32)],
        compiler_params=_params(("parallel", "arbitrary")),
    )


_INV_SQRT2 = 1.0 / math.sqrt(2.0)
_INV_SQRT2PI = 1.0 / math.sqrt(2.0 * math.pi)


def _gelu(v):
    return 0.5 * v * (1.0 + lax.erf(v * _INV_SQRT2))


def _gelu_grad(v):
    return 0.5 * (1.0 + lax.erf(v * _INV_SQRT2)) + v * jnp.exp(-0.5 * v * v) * _INV_SQRT2PI


def _gmlp_norm(v, ln_w, ln_b):
    vf = _gelu(v)
    mu = jnp.mean(vf, axis=-1, keepdims=True)
    cen = vf - mu
    rstd = lax.rsqrt(jnp.mean(cen * cen, axis=-1, keepdims=True) + EPS)
    vh = cen * rstd
    return vh, rstd, vh * ln_w + ln_b


def _tril_mask():
    n = GMLP_CHUNK
    return lax.broadcasted_iota(jnp.int32, (n, n), 0) >= lax.broadcasted_iota(jnp.int32, (n, n), 1)


def _gmlp_fwd(proj6, cat, ln_w, ln_b, w_s, b_st):
    _, T, Wd = proj6.shape
    G, N = Wd // HEAD, GMLP_CHUNK

    def body(p_ref, cat_in, lnw_ref, lnb_ref, ws_ref, bst_ref, out_ref):
        del cat_in
        _, _, vn = _gmlp_norm(p_ref[1], lnw_ref[...], lnb_ref[...])
        tril = _tril_mask()
        for g in range(G):
            cols = slice(g * HEAD, (g + 1) * HEAD)
            z = _bdot(jnp.where(tril, ws_ref[g], 0.0), vn[:, cols], NN) + bst_ref[:, g:g + 1]
            out_ref[:, cols] = (_gelu(p_ref[0, :, cols]) * z).astype(BF16)

    return _pcall(
        body, [proj6, cat, ln_w, ln_b, w_s, b_st], name="gmlp_fwd", grid=(T // N,),
        in_specs=[pl.BlockSpec((2, N, Wd), lambda t: (2, t, 0)), pl.BlockSpec(memory_space=pl.ANY),
                  pl.BlockSpec((1, Wd), lambda t: (0, 0)), pl.BlockSpec((1, Wd), lambda t: (0, 0)),
                  pl.BlockSpec((G, N, N), lambda t: (0, 0, 0)), pl.BlockSpec((N, LANES), lambda t: (0, 0))],
        out_specs=pl.BlockSpec((None, N, Wd), lambda t: (1, t, 0)),
        out_shape=jax.ShapeDtypeStruct(cat.shape, cat.dtype),
        input_output_aliases={1: 0},
        compiler_params=_params(("arbitrary",)),
    )


def _gmlp_bwd(proj6, dproj6, dcat, ln_w, ln_b, w_s, b_st):
    _, T, Wd = proj6.shape
    G, N = Wd // HEAD, GMLP_CHUNK

    def body(p_ref, dp_in, db_ref, lnw_ref, lnb_ref, ws_ref, bst_ref, dp_ref, dlnw_ref, dlnb_ref, dws_ref, dbst_ref, dvn):
        del dp_in

        @pl.when(pl.program_id(0) == 0)
        def _():
            dlnw_ref[...] = jnp.zeros_like(dlnw_ref)
            dlnb_ref[...] = jnp.zeros_like(dlnb_ref)
            dws_ref[...] = jnp.zeros_like(dws_ref)
            dbst_ref[...] = jnp.zeros_like(dbst_ref)

        lnw = lnw_ref[...]
        v = p_ref[1]
        vh, rstd, vn = _gmlp_norm(v, lnw, lnb_ref[...])
        tril = _tril_mask()
        lane = lax.broadcasted_iota(jnp.int32, (N, LANES), 1)
        dbst = jnp.zeros((N, LANES), F32)
        for g in range(G):
            cols = slice(g * HEAD, (g + 1) * HEAD)
            wt = jnp.where(tril, ws_ref[g], 0.0)
            vn_g = vn[:, cols]
            z = _bdot(wt, vn_g, NN) + bst_ref[:, g:g + 1]
            u = p_ref[0, :, cols]
            db = db_ref[:, cols]
            dp_ref[0, :, cols] = (db * z * _gelu_grad(u)).astype(BF16)
            dz = db * _gelu(u)
            dbst = dbst + jnp.where(lane == g, jnp.sum(dz, axis=1, keepdims=True), 0.0)
            dws_ref[g] += jnp.where(tril, _bdot(dz, vn_g, NT), 0.0)
            dvn[:, cols] = _bdot(wt, dz, TN)
        dbst_ref[...] += dbst
        dv = dvn[...]
        dlnw_ref[...] += jnp.sum(dv * vh, axis=0, keepdims=True)
        dlnb_ref[...] += jnp.sum(dv, axis=0, keepdims=True)
        dvh = dv * lnw
        dvf = rstd * (dvh - jnp.mean(dvh, axis=-1, keepdims=True) - vh * jnp.mean(dvh * vh, axis=-1, keepdims=True))
        dp_ref[1] = (dvf * _gelu_grad(v)).astype(BF16)

    vec = pl.BlockSpec((1, Wd), lambda t: (0, 0))
    return _pcall(
        body, [proj6, dproj6, dcat, ln_w, ln_b, w_s, b_st], name="gmlp_bwd", grid=(T // N,),
        in_specs=[pl.BlockSpec((2, N, Wd), lambda t: (2, t, 0)), pl.BlockSpec(memory_space=pl.ANY),
                  pl.BlockSpec((None, N, Wd), lambda t: (1, t, 0)), vec, vec,
                  pl.BlockSpec((G, N, N), lambda t: (0, 0, 0)), pl.BlockSpec((N, LANES), lambda t: (0, 0))],
        out_specs=[pl.BlockSpec((2, N, Wd), lambda t: (2, t, 0)), vec, vec,
                   pl.BlockSpec((G, N, N), lambda t: (0, 0, 0)), pl.BlockSpec((N, LANES), lambda t: (0, 0))],
        out_shape=[jax.ShapeDtypeStruct(dproj6.shape, dproj6.dtype), jax.ShapeDtypeStruct((1, Wd), F32),
                   jax.ShapeDtypeStruct((1, Wd), F32), jax.ShapeDtypeStruct((G, N, N), F32),
                   jax.ShapeDtypeStruct((N, LANES), F32)],
        scratch_shapes=[pltpu.VMEM((N, Wd), F32)],
        input_output_aliases={1: 0},
        compiler_params=_params(("arbitrary",)),
    )


def _place():
    return lax.axis_index("x"), lax.axis_index("y"), lax.axis_index("c")


def _other_chips(x, y):
    return [(1 - x, y), (x, 1 - y), (1 - x, 1 - y)]


def _rcopy(src, dst, ssem, rsem, k, to):
    return pltpu.make_async_remote_copy(src_ref=src, dst_ref=dst, send_sem=ssem.at[k], recv_sem=rsem.at[k],
                                        device_id=to, device_id_type=MESH)


def _cast_slot(name, w, place):
    _, R, C = w.shape
    tr = _row_block(R, C)

    def body(p_ref, w_ref, o_ref):
        del p_ref
        o_ref[...] = w_ref[...].astype(BF16)

    return _pcall(
        body, [place[0], w], name=name, prefetch=1, grid=(R // tr,),
        in_specs=[pl.BlockSpec((None, tr, C), lambda i, p: (0, i, 0))],
        out_specs=pl.BlockSpec((None, tr, C), lambda i, p: (p[0], i, 0)),
        out_shape=jax.ShapeDtypeStruct((4, R, C), BF16),
        compiler_params=_params(("parallel",)))


def _half(ref, px, py, pc, Rh):
    return ref.at[2 * px + py, pl.ds(pc * Rh, Rh), :]


def _split_start(name, bufs, n_sem, copies):
    nb = len(bufs)

    def body(*refs):
        for cp in copies(refs[:nb], refs[nb], refs[nb + 1], False):
            cp.start()
        refs[-1][...] = jnp.zeros_like(refs[-1])

    outs = _pcall(
        body, [pltpu.with_memory_space_constraint(b, pltpu.HBM) for b in bufs], same=bufs, token=-1, name=name,
        in_specs=[HBM_SPEC] * nb, out_specs=(SEM_SPEC, SEM_SPEC, *[HBM_SPEC] * nb, VMEM_SPEC),
        out_shape=(pltpu.SemaphoreType.DMA((n_sem,)), pltpu.SemaphoreType.DMA((n_sem,)),
                   *[pltpu.HBM(b.shape, b.dtype) for b in bufs], jax.ShapeDtypeStruct((8, LANES), F32)),
        input_output_aliases={i: 2 + i for i in range(nb)},
        compiler_params=pltpu.CompilerParams(has_side_effects=SIDE_EFFECT))
    return outs[:-1]


def _split_wait(name, started, copies):
    ssem, rsem, *bufs = started
    nb = len(bufs)

    def body(*refs):
        for cp in copies(refs[:nb], refs[nb], refs[nb + 1], True):
            cp.wait_send()
            cp.wait_recv()

    outs = _pcall(
        body, [*bufs, ssem, rsem], name=name, in_specs=[HBM_SPEC] * nb + [SEM_SPEC, SEM_SPEC],
        out_specs=tuple([HBM_SPEC] * nb), out_shape=tuple(pltpu.HBM(b.shape, b.dtype) for b in bufs),
        input_output_aliases={i: i for i in range(nb)},
        compiler_params=pltpu.CompilerParams(has_side_effects=SIDE_EFFECT))
    return outs


def _gather_copies(Rh):
    def copies(refs, ssem, rsem, waiting):
        x, y, c = _place()
        mine = _half(refs[0], x, y, c, Rh)
        return [_rcopy(mine, _half(refs[0], px, py, c, Rh) if waiting else mine, ssem, rsem, j, (px, py, c))
                for j, (px, py) in enumerate(_other_chips(x, y))]
    return copies


def _swap_copies(Rh):
    def copies(refs, ssem, rsem, waiting):
        x, y, c = _place()
        return [_rcopy(refs[0].at[:, pl.ds((1 - c) * Rh, Rh), :], refs[1], ssem, rsem, 0, (x, y, 1 - c))]
    return copies


def _exchange_copies(refs, ssem, rsem, waiting):
    x, y, c = _place()
    return [_rcopy(refs[0].at[2 * px + py], refs[1].at[j], ssem, rsem, j, (px, py, c))
            for j, (px, py) in enumerate(_other_chips(x, y))]


def _landing(shape, dtype):
    return lax.empty(shape, dtype)


def _forward_copies(Rh):
    def copies(refs, ssem, rsem, waiting):
        x, y, c = _place()
        return [_rcopy(_half(refs[0], px, py, c, Rh), _half(refs[0], px, py, 1 - c if waiting else c, Rh),
                       ssem, rsem, j, (x, y, 1 - c)) for j, (px, py) in enumerate(_other_chips(x, y))]
    return copies


def _share_copies(Rh):
    def copies(refs, ssem, rsem, waiting):
        x, y, c = _place()
        mine = refs[0].at[pl.ds(c * Rh, Rh), :]
        return [_rcopy(mine, refs[0].at[pl.ds((1 - c) * Rh, Rh), :] if waiting else mine, ssem, rsem, 0, (x, y, 1 - c))]
    return copies


def _sum_all_devices(v):
    R, L = v.shape

    def body(v_ref, out_ref, buf, ssem, rsem):
        x, y, c = _place()
        me = 4 * x + 2 * y + c
        buf[me] = v_ref[...]
        copies = []
        for m in range(1, 8):
            to = (x ^ (m >> 2), y ^ ((m >> 1) & 1), c ^ (m & 1))
            copies.append(_rcopy(v_ref, buf.at[me], ssem, rsem, m - 1, to))
        for cp in copies:
            cp.start()
        for cp in copies:
            cp.wait()
        acc = buf[0]
        for d in range(1, 8):
            acc = acc + buf[d]
        out_ref[...] = acc

    return _pcall(
        body, [v], name="sum_all_devices", in_specs=[VMEM_SPEC], out_specs=VMEM_SPEC,
        out_shape=jax.ShapeDtypeStruct((R, L), F32),
        scratch_shapes=[pltpu.VMEM((8, R, L), F32), pltpu.SemaphoreType.DMA((7,)), pltpu.SemaphoreType.DMA((7,))],
        compiler_params=pltpu.CompilerParams(vmem_limit_bytes=VMEM_LIMIT))


def _elementwise(name, body, ins, in_specs, outs, out_specs, grid):
    return _pcall(body, ins, name=name, grid=grid, in_specs=in_specs, out_specs=out_specs, out_shape=outs,
                  compiler_params=_params(("parallel",) * len(grid)))


def _row_block(R, C, bytes_per_row_elem=4, target=1 << 20):
    return _tile(R, max(16, target // (C * bytes_per_row_elem)), 16)


def _add_halves(name, g4, land, c_idx):
    S, R, C = g4.shape
    Rh = R // 2
    tr = _row_block(Rh, C)
    nb = Rh // tr

    def body(c_ref, g_ref, l_ref, o_ref):
        del c_ref
        o_ref[...] = (g_ref[...].astype(F32) + l_ref[...].astype(F32)).astype(BF16)

    return _pcall(
        body, [c_idx, g4, land], name=name, prefetch=1, grid=(S, nb),
        in_specs=[pl.BlockSpec((None, tr, C), lambda s, i, c: (s, c[0] * nb + i, 0)),
                  pl.BlockSpec((None, tr, C), lambda s, i, c: (s, i, 0))],
        out_specs=pl.BlockSpec((None, tr, C), lambda s, i, c: (s, i, 0)),
        out_shape=jax.ShapeDtypeStruct((S, Rh, C), BF16),
        compiler_params=_params(("parallel", "parallel")))


def _sum_chips(name, land, pb, place):
    S, Rh, C = land.shape
    tr = _row_block(Rh, C)
    nb = Rh // tr

    def body(p_ref, c_ref, l_ref, own_ref, o_ref):
        del p_ref, c_ref
        acc = own_ref[...].astype(F32)
        for s in range(S):
            acc = acc + l_ref[s].astype(F32)
        o_ref[...] = acc

    return _pcall(
        body, [place[0], place[1], land, pb], name=name, prefetch=2, grid=(nb,),
        in_specs=[pl.BlockSpec((S, tr, C), lambda i, p, c: (0, i, 0)),
                  pl.BlockSpec((None, tr, C), lambda i, p, c: (p[0], i, 0))],
        out_specs=pl.BlockSpec((tr, C), lambda i, p, c: (c[0] * nb + i, 0)),
        out_shape=jax.ShapeDtypeStruct((2 * Rh, C), F32),
        compiler_params=_params(("parallel",)))


def _adamw(name, w, g, m, v):
    R, C = g.shape
    tr = _row_block(R, C, target=1 << 19)
    c1 = 1.0 - ADAM_B1 ** ADAM_STEP
    c2 = 1.0 - ADAM_B2 ** ADAM_STEP

    def body(w_ref, g_ref, m_ref, v_ref, go_ref, d_ref, nm_ref, nv_ref):
        gv = g_ref[...]
        go_ref[...] = gv
        nm = ADAM_B1 * m_ref[...] + (1.0 - ADAM_B1) * gv
        nv = ADAM_B2 * v_ref[...] + (1.0 - ADAM_B2) * (gv * gv)
        nm_ref[...] = nm
        nv_ref[...] = nv
        d_ref[...] = -ADAM_LR * ((nm / c1) / (jnp.sqrt(nv / c2) + ADAM_EPS) + ADAM_WD * w_ref[...])

    spec = pl.BlockSpec((tr, C), lambda i: (i, 0))
    wspec = pl.BlockSpec((None, tr, C), lambda i: (0, i, 0)) if w.ndim == 3 else spec
    shp = jax.ShapeDtypeStruct((R, C), F32)
    return _elementwise(name, body, [w, g, m, v], [wspec, spec, wspec, wspec], [shp] * 4, [spec] * 4, (R // tr,))


def _lb_grad(dlb, lb_param):
    def body(d_ref, p_ref, o_ref):
        lb = _lower_bound(p_ref[...])
        t = d_ref[...] * lb * (1.0 - lb)
        o_ref[0:1, :] = t
        o_ref[1:2, :] = -t

    return _pcall(body, [dlb, lb_param], name="lb_grad", in_specs=[VMEM_SPEC, VMEM_SPEC], out_specs=VMEM_SPEC,
                  out_shape=jax.ShapeDtypeStruct(lb_param.shape, F32))


class _Reduction:
    def __init__(self, tag, g4, place):
        self.tag, self.place = tag, place
        _, R, C = g4.shape
        self.Rh = R // 2
        self.state = _split_start("swap_start_" + tag, [g4, _landing((4, self.Rh, C), g4.dtype)], 1, _swap_copies(self.Rh))

    def exchange(self):
        g4, land = _split_wait("swap_wait_" + self.tag, self.state, _swap_copies(self.Rh))
        pb = _add_halves("addh_" + self.tag, g4, land, self.place[1])
        self.state = _split_start("xchg_start_" + self.tag, [pb, _landing((3,) + pb.shape[1:], BF16)], 3, _exchange_copies)

    def finish(self):
        pb, got = _split_wait("xchg_wait_" + self.tag, self.state, _exchange_copies)
        full = _sum_chips("sumc_" + self.tag, got, pb, self.place)
        self.state = _split_start("share_start_" + self.tag, [full], 1, _share_copies(self.Rh))

    def update(self, w, m, v):
        (full,) = _split_wait("share_wait_" + self.tag, self.state, _share_copies(self.Rh))
        return tuple(a[None] for a in _adamw("adamw_" + self.tag, w, full, m, v))


def kernel(x, p, pre_mix_w, w_in, lb_param, a_norm_w, gmlp_ln_w, gmlp_ln_b, w_spatial, b_spatial, w_out, post_mix_w, pre_ffn_w, w_gate, w_up, w_down, post_ffn_w, w_ple, w_ple_gate, post_ple_w, loss_target, m_pre_mix_w, m_w_in, m_lb_param, m_a_norm_w, m_gmlp_ln_w, m_gmlp_ln_b, m_w_spatial, m_b_spatial, m_w_out, m_post_mix_w, m_pre_ffn_w, m_w_gate, m_w_up, m_w_down, m_post_ffn_w, m_w_ple, m_w_ple_gate, m_post_ple_w, v_pre_mix_w, v_w_in, v_lb_param, v_a_norm_w, v_gmlp_ln_w, v_gmlp_ln_b, v_w_spatial, v_b_spatial, v_w_out, v_post_mix_w, v_pre_ffn_w, v_w_gate, v_w_up, v_w_down, v_post_ffn_w, v_w_ple, v_w_ple_gate, v_post_ple_w):
    T, D = x.shape[1], x.shape[2]
    Wd = D // 2
    G = Wd // HEAD
    NCH = 4
    Ci = w_in.shape[2]
    Fs = w_gate.shape[2]
    Dq = w_out.shape[1]
    Pd = w_ple.shape[1]
    xs, ps, tgt = x[0], p[0, 0], loss_target[0]
    c_idx = lax.axis_index("c").astype(jnp.int32).reshape((1,))
    place = ((2 * lax.axis_index("x") + lax.axis_index("y")).astype(jnp.int32).reshape((1,)), c_idx)

    swap = lambda a: jnp.swapaxes(a, 1, 2)
    transposed = ("w_gate", "w_up")
    weights = dict(w_in=w_in, w_out=w_out, w_gate=swap(w_gate), w_up=swap(w_up), w_down=w_down,
                   w_ple_gate=w_ple_gate, w_ple=w_ple)
    _Order.last = None
    gathering = {}
    for tag, w in weights.items():
        buf = _cast_slot("cast_" + tag, w, place)
        gathering[tag] = _split_start("gather_start_" + tag, [buf], 3, _gather_copies(buf.shape[1] // 2))

    forwarding = {}

    def forward_start(tag):
        Rh = weights[tag].shape[1] // 2
        (buf,) = _split_wait("gather_wait_" + tag, gathering.pop(tag), _gather_copies(Rh))
        forwarding[tag] = _split_start("forward_start_" + tag, [buf], 3, _forward_copies(Rh))

    def gathered(tag):
        (buf,) = _split_wait("forward_wait_" + tag, forwarding.pop(tag), _forward_copies(weights[tag].shape[1] // 2))
        return buf

    res = {}
    moments = dict(w_in=(m_w_in, v_w_in), w_out=(m_w_out, v_w_out), w_gate=(swap(m_w_gate), swap(v_w_gate)),
                   w_up=(swap(m_w_up), swap(v_w_up)), w_down=(m_w_down, v_w_down), w_ple=(m_w_ple, v_w_ple),
                   w_ple_gate=(m_w_ple_gate, v_w_ple_gate))
    reducing = {}

    def reduce_start(tag, g4):
        reducing[tag] = _Reduction(tag, g4, place)

    def reduce_exchange(tag):
        reducing[tag].exchange()

    def reduce_finish(tag):
        reducing[tag].finish()

    def reduce_update(tag):
        outs = reducing.pop(tag).update(weights[tag], *moments[tag])
        res[tag] = tuple(swap(a) for a in outs) if tag in transposed else outs

    tm, tn, tk = _tile(T, 1024), _tile(D, 1024), _tile(Wd, 1024)
    ts = _tile(math.gcd(Wd, Ci), 1024)
    tkt = _tile(T, 1024)
    n_sec, n_shd = Wd // ts, Ci // ts
    b_st = jnp.pad(b_spatial[0].T, ((0, 0), (0, LANES - G)))

    forward_start("w_in")
    h1 = _norm_in(xs, pre_mix_w)
    gw_in = gathered("w_in")
    proj6 = _mm("proj", [h1, gw_in],
                [pl.BlockSpec((tm, tk), lambda i, j, k: (i, k)),
                 pl.BlockSpec((None, tk, ts), lambda i, j, k: (j // n_shd, k, j % n_shd))],
                jax.ShapeDtypeStruct((6, T, Wd), F32),
                pl.BlockSpec((None, tm, ts), lambda i, j, k: (j // n_sec, i, j % n_sec)),
                (tm, ts), (T // tm, 6 * n_sec, D // tk), NN)
    forward_start("w_out")
    cat, o_a, states = _hgrn2_fwd(proj6, lb_param, a_norm_w)
    forward_start("w_gate")
    cat = _gmlp_fwd(proj6, cat, gmlp_ln_w, gmlp_ln_b, w_spatial[0], b_st)
    nkc = Wd // tk
    gw_out = gathered("w_out").reshape(D, D)
    mix = _mm("mix", [cat, gw_out],
              [pl.BlockSpec((None, tm, tk), lambda i, j, k: (k // nkc, i, k % nkc)),
               pl.BlockSpec((tk, tn), lambda i, j, k: (k, j))],
              jax.ShapeDtypeStruct((T, D), F32), pl.BlockSpec((tm, tn), lambda i, j, k: (i, j)),
              (tm, tn), (T // tm, D // tn, D // tk), NN)
    forward_start("w_up")
    x1, h2 = _mid1(xs, mix, post_mix_w, pre_ffn_w)
    gw_gate = gathered("w_gate")
    gw_up = gathered("w_up")
    gt, up, act = _ffn_up(h2, gw_gate, gw_up)
    tnf = _tile(D, 512)
    forward_start("w_down")
    forward_start("w_ple_gate")
    forward_start("w_ple")
    gw_down = gathered("w_down")
    ff = _mm("ffn_down", [act, gw_down],
             [pl.BlockSpec((None, tm, Fs), lambda i, j, k: (k, i, 0)),
              pl.BlockSpec((None, Fs, tnf), lambda i, j, k: (k, 0, j))],
             jax.ShapeDtypeStruct((T, D), F32), pl.BlockSpec((tm, tnf), lambda i, j, k: (i, j)),
             (tm, tnf), (T // tm, D // tnf, NCH), NN)
    x2, x2b = _mid2(x1, ff, post_ffn_w)
    gw_pg = gathered("w_ple_gate").reshape(D, D)
    gl = _mm("ple_gate", [x2b, gw_pg],
             [pl.BlockSpec((tm, tk), lambda i, j, k: (i, k)), pl.BlockSpec((tk, tn), lambda i, j, k: (k, j))],
             jax.ShapeDtypeStruct((T, D), F32), pl.BlockSpec((tm, tn), lambda i, j, k: (i, j)),
             (tm, tn), (T // tm, D // tn, D // tk), NN)
    tq = _tile(Dq, 1024)
    nq = Dq // tq
    gw_ple = gathered("w_ple")
    e = _mm("ple_embed", [ps, gw_ple],
            [pl.BlockSpec((tm, Pd), lambda i, j, k: (i, 0)),
             pl.BlockSpec((None, Pd, tq), lambda i, j, k: (j // nq, 0, j % nq))],
            jax.ShapeDtypeStruct((T, D), F32), pl.BlockSpec((tm, tq), lambda i, j, k: (i, j)),
            (tm, tq), (T // tm, D // tq, 1), NN)

    d3, de, dgl, dw_pp, loss_vec = _head(x2, gl, e, post_ple_w, tgt)
    g_ple = _mm("d_w_ple", [ps, de],
                [pl.BlockSpec((tkt, Pd), lambda i, j, k: (k, 0)), pl.BlockSpec((tkt, tq), lambda i, j, k: (k, j))],
                jax.ShapeDtypeStruct((NCH, Pd, Dq), PARTIAL),
                pl.BlockSpec((None, Pd, tq), lambda i, j, k: (j // nq, 0, j % nq)),
                (Pd, tq), (1, D // tq, T // tkt), TN)
    reduce_start("w_ple", g_ple)
    tmw = _tile(D, 1024)
    g_pg = _mm("d_w_ple_gate", [x2b, dgl],
               [pl.BlockSpec((tkt, tq), lambda i, j, k: (k, i)), pl.BlockSpec((tkt, tn), lambda i, j, k: (k, j))],
               jax.ShapeDtypeStruct((NCH, Dq, D), PARTIAL), pl.BlockSpec((None, tq, tn), lambda i, j, k: (i // nq, i % nq, j)),
               (tq, tn), (D // tq, D // tn, T // tkt), TN)
    reduce_start("w_ple_gate", g_pg)
    reduce_exchange("w_ple")
    dx2 = _mm("d_x2", [dgl, gw_pg],
              [pl.BlockSpec((tm, tk), lambda i, j, k: (i, k)), pl.BlockSpec((tn, tk), lambda i, j, k: (j, k))],
              jax.ShapeDtypeStruct((T, D), F32), pl.BlockSpec((tm, tn), lambda i, j, k: (i, j)),
              (tm, tn), (T // tm, D // tn, D // tk), NT,
              add=d3, add_spec=pl.BlockSpec((tm, tn), lambda i, j, k: (i, j)))
    reduce_exchange("w_ple_gate")
    dff, dw_pff = _bwd_ffn_norm(dx2, ff, post_ffn_w)
    g_down = _mm("d_w_down", [act, dff],
                 [pl.BlockSpec((None, tkt, Fs), lambda i, j, k: (i, k, 0)),
                  pl.BlockSpec((tkt, tnf), lambda i, j, k: (k, j))],
                 jax.ShapeDtypeStruct((NCH, Fs, D), PARTIAL), pl.BlockSpec((None, Fs, tnf), lambda i, j, k: (i, 0, j)),
                 (Fs, tnf), (NCH, D // tnf, T // tkt), TN)
    reduce_start("w_down", g_down)
    reduce_finish("w_ple")
    dgt, dup = _ffn_dact(dff, gw_down, gt, up)
    reduce_exchange("w_down")
    reduce_finish("w_ple_gate")
    reduce_update("w_ple")
    wgrad_specs = [pl.BlockSpec((None, tkt, Fs), lambda i, j, k: (i, k, 0)),
                   pl.BlockSpec((tkt, tnf), lambda i, j, k: (k, j))]
    wgrad_out = pl.BlockSpec((None, Fs, tnf), lambda i, j, k: (i, 0, j))
    g_gate = _mm("d_w_gate", [dgt, h2], wgrad_specs, jax.ShapeDtypeStruct((NCH, Fs, D), PARTIAL), wgrad_out,
                 (Fs, tnf), (NCH, D // tnf, T // tkt), TN)
    reduce_start("w_gate", g_gate)
    reduce_update("w_ple_gate")
    g_up = _mm("d_w_up", [dup, h2], wgrad_specs, jax.ShapeDtypeStruct((NCH, Fs, D), PARTIAL), wgrad_out,
               (Fs, tnf), (NCH, D // tnf, T // tkt), TN)
    reduce_start("w_up", g_up)
    reduce_exchange("w_gate")
    tmh = _tile(T, 1024)
    a_sp = pl.BlockSpec((None, tmh, Fs), lambda i, j, k: (k, i, 0))
    b_sp = pl.BlockSpec((None, Fs, tnf), lambda i, j, k: (k, 0, j))
    dh2 = _mm("d_h2", [dgt, gw_gate, dup, gw_up], [a_sp, b_sp, a_sp, b_sp],
              jax.ShapeDtypeStruct((T, D), F32), pl.BlockSpec((tmh, tnf), lambda i, j, k: (i, j)),
              (tmh, tnf), (T // tmh, D // tnf, NCH), NN)
    reduce_exchange("w_up")
    reduce_finish("w_down")
    dx1, dmix, dw_pf, dw_pm = _bwd_mid(dx2, dh2, x1, pre_ffn_w, mix, post_mix_w)
    reduce_update("w_down")
    tw = _tile(Wd, 1024)
    nw = Wd // tw
    nwq = Wd // tq
    g_out = _mm("d_w_out", [cat, dmix],
                [pl.BlockSpec((None, tkt, tq), lambda i, j, k: (i // nwq, k, i % nwq)),
                 pl.BlockSpec((tkt, tn), lambda i, j, k: (k, j))],
                jax.ShapeDtypeStruct((NCH, Dq, D), PARTIAL), pl.BlockSpec((None, tq, tn), lambda i, j, k: (i // nq, i % nq, j)),
                (tq, tn), (D // tq, D // tn, T // tkt), TN)
    reduce_start("w_out", g_out)
    reduce_finish("w_gate")
    dcat = _mm("d_cat", [dmix, gw_out],
               [pl.BlockSpec((tm, tk), lambda i, j, k: (i, k)), pl.BlockSpec((tw, tk), lambda i, j, k: (j, k))],
               jax.ShapeDtypeStruct((2, T, Wd), F32), pl.BlockSpec((None, tm, tw), lambda i, j, k: (j // nw, i, j % nw)),
               (tm, tw), (T // tm, D // tw, D // tk), NT)
    reduce_exchange("w_out")
    reduce_update("w_gate")
    dproj6, dw_an, dlb = _hgrn2_bwd(proj6, o_a, states, dcat, lb_param, a_norm_w)
    reduce_finish("w_up")
    dproj6, dw_lnw, dw_lnb, dw_sp, dw_bst = _gmlp_bwd(proj6, dproj6, dcat, gmlp_ln_w, gmlp_ln_b, w_spatial[0], b_st)
    reduce_update("w_up")
    g_in = _mm("d_w_in", [h1, dproj6],
               [pl.BlockSpec((tkt, tmw), lambda i, j, k: (k, i)),
                pl.BlockSpec((None, tkt, ts), lambda i, j, k: (j // n_sec, k, j % n_sec))],
               jax.ShapeDtypeStruct((NCH, D, Ci), PARTIAL),
               pl.BlockSpec((None, tmw, ts), lambda i, j, k: (j // n_shd, i, j % n_shd)),
               (tmw, ts), (D // tmw, 6 * n_sec, T // tkt), TN)
    reduce_start("w_in", g_in)
    reduce_finish("w_out")
    reduce_exchange("w_in")
    dh1 = _mm("d_h1", [dproj6, gw_in],
              [pl.BlockSpec((None, tm, ts), lambda i, j, k: (k // n_sec, i, k % n_sec)),
               pl.BlockSpec((None, tn, ts), lambda i, j, k: (k // n_shd, j, k % n_shd))],
              jax.ShapeDtypeStruct((T, D), F32), pl.BlockSpec((tm, tn), lambda i, j, k: (i, j)),
              (tm, tn), (T // tm, D // tn, 6 * n_sec), NT)
    reduce_update("w_out")
    grad_x, dw_pre = _bwd_in(dx1, dh1, xs, pre_mix_w)

    small = [("pre_mix_w", dw_pre), ("lb", dlb), ("a_norm_w", dw_an), ("gmlp_ln_w", dw_lnw), ("gmlp_ln_b", dw_lnb),
             ("w_spatial", dw_sp), ("b_spatial_t", dw_bst), ("post_mix_w", dw_pm), ("pre_ffn_w", dw_pf),
             ("post_ffn_w", dw_pff), ("post_ple_w", dw_pp), ("loss", loss_vec)]
    rows = [a.size // LANES for _, a in small]
    total = sum(rows)
    pad = (-total) % 8
    packed = jnp.concatenate([a.reshape(-1, LANES) for _, a in small] + [jnp.zeros((pad, LANES), F32)], axis=0)
    summed = _sum_all_devices(packed)
    off, piece = 0, {}
    for (nm, a), r in zip(small, rows):
        piece[nm] = summed[off:off + r].reshape(a.shape)
        off += r
    loss = jnp.sum(piece["loss"])
    g_small = {
        "pre_mix_w": piece["pre_mix_w"], "lb_param": _lb_grad(piece["lb"], lb_param), "a_norm_w": piece["a_norm_w"],
        "gmlp_ln_w": piece["gmlp_ln_w"], "gmlp_ln_b": piece["gmlp_ln_b"], "w_spatial": piece["w_spatial"][None],
        "b_spatial": piece["b_spatial_t"][:, :G].T[None], "post_mix_w": piece["post_mix_w"],
        "pre_ffn_w": piece["pre_ffn_w"], "post_ffn_w": piece["post_ffn_w"], "post_ple_w": piece["post_ple_w"],
    }
    w_small = dict(pre_mix_w=(pre_mix_w, m_pre_mix_w, v_pre_mix_w), lb_param=(lb_param, m_lb_param, v_lb_param),
                   a_norm_w=(a_norm_w, m_a_norm_w, v_a_norm_w), gmlp_ln_w=(gmlp_ln_w, m_gmlp_ln_w, v_gmlp_ln_w),
                   gmlp_ln_b=(gmlp_ln_b, m_gmlp_ln_b, v_gmlp_ln_b), w_spatial=(w_spatial, m_w_spatial, v_w_spatial),
                   b_spatial=(b_spatial, m_b_spatial, v_b_spatial), post_mix_w=(post_mix_w, m_post_mix_w, v_post_mix_w),
                   pre_ffn_w=(pre_ffn_w, m_pre_ffn_w, v_pre_ffn_w), post_ffn_w=(post_ffn_w, m_post_ffn_w, v_post_ffn_w),
                   post_ple_w=(post_ple_w, m_post_ple_w, v_post_ple_w))
    names_small = list(w_small)

    def pack(arrs):
        flat = jnp.concatenate([a.reshape(-1, LANES) for a in arrs], axis=0)
        return jnp.pad(flat, ((0, (-flat.shape[0]) % 16), (0, 0)))

    pk = [pack([w_small[n][i] for n in names_small]) for i in range(3)]
    pg = pack([g_small[n] for n in names_small])
    _, sd, sm, sv = _adamw("adamw_small", pk[0], pg, pk[1], pk[2])
    off = 0
    for n in names_small:
        shp = w_small[n][0].shape
        r = w_small[n][0].size // LANES
        res[n] = (g_small[n].reshape(shp), sd[off:off + r].reshape(shp), sm[off:off + r].reshape(shp),
                  sv[off:off + r].reshape(shp))
        off += r

    reduce_finish("w_in")
    reduce_update("w_in")

    order =["pre_mix_w", "w_in", "lb_param", "a_norm_w", "gmlp_ln_w", "gmlp_ln_b", "w_spatial", "b_spatial", "w_out",
             "post_mix_w", "pre_ffn_w", "w_gate", "w_up", "w_down", "post_ffn_w", "w_ple", "w_ple_gate", "post_ple_w"]
    return (loss, grad_x[None], *[res[n][0] for n in order], *[res[n][1] for n in order],
            *[res[n][2] for n in order], *[res[n][3] for n in order])
```

```python
import functools
import math

import jax
import jax.numpy as jnp
from jax import lax
from jax.experimental import pallas as pl
from jax.experimental.pallas import tpu as pltpu

F32 = jnp.float32
BF16 = jnp.bfloat16
PARTIAL = jnp.bfloat16
EPS = 1e-6
HEAD = 128
GLA_CHUNK = 64
HEADS_PER_STEP = 4
GMLP_CHUNK = 128
LANES = 128
VMEM_LIMIT = 48 * 1024 * 1024
FFN_UP_VMEM_LIMIT = 58 * 1024 * 1024
ADAM_LR, ADAM_B1, ADAM_B2, ADAM_EPS, ADAM_WD, ADAM_STEP = 0.001, 0.9, 0.999, 1e-08, 0.01, 10
MESH = pl.DeviceIdType.MESH
HBM_SPEC = pl.BlockSpec(memory_space=pltpu.HBM)
VMEM_SPEC = pl.BlockSpec(memory_space=pltpu.VMEM)
SEM_SPEC = pl.BlockSpec(memory_space=pltpu.SEMAPHORE)
SIDE_EFFECT = pltpu.SideEffectType.DATAFLOW_SIDE_EFFECTING

NN = ((1,), (0,))
NT = ((1,), (1,))
TN = ((0,), (0,))


def _tile(n, pref, mult=LANES):
    if n <= pref:
        return n
    t = (pref // mult) * mult
    while t >= mult:
        if n % t == 0:
            return t
        t -= mult
    return n


def _params(sem, vmem_limit=VMEM_LIMIT):
    return pltpu.CompilerParams(dimension_semantics=sem, vmem_limit_bytes=vmem_limit)


class _Order:
    last = None


def _pcall(body, args, *, token=0, prefetch=0, grid=(), in_specs, out_specs, scratch_shapes=(), same=(), **kw):
    args, in_specs = list(args), list(in_specs)
    n = len(args)
    run = body
    if _Order.last is not None and not any(_Order.last is a for a in [*args, *same]):
        def run(*refs):
            return body(*refs[:n], *refs[n + 1:])
        args.append(_Order.last)
        in_specs.append(pl.BlockSpec(memory_space=pl.ANY))
    if prefetch:
        kw["grid_spec"] = pltpu.PrefetchScalarGridSpec(num_scalar_prefetch=prefetch, grid=grid, in_specs=in_specs,
                                                       out_specs=out_specs, scratch_shapes=list(scratch_shapes))
    else:
        kw.update(grid=grid, in_specs=in_specs, out_specs=out_specs, scratch_shapes=list(scratch_shapes))
    outs = pl.pallas_call(run, **kw)(*args)
    _Order.last = outs[token] if isinstance(outs, (tuple, list)) else outs
    return outs


def _dot(a, b, dims, precision=None):
    return lax.dot_general(a, b, (dims, ((), ())), preferred_element_type=F32, precision=precision)


def _bdot(a, b, dims):
    return _dot(a.astype(BF16), b.astype(BF16), dims)


def _split_bf16(v):
    hi = v.astype(BF16)
    return hi, (v - hi.astype(F32)).astype(BF16)


def _fdot(a, b, dims):
    a_hi, a_lo = _split_bf16(a)
    b_hi, b_lo = _split_bf16(b)
    return _dot(a_hi, b_hi, dims) + (_dot(a_hi, b_lo, dims) + _dot(a_lo, b_hi, dims))


def _mask_dot(mask, v):
    m = mask.astype(BF16)
    hi, lo = _split_bf16(v)
    lo2 = (v - hi.astype(F32) - lo.astype(F32)).astype(BF16)
    return _dot(m, hi, NN) + (_dot(m, lo, NN) + _dot(m, lo2, NN))


def _sigmoid(v):
    return 1.0 / (1.0 + jnp.exp(-v))


def _mm(name, operands, specs, out_shape, out_spec, acc_shape, grid, dims, add=None, add_spec=None):
    n_pairs = len(operands) // 2
    nk = grid[2]

    def body(*refs):
        ins = refs[:2 * n_pairs]
        r_ref = refs[2 * n_pairs] if add is not None else None
        o_ref, acc = refs[-2], refs[-1]
        k = pl.program_id(2)

        @pl.when(k == 0)
        def _():
            acc[...] = jnp.zeros_like(acc)

        part = _bdot(ins[0][...], ins[1][...], dims)
        for q in range(1, n_pairs):
            part = part + _bdot(ins[2 * q][...], ins[2 * q + 1][...], dims)
        acc[...] += part

        @pl.when(k == nk - 1)
        def _():
            r = acc[...]
            if r_ref is not None:
                r = r + r_ref[...].astype(F32)
            o_ref[...] = r.astype(o_ref.dtype)

    in_specs = list(specs) + ([add_spec] if add is not None else [])
    args = list(operands) + ([add] if add is not None else [])
    return _pcall(
        body, args, name=name, grid=grid, in_specs=in_specs, out_specs=out_spec, out_shape=out_shape,
        scratch_shapes=[pltpu.VMEM(acc_shape, F32)],
        compiler_params=_params(("parallel", "parallel", "arbitrary")),
    )


def _rms(v, w):
    r = lax.rsqrt(jnp.mean(v * v, axis=-1, keepdims=True) + EPS)
    return v * r * w


def _rms_bwd(v, w, dy):
    r = lax.rsqrt(jnp.mean(v * v, axis=-1, keepdims=True) + EPS)
    vh = v * r
    gy = dy * w
    dv = r * (gy - vh * jnp.mean(gy * vh, axis=-1, keepdims=True))
    return dv, jnp.sum(dy * vh, axis=0, keepdims=True)


def _rowwise(name, body, ins, in_kinds, outs, out_kinds, T, D):
    tr = _tile(T, 128, 16)
    row = pl.BlockSpec((tr, D), lambda i: (i, 0))
    vec = pl.BlockSpec((1, D), lambda i: (0, 0))
    pick = {"row": row, "vec": vec}
    return _pcall(
        body, ins, name=name, grid=(T // tr,),
        in_specs=[pick[k] for k in in_kinds], out_specs=[pick[k] for k in out_kinds], out_shape=outs,
        compiler_params=_params(("arbitrary",)),
    )


def _acc_vec(ref, val):
    @pl.when(pl.program_id(0) == 0)
    def _():
        ref[...] = jnp.zeros_like(ref)
    ref[...] += val


def _norm_in(x, w):
    T, D = x.shape

    def body(x_ref, w_ref, h_ref):
        h_ref[...] = _rms(x_ref[...], w_ref[...]).astype(BF16)

    return _rowwise("norm_in", body, [x, w], ["row", "vec"], [jax.ShapeDtypeStruct((T, D), BF16)], ["row"], T, D)[0]


def _mid1(x, mix, w_pm, w_pf):
    T, D = x.shape

    def body(x_ref, mix_ref, wpm_ref, wpf_ref, x1_ref, h2_ref):
        x1 = x_ref[...] + _rms(mix_ref[...], wpm_ref[...])
        x1_ref[...] = x1
        h2_ref[...] = _rms(x1, wpf_ref[...]).astype(BF16)

    return _rowwise("mid1", body, [x, mix, w_pm, w_pf], ["row", "row", "vec", "vec"],
                    [jax.ShapeDtypeStruct((T, D), F32), jax.ShapeDtypeStruct((T, D), BF16)], ["row", "row"], T, D)


def _mid2(x1, ff, w_pff):
    T, D = x1.shape

    def body(x1_ref, ff_ref, w_ref, x2_ref, x2b_ref):
        x2 = x1_ref[...] + _rms(ff_ref[...], w_ref[...])
        x2_ref[...] = x2
        x2b_ref[...] = x2.astype(BF16)

    return _rowwise("mid2", body, [x1, ff, w_pff], ["row", "row", "vec"],
                    [jax.ShapeDtypeStruct((T, D), F32), jax.ShapeDtypeStruct((T, D), BF16)], ["row", "row"], T, D)


def _head(x2, gl, e, w_pp, tgt):
    T, D = x2.shape

    def body(x2_ref, gl_ref, e_ref, w_ref, t_ref, d3_ref, de_ref, dgl_ref, dw_ref, loss_ref):
        ev = e_ref[...]
        gate = _sigmoid(gl_ref[...])
        pe = ev * gate
        w = w_ref[...]
        err = x2_ref[...] + _rms(pe, w) - t_ref[...]
        d3 = err * (1.0 / D)
        d3_ref[...] = d3
        dpe, dw = _rms_bwd(pe, w, d3)
        de_ref[...] = (dpe * gate).astype(BF16)
        dgl_ref[...] = (dpe * ev * gate * (1.0 - gate)).astype(BF16)
        _acc_vec(dw_ref, dw)
        _acc_vec(loss_ref, jnp.sum(err * err, axis=0, keepdims=True) * (0.5 / D))

    return _rowwise("head", body, [x2, gl, e, w_pp, tgt], ["row", "row", "row", "vec", "row"],
                    [jax.ShapeDtypeStruct((T, D), F32), jax.ShapeDtypeStruct((T, D), BF16),
                     jax.ShapeDtypeStruct((T, D), BF16), jax.ShapeDtypeStruct((1, D), F32),
                     jax.ShapeDtypeStruct((1, D), F32)], ["row", "row", "row", "vec", "vec"], T, D)


def _bwd_ffn_norm(dx2, ff, w_pff):
    T, D = dx2.shape

    def body(d_ref, ff_ref, w_ref, dff_ref, dw_ref):
        dff, dw = _rms_bwd(ff_ref[...], w_ref[...], d_ref[...])
        dff_ref[...] = dff.astype(BF16)
        _acc_vec(dw_ref, dw)

    return _rowwise("bwd_ffn_norm", body, [dx2, ff, w_pff], ["row", "row", "vec"],
                    [jax.ShapeDtypeStruct((T, D), BF16), jax.ShapeDtypeStruct((1, D), F32)], ["row", "vec"], T, D)


def _bwd_mid(dx2, dh2, x1, w_pf, mix, w_pm):
    T, D = dx2.shape

    def body(dx2_ref, dh2_ref, x1_ref, wpf_ref, mix_ref, wpm_ref, dx1_ref, dmix_ref, dwpf_ref, dwpm_ref):
        d1, dwpf = _rms_bwd(x1_ref[...], wpf_ref[...], dh2_ref[...])
        dx1 = dx2_ref[...] + d1
        dx1_ref[...] = dx1
        dmix, dwpm = _rms_bwd(mix_ref[...], wpm_ref[...], dx1)
        dmix_ref[...] = dmix.astype(BF16)
        _acc_vec(dwpf_ref, dwpf)
        _acc_vec(dwpm_ref, dwpm)

    return _rowwise("bwd_mid", body, [dx2, dh2, x1, w_pf, mix, w_pm], ["row", "row", "row", "vec", "row", "vec"],
                    [jax.ShapeDtypeStruct((T, D), F32), jax.ShapeDtypeStruct((T, D), BF16),
                     jax.ShapeDtypeStruct((1, D), F32), jax.ShapeDtypeStruct((1, D), F32)],
                    ["row", "row", "vec", "vec"], T, D)


def _bwd_in(dx1, dh1, x, w_pre):
    T, D = dx1.shape

    def body(dx1_ref, dh1_ref, x_ref, w_ref, gx_ref, dw_ref):
        d0, dw = _rms_bwd(x_ref[...], w_ref[...], dh1_ref[...])
        gx_ref[...] = dx1_ref[...] + d0
        _acc_vec(dw_ref, dw)

    return _rowwise("bwd_in", body, [dx1, dh1, x, w_pre], ["row", "row", "row", "vec"],
                    [jax.ShapeDtypeStruct((T, D), F32), jax.ShapeDtypeStruct((1, D), F32)], ["row", "vec"], T, D)


def _ffn_up(h2, wg, wu):
    T, D = h2.shape
    S, Fs, _ = wg.shape
    tm, tk = _tile(T, 512), _tile(D, 1024)
    nk = D // tk
    te = _tile(tm, 128, 16)

    def body(h_ref, wg_ref, wu_ref, gt_ref, up_ref, act_ref, accg, accu):
        k = pl.program_id(2)

        @pl.when(k == 0)
        def _():
            accg[...] = jnp.zeros_like(accg)
            accu[...] = jnp.zeros_like(accu)

        h = h_ref[...]
        accg[...] += _dot(h, wg_ref[...], NT)
        accu[...] += _dot(h, wu_ref[...], NT)

        @pl.when(k == nk - 1)
        def _():
            for r in range(0, tm, te):
                rows = slice(r, r + te)
                g, u = accg[rows, :], accu[rows, :]
                gt_ref[rows, :] = g.astype(BF16)
                up_ref[rows, :] = u.astype(BF16)
                act_ref[rows, :] = (g * _sigmoid(g) * u).astype(BF16)

    w_spec = pl.BlockSpec((None, Fs, tk), lambda i, j, k: (j, 0, k))
    o_spec = pl.BlockSpec((None, tm, Fs), lambda i, j, k: (j, i, 0))
    return _pcall(
        body, [h2, wg, wu], name="ffn_up", grid=(T // tm, S, nk),
        in_specs=[pl.BlockSpec((tm, tk), lambda i, j, k: (i, k)), w_spec, w_spec],
        out_specs=[o_spec, o_spec, o_spec],
        out_shape=[jax.ShapeDtypeStruct((S, T, Fs), BF16)] * 3,
        scratch_shapes=[pltpu.VMEM((tm, Fs), F32), pltpu.VMEM((tm, Fs), F32)],
        compiler_params=_params(("parallel", "parallel", "arbitrary"), FFN_UP_VMEM_LIMIT),
    )


def _ffn_dact(dff, wd, gt, up):
    T, D = dff.shape
    S, Fs, _ = wd.shape
    tm, tk = _tile(T, 512), _tile(D, 1024)
    nk = D // tk
    te = _tile(tm, 128, 16)

    def body(d_ref, w_ref, gt_ref, up_ref, dgt_ref, dup_ref, acc):
        k = pl.program_id(2)

        @pl.when(k == 0)
        def _():
            acc[...] = jnp.zeros_like(acc)

        acc[...] += _dot(d_ref[...], w_ref[...], NT)

        @pl.when(k == nk - 1)
        def _():
            for r in range(0, tm, te):
                rows = slice(r, r + te)
                da, g, u = acc[rows, :], gt_ref[rows, :].astype(F32), up_ref[rows, :].astype(F32)
                sg = _sigmoid(g)
                dgt_ref[rows, :] = (da * u * sg * (1.0 + g * (1.0 - sg))).astype(BF16)
                dup_ref[rows, :] = (da * g * sg).astype(BF16)

    o_spec = pl.BlockSpec((None, tm, Fs), lambda i, j, k: (j, i, 0))
    return _pcall(
        body, [dff, wd, gt, up], name="ffn_dact", grid=(T // tm, S, nk),
        in_specs=[pl.BlockSpec((tm, tk), lambda i, j, k: (i, k)),
                  pl.BlockSpec((None, Fs, tk), lambda i, j, k: (j, 0, k)), o_spec, o_spec],
        out_specs=[o_spec, o_spec],
        out_shape=[jax.ShapeDtypeStruct((S, T, Fs), BF16), jax.ShapeDtypeStruct((S, T, Fs), BF16)],
        scratch_shapes=[pltpu.VMEM((tm, Fs), F32)],
        compiler_params=_params(("parallel", "parallel", "arbitrary")),
    )


def _hgrn2_gates(q, f, lb):
    C = GLA_CHUNK
    sq = _sigmoid(q)
    qs = q * sq
    sf = _sigmoid(f)
    fg = lb + (1.0 - lb) * sf
    lf = jnp.log(jnp.maximum(fg, 1e-30))
    row = lax.broadcasted_iota(jnp.int32, (C, C), 0)
    col = lax.broadcasted_iota(jnp.int32, (C, C), 1)
    tril = row >= col
    b = _mask_dot(jnp.where(tril, 1.0, 0.0), lf)
    bmid = b[C // 2 - 1:C // 2, :]
    bend = b[C - 1:C, :]
    ea = jnp.exp(b - bmid)
    ena = jnp.exp(bmid - b)
    eb = ea * jnp.exp(bmid)
    ee = ena * jnp.exp(bend - bmid)
    return sq, qs, sf, fg, tril, ea, ena, eb, ee, jnp.exp(bend)


def _lower_bound(lbp):
    return _sigmoid(lbp[0:1, :] - lbp[1:2, :])


def _hgrn2_fwd(proj6, lb_param, a_norm_w):
    _, T, Wd = proj6.shape
    H, C = Wd // HEAD, GLA_CHUNK
    TB = _tile(T, 512, C)
    NB, NCB = T // TB, TB // C

    HS, W2 = HEADS_PER_STEP, HEADS_PER_STEP * HEAD

    def body(p_ref, lbp_ref, w_ref, cat_ref, o_ref, st_ref, S):
        @pl.when(pl.program_id(1) == 0)
        def _():
            S[...] = jnp.zeros_like(S)

        lanes = [slice(hh * HEAD, (hh + 1) * HEAD) for hh in range(HS)]
        lbs = [_lower_bound(lbp_ref[:, ls]) for ls in lanes]
        ws = [w_ref[:, ls] for ls in lanes]

        def chunk(c, carry):
            rows = pl.ds(pl.multiple_of(c * C, C), C)
            loaded = [([p_ref[s, rows, ls] for s in range(4)], S[hh]) for hh, ls in enumerate(lanes)]
            done = []
            for hh, ((q, f, v, g), st) in enumerate(loaded):
                _, qs, _, fg, tril, ea, ena, eb, ee, ebend = _hgrn2_gates(q, f, lbs[hh])
                k = 1.0 - fg
                p = jnp.where(tril, _bdot(qs * ea, k * ena, NT), 0.0)
                o = _bdot(qs * eb, st, NT) + _bdot(p, v, NN)
                r = lax.rsqrt(jnp.mean(o * o, axis=-1, keepdims=True) + EPS)
                done.append((st, st * ebend + _bdot(v, k * ee, TN), o,
                             (o * r * ws[hh] * (g * _sigmoid(g))).astype(BF16)))
            for hh, ls in enumerate(lanes):
                st, st_new, o, a_out = done[hh]
                st_ref[hh, c] = st
                S[hh] = st_new
                o_ref[rows, ls] = o
                cat_ref[rows, ls] = a_out
            return carry

        lax.fori_loop(0, NCB, chunk, 0)

    return _pcall(
        body, [proj6, lb_param, a_norm_w], name="hgrn2_fwd", grid=(H // HS, NB),
        in_specs=[pl.BlockSpec((4, TB, W2), lambda h, t: (0, t, h)),
                  pl.BlockSpec((2, W2), lambda h, t: (0, h)),
                  pl.BlockSpec((1, W2), lambda h, t: (0, h))],
        out_specs=[pl.BlockSpec((None, TB, W2), lambda h, t: (0, t, h)),
                   pl.BlockSpec((TB, W2), lambda h, t: (t, h)),
                   pl.BlockSpec((HS, NCB, HEAD, HEAD), lambda h, t: (h, t, 0, 0))],
        out_shape=[jax.ShapeDtypeStruct((2, T, Wd), BF16), jax.ShapeDtypeStruct((T, Wd), F32),
                   jax.ShapeDtypeStruct((H, T // C, HEAD, HEAD), F32)],
        scratch_shapes=[pltpu.VMEM((HS, HEAD, HEAD), F32)],
        compiler_params=_params(("parallel", "arbitrary")),
    )


def _hgrn2_bwd(proj6, o, states, dcat, lb_param, a_norm_w):
    _, T, Wd = proj6.shape
    H, C = Wd // HEAD, GLA_CHUNK
    TB = _tile(T, 512, C)
    NB, NCB = T // TB, TB // C
    HS, W2 = HEADS_PER_STEP, HEADS_PER_STEP * HEAD

    def body(p_ref, o_ref, st_ref, da_ref, lbp_ref, w_ref, dp_ref, dw_ref, dlb_ref, dS):
        @pl.when(pl.program_id(1) == 0)
        def _():
            dS[...] = jnp.zeros_like(dS)
            dw_ref[...] = jnp.zeros_like(dw_ref)
            dlb_ref[...] = jnp.zeros_like(dlb_ref)

        lanes = [slice(hh * HEAD, (hh + 1) * HEAD) for hh in range(HS)]
        lbs = [_lower_bound(lbp_ref[:, ls]) for ls in lanes]
        ws = [w_ref[:, ls] for ls in lanes]

        def chunk(i, carry):
            c = NCB - 1 - i
            rows = pl.ds(pl.multiple_of(c * C, C), C)
            loaded = [([p_ref[s, rows, ls] for s in range(4)], st_ref[hh, c], dS[hh], o_ref[rows, ls], da_ref[rows, ls],
                       dw_ref[:, ls], dlb_ref[:, ls]) for hh, ls in enumerate(lanes)]
            done = []
            for hh, ((q, f, v, g), st, dst, ov, da, dw_old, dlb_old) in enumerate(loaded):
                lb, w = lbs[hh], ws[hh]
                sq, qs, sf, fg, tril, ea, ena, eb, ee, ebend = _hgrn2_gates(q, f, lb)
                k = 1.0 - fg
                qt, kt, qb, ke = qs * ea, k * ena, qs * eb, k * ee
                p = jnp.where(tril, _bdot(qt, kt, NT), 0.0)
                r = lax.rsqrt(jnp.mean(ov * ov, axis=-1, keepdims=True) + EPS)
                oh = ov * r
                sg = _sigmoid(g)
                gs = g * sg
                dw_new = dw_old + jnp.sum(da * oh * gs, axis=0, keepdims=True)
                dg = da * oh * w * sg * (1.0 + g * (1.0 - sg))
                doh = da * w * gs
                do = r * (doh - oh * jnp.mean(doh * oh, axis=-1, keepdims=True))
                dqb = _fdot(do, st, NN)
                dp = jnp.where(tril, _bdot(do, v, NT), 0.0)
                dv = _bdot(p, do, TN) + _bdot(ke, dst, NT)
                dqt = _fdot(dp, kt, NN)
                dkt = _fdot(dp, qt, TN)
                dke = _fdot(v, dst, NN)
                dqs = dqt * ea + dqb * eb
                dk = dkt * ena + dke * ee
                db = dqt * qt + dqb * qb - dkt * kt - dke * ke
                db_end = (jnp.sum(dke * ke, axis=0, keepdims=True)
                          + ebend * jnp.sum(dst * st, axis=0, keepdims=True))
                triu = lax.broadcasted_iota(jnp.int32, (C, C), 0) <= lax.broadcasted_iota(jnp.int32, (C, C), 1)
                dlf = _mask_dot(jnp.where(triu, 1.0, 0.0), db) + db_end
                dfg = jnp.where(fg > 1e-30, dlf / fg, 0.0) - dk
                done.append((dw_new, dlb_old + jnp.sum(dfg * (1.0 - sf), axis=0, keepdims=True),
                             [(dqs * sq * (1.0 + q * (1.0 - sq))).astype(BF16),
                              (dfg * (1.0 - lb) * sf * (1.0 - sf)).astype(BF16), dv.astype(BF16), dg.astype(BF16)],
                             dst * ebend + _bdot(do, qb, TN)))
            for hh, ls in enumerate(lanes):
                dw_new, dlb_new, dsec, dst_new = done[hh]
                dw_ref[:, ls] = dw_new
                dlb_ref[:, ls] = dlb_new
                for s in range(4):
                    dp_ref[s, rows, ls] = dsec[s]
                dS[hh] = dst_new
            return carry

        lax.fori_loop(0, NCB, chunk, 0)

    rev = lambda t: NB - 1 - t
    return _pcall(
        body, [proj6, o, states, dcat, lb_param, a_norm_w], name="hgrn2_bwd", grid=(H // HS, NB),
        in_specs=[pl.BlockSpec((4, TB, W2), lambda h, t: (0, rev(t), h)),
                  pl.BlockSpec((TB, W2), lambda h, t: (rev(t), h)),
                  pl.BlockSpec((HS, NCB, HEAD, HEAD), lambda h, t: (h, rev(t), 0, 0)),
                  pl.BlockSpec((None, TB, W2), lambda h, t: (0, rev(t), h)),
                  pl.BlockSpec((2, W2), lambda h, t: (0, h)),
                  pl.BlockSpec((1, W2), lambda h, t: (0, h))],
        out_specs=[pl.BlockSpec((4, TB, W2), lambda h, t: (0, rev(t), h)),
                   pl.BlockSpec((1, W2), lambda h, t: (0, h)),
                   pl.BlockSpec((1, W2), lambda h, t: (0, h))],
        out_shape=[jax.ShapeDtypeStruct((6, T, Wd), BF16), jax.ShapeDtypeStruct((1, Wd), F32),
                   jax.ShapeDtypeStruct((1, Wd), F32)],
        scratch_shapes=[pltpu.VMEM((HS, HEAD, HEAD), F32)],
        compiler_params=_params(("parallel", "arbitrary")),
    )


_INV_SQRT2 = 1.0 / math.sqrt(2.0)
_INV_SQRT2PI = 1.0 / math.sqrt(2.0 * math.pi)


def _gelu(v):
    return 0.5 * v * (1.0 + lax.erf(v * _INV_SQRT2))


def _gelu_grad(v):
    return 0.5 * (1.0 + lax.erf(v * _INV_SQRT2)) + v * jnp.exp(-0.5 * v * v) * _INV_SQRT2PI


def _gmlp_norm(v, ln_w, ln_b):
    vf = _gelu(v)
    mu = jnp.mean(vf, axis=-1, keepdims=True)
    cen = vf - mu
    rstd = lax.rsqrt(jnp.mean(cen * cen, axis=-1, keepdims=True) + EPS)
    vh = cen * rstd
    return vh, rstd, vh * ln_w + ln_b


def _tril_mask():
    n = GMLP_CHUNK
    return lax.broadcasted_iota(jnp.int32, (n, n), 0) >= lax.broadcasted_iota(jnp.int32, (n, n), 1)


def _gmlp_fwd(proj6, cat, ln_w, ln_b, w_s, b_st):
    _, T, Wd = proj6.shape
    G, N = Wd // HEAD, GMLP_CHUNK

    def body(p_ref, cat_in, lnw_ref, lnb_ref, ws_ref, bst_ref, out_ref):
        del cat_in
        _, _, vn = _gmlp_norm(p_ref[1], lnw_ref[...], lnb_ref[...])
        tril = _tril_mask()
        for g in range(G):
            cols = slice(g * HEAD, (g + 1) * HEAD)
            z = _bdot(jnp.where(tril, ws_ref[g], 0.0), vn[:, cols], NN) + bst_ref[:, g:g + 1]
            out_ref[:, cols] = (_gelu(p_ref[0, :, cols]) * z).astype(BF16)

    return _pcall(
        body, [proj6, cat, ln_w, ln_b, w_s, b_st], name="gmlp_fwd", grid=(T // N,),
        in_specs=[pl.BlockSpec((2, N, Wd), lambda t: (2, t, 0)), pl.BlockSpec(memory_space=pl.ANY),
                  pl.BlockSpec((1, Wd), lambda t: (0, 0)), pl.BlockSpec((1, Wd), lambda t: (0, 0)),
                  pl.BlockSpec((G, N, N), lambda t: (0, 0, 0)), pl.BlockSpec((N, LANES), lambda t: (0, 0))],
        out_specs=pl.BlockSpec((None, N, Wd), lambda t: (1, t, 0)),
        out_shape=jax.ShapeDtypeStruct(cat.shape, cat.dtype),
        input_output_aliases={1: 0},
        compiler_params=_params(("arbitrary",)),
    )


def _gmlp_bwd(proj6, dproj6, dcat, ln_w, ln_b, w_s, b_st):
    _, T, Wd = proj6.shape
    G, N = Wd // HEAD, GMLP_CHUNK

    def body(p_ref, dp_in, db_ref, lnw_ref, lnb_ref, ws_ref, bst_ref, dp_ref, dlnw_ref, dlnb_ref, dws_ref, dbst_ref, dvn):
        del dp_in

        @pl.when(pl.program_id(0) == 0)
        def _():
            dlnw_ref[...] = jnp.zeros_like(dlnw_ref)
            dlnb_ref[...] = jnp.zeros_like(dlnb_ref)
            dws_ref[...] = jnp.zeros_like(dws_ref)
            dbst_ref[...] = jnp.zeros_like(dbst_ref)

        lnw = lnw_ref[...]
        v = p_ref[1]
        vh, rstd, vn = _gmlp_norm(v, lnw, lnb_ref[...])
        tril = _tril_mask()
        lane = lax.broadcasted_iota(jnp.int32, (N, LANES), 1)
        dbst = jnp.zeros((N, LANES), F32)
        for g in range(G):
            cols = slice(g * HEAD, (g + 1) * HEAD)
            wt = jnp.where(tril, ws_ref[g], 0.0)
            vn_g = vn[:, cols]
            z = _bdot(wt, vn_g, NN) + bst_ref[:, g:g + 1]
            u = p_ref[0, :, cols]
            db = db_ref[:, cols]
            dp_ref[0, :, cols] = (db * z * _gelu_grad(u)).astype(BF16)
            dz = db * _gelu(u)
            dbst = dbst + jnp.where(lane == g, jnp.sum(dz, axis=1, keepdims=True), 0.0)
            dws_ref[g] += jnp.where(tril, _bdot(dz, vn_g, NT), 0.0)
            dvn[:, cols] = _bdot(wt, dz, TN)
        dbst_ref[...] += dbst
        dv = dvn[...]
        dlnw_ref[...] += jnp.sum(dv * vh, axis=0, keepdims=True)
        dlnb_ref[...] += jnp.sum(dv, axis=0, keepdims=True)
        dvh = dv * lnw
        dvf = rstd * (dvh - jnp.mean(dvh, axis=-1, keepdims=True) - vh * jnp.mean(dvh * vh, axis=-1, keepdims=True))
        dp_ref[1] = (dvf * _gelu_grad(v)).astype(BF16)

    vec = pl.BlockSpec((1, Wd), lambda t: (0, 0))
    return _pcall(
        body, [proj6, dproj6, dcat, ln_w, ln_b, w_s, b_st], name="gmlp_bwd", grid=(T // N,),
        in_specs=[pl.BlockSpec((2, N, Wd), lambda t: (2, t, 0)), pl.BlockSpec(memory_space=pl.ANY),
                  pl.BlockSpec((None, N, Wd), lambda t: (1, t, 0)), vec, vec,
                  pl.BlockSpec((G, N, N), lambda t: (0, 0, 0)), pl.BlockSpec((N, LANES), lambda t: (0, 0))],
        out_specs=[pl.BlockSpec((2, N, Wd), lambda t: (2, t, 0)), vec, vec,
                   pl.BlockSpec((G, N, N), lambda t: (0, 0, 0)), pl.BlockSpec((N, LANES), lambda t: (0, 0))],
        out_shape=[jax.ShapeDtypeStruct(dproj6.shape, dproj6.dtype), jax.ShapeDtypeStruct((1, Wd), F32),
                   jax.ShapeDtypeStruct((1, Wd), F32), jax.ShapeDtypeStruct((G, N, N), F32),
                   jax.ShapeDtypeStruct((N, LANES), F32)],
        scratch_shapes=[pltpu.VMEM((N, Wd), F32)],
        input_output_aliases={1: 0},
        compiler_params=_params(("arbitrary",)),
    )


def _place():
    return lax.axis_index("x"), lax.axis_index("y"), lax.axis_index("c")


def _other_chips(x, y):
    return [(1 - x, y), (x, 1 - y), (1 - x, 1 - y)]


def _rcopy(src, dst, ssem, rsem, k, to):
    return pltpu.make_async_remote_copy(src_ref=src, dst_ref=dst, send_sem=ssem.at[k], recv_sem=rsem.at[k],
                                        device_id=to, device_id_type=MESH)


def _cast_slot(name, w, place):
    _, R, C = w.shape
    tr = _row_block(R, C)

    def body(p_ref, w_ref, o_ref):
        del p_ref
        o_ref[...] = w_ref[...].astype(BF16)

    return _pcall(
        body, [place[0], w], name=name, prefetch=1, grid=(R // tr,),
        in_specs=[pl.BlockSpec((None, tr, C), lambda i, p: (0, i, 0))],
        out_specs=pl.BlockSpec((None, tr, C), lambda i, p: (p[0], i, 0)),
        out_shape=jax.ShapeDtypeStruct((4, R, C), BF16),
        compiler_params=_params(("parallel",)))


def _half(ref, px, py, pc, Rh):
    return ref.at[2 * px + py, pl.ds(pc * Rh, Rh), :]


def _split_start(name, bufs, n_sem, copies):
    nb = len(bufs)

    def body(*refs):
        for cp in copies(refs[:nb], refs[nb], refs[nb + 1], False):
            cp.start()
        refs[-1][...] = jnp.zeros_like(refs[-1])

    outs = _pcall(
        body, [pltpu.with_memory_space_constraint(b, pltpu.HBM) for b in bufs], same=bufs, token=-1, name=name,
        in_specs=[HBM_SPEC] * nb, out_specs=(SEM_SPEC, SEM_SPEC, *[HBM_SPEC] * nb, VMEM_SPEC),
        out_shape=(pltpu.SemaphoreType.DMA((n_sem,)), pltpu.SemaphoreType.DMA((n_sem,)),
                   *[pltpu.HBM(b.shape, b.dtype) for b in bufs], jax.ShapeDtypeStruct((8, LANES), F32)),
        input_output_aliases={i: 2 + i for i in range(nb)},
        compiler_params=pltpu.CompilerParams(has_side_effects=SIDE_EFFECT))
    return outs[:-1]


def _split_wait(name, started, copies):
    ssem, rsem, *bufs = started
    nb = len(bufs)

    def body(*refs):
        for cp in copies(refs[:nb], refs[nb], refs[nb + 1], True):
            cp.wait_send()
            cp.wait_recv()

    outs = _pcall(
        body, [*bufs, ssem, rsem], name=name, in_specs=[HBM_SPEC] * nb + [SEM_SPEC, SEM_SPEC],
        out_specs=tuple([HBM_SPEC] * nb), out_shape=tuple(pltpu.HBM(b.shape, b.dtype) for b in bufs),
        input_output_aliases={i: i for i in range(nb)},
        compiler_params=pltpu.CompilerParams(has_side_effects=SIDE_EFFECT))
    return outs


def _gather_copies(Rh):
    def copies(refs, ssem, rsem, waiting):
        x, y, c = _place()
        mine = _half(refs[0], x, y, c, Rh)
        return [_rcopy(mine, _half(refs[0], px, py, c, Rh) if waiting else mine, ssem, rsem, j, (px, py, c))
                for j, (px, py) in enumerate(_other_chips(x, y))]
    return copies


def _swap_copies(Rh):
    def copies(refs, ssem, rsem, waiting):
        x, y, c = _place()
        return [_rcopy(refs[0].at[:, pl.ds((1 - c) * Rh, Rh), :], refs[1], ssem, rsem, 0, (x, y, 1 - c))]
    return copies


def _exchange_copies(refs, ssem, rsem, waiting):
    x, y, c = _place()
    return [_rcopy(refs[0].at[2 * px + py], refs[1].at[j], ssem, rsem, j, (px, py, c))
            for j, (px, py) in enumerate(_other_chips(x, y))]


def _landing(shape, dtype):
    return lax.empty(shape, dtype)


def _forward_copies(Rh):
    def copies(refs, ssem, rsem, waiting):
        x, y, c = _place()
        return [_rcopy(_half(refs[0], px, py, c, Rh), _half(refs[0], px, py, 1 - c if waiting else c, Rh),
                       ssem, rsem, j, (x, y, 1 - c)) for j, (px, py) in enumerate(_other_chips(x, y))]
    return copies


def _share_copies(Rh):
    def copies(refs, ssem, rsem, waiting):
        x, y, c = _place()
        mine = refs[0].at[pl.ds(c * Rh, Rh), :]
        return [_rcopy(mine, refs[0].at[pl.ds((1 - c) * Rh, Rh), :] if waiting else mine, ssem, rsem, 0, (x, y, 1 - c))]
    return copies


def _sum_all_devices(v):
    R, L = v.shape

    def body(v_ref, out_ref, buf, ssem, rsem):
        x, y, c = _place()
        me = 4 * x + 2 * y + c
        buf[me] = v_ref[...]
        copies = []
        for m in range(1, 8):
            to = (x ^ (m >> 2), y ^ ((m >> 1) & 1), c ^ (m & 1))
            copies.append(_rcopy(v_ref, buf.at[me], ssem, rsem, m - 1, to))
        for cp in copies:
            cp.start()
        for cp in copies:
            cp.wait()
        acc = buf[0]
        for d in range(1, 8):
            acc = acc + buf[d]
        out_ref[...] = acc

    return _pcall(
        body, [v], name="sum_all_devices", in_specs=[VMEM_SPEC], out_specs=VMEM_SPEC,
        out_shape=jax.ShapeDtypeStruct((R, L), F32),
        scratch_shapes=[pltpu.VMEM((8, R, L), F32), pltpu.SemaphoreType.DMA((7,)), pltpu.SemaphoreType.DMA((7,))],
        compiler_params=pltpu.CompilerParams(vmem_limit_bytes=VMEM_LIMIT))


def _elementwise(name, body, ins, in_specs, outs, out_specs, grid):
    return _pcall(body, ins, name=name, grid=grid, in_specs=in_specs, out_specs=out_specs, out_shape=outs,
                  compiler_params=_params(("parallel",) * len(grid)))


def _row_block(R, C, bytes_per_row_elem=4, target=1 << 20):
    return _tile(R, max(16, target // (C * bytes_per_row_elem)), 16)


def _add_halves(name, g4, land, c_idx):
    S, R, C = g4.shape
    Rh = R // 2
    tr = _row_block(Rh, C)
    nb = Rh // tr

    def body(c_ref, g_ref, l_ref, o_ref):
        del c_ref
        o_ref[...] = (g_ref[...].astype(F32) + l_ref[...].astype(F32)).astype(BF16)

    return _pcall(
        body, [c_idx, g4, land], name=name, prefetch=1, grid=(S, nb),
        in_specs=[pl.BlockSpec((None, tr, C), lambda s, i, c: (s, c[0] * nb + i, 0)),
                  pl.BlockSpec((None, tr, C), lambda s, i, c: (s, i, 0))],
        out_specs=pl.BlockSpec((None, tr, C), lambda s, i, c: (s, i, 0)),
        out_shape=jax.ShapeDtypeStruct((S, Rh, C), BF16),
        compiler_params=_params(("parallel", "parallel")))


def _sum_chips(name, land, pb, place):
    S, Rh, C = land.shape
    tr = _row_block(Rh, C)
    nb = Rh // tr

    def body(p_ref, c_ref, l_ref, own_ref, o_ref):
        del p_ref, c_ref
        acc = own_ref[...].astype(F32)
        for s in range(S):
            acc = acc + l_ref[s].astype(F32)
        o_ref[...] = acc

    return _pcall(
        body, [place[0], place[1], land, pb], name=name, prefetch=2, grid=(nb,),
        in_specs=[pl.BlockSpec((S, tr, C), lambda i, p, c: (0, i, 0)),
                  pl.BlockSpec((None, tr, C), lambda i, p, c: (p[0], i, 0))],
        out_specs=pl.BlockSpec((tr, C), lambda i, p, c: (c[0] * nb + i, 0)),
        out_shape=jax.ShapeDtypeStruct((2 * Rh, C), F32),
        compiler_params=_params(("parallel",)))


def _adamw(name, w, g, m, v):
    R, C = g.shape
    tr = _row_block(R, C, target=1 << 19)
    c1 = 1.0 - ADAM_B1 ** ADAM_STEP
    c2 = 1.0 - ADAM_B2 ** ADAM_STEP

    def body(w_ref, g_ref, m_ref, v_ref, go_ref, d_ref, nm_ref, nv_ref):
        gv = g_ref[...]
        go_ref[...] = gv
        nm = ADAM_B1 * m_ref[...] + (1.0 - ADAM_B1) * gv
        nv = ADAM_B2 * v_ref[...] + (1.0 - ADAM_B2) * (gv * gv)
        nm_ref[...] = nm
        nv_ref[...] = nv
        d_ref[...] = -ADAM_LR * ((nm / c1) / (jnp.sqrt(nv / c2) + ADAM_EPS) + ADAM_WD * w_ref[...])

    spec = pl.BlockSpec((tr, C), lambda i: (i, 0))
    wspec = pl.BlockSpec((None, tr, C), lambda i: (0, i, 0)) if w.ndim == 3 else spec
    shp = jax.ShapeDtypeStruct((R, C), F32)
    return _elementwise(name, body, [w, g, m, v], [wspec, spec, wspec, wspec], [shp] * 4, [spec] * 4, (R // tr,))


def _lb_grad(dlb, lb_param):
    def body(d_ref, p_ref, o_ref):
        lb = _lower_bound(p_ref[...])
        t = d_ref[...] * lb * (1.0 - lb)
        o_ref[0:1, :] = t
        o_ref[1:2, :] = -t

    return _pcall(body, [dlb, lb_param], name="lb_grad", in_specs=[VMEM_SPEC, VMEM_SPEC], out_specs=VMEM_SPEC,
                  out_shape=jax.ShapeDtypeStruct(lb_param.shape, F32))


class _Reduction:
    def __init__(self, tag, g4, place):
        self.tag, self.place = tag, place
        _, R, C = g4.shape
        self.Rh = R // 2
        self.state = _split_start("swap_start_" + tag, [g4, _landing((4, self.Rh, C), g4.dtype)], 1, _swap_copies(self.Rh))

    def exchange(self):
        g4, land = _split_wait("swap_wait_" + self.tag, self.state, _swap_copies(self.Rh))
        pb = _add_halves("addh_" + self.tag, g4, land, self.place[1])
        self.state = _split_start("xchg_start_" + self.tag, [pb, _landing((3,) + pb.shape[1:], BF16)], 3, _exchange_copies)

    def finish(self):
        pb, got = _split_wait("xchg_wait_" + self.tag, self.state, _exchange_copies)
        full = _sum_chips("sumc_" + self.tag, got, pb, self.place)
        self.state = _split_start("share_start_" + self.tag, [full], 1, _share_copies(self.Rh))

    def update(self, w, m, v):
        (full,) = _split_wait("share_wait_" + self.tag, self.state, _share_copies(self.Rh))
        return tuple(a[None] for a in _adamw("adamw_" + self.tag, w, full, m, v))


def kernel(x, p, pre_mix_w, w_in, lb_param, a_norm_w, gmlp_ln_w, gmlp_ln_b, w_spatial, b_spatial, w_out, post_mix_w, pre_ffn_w, w_gate, w_up, w_down, post_ffn_w, w_ple, w_ple_gate, post_ple_w, loss_target, m_pre_mix_w, m_w_in, m_lb_param, m_a_norm_w, m_gmlp_ln_w, m_gmlp_ln_b, m_w_spatial, m_b_spatial, m_w_out, m_post_mix_w, m_pre_ffn_w, m_w_gate, m_w_up, m_w_down, m_post_ffn_w, m_w_ple, m_w_ple_gate, m_post_ple_w, v_pre_mix_w, v_w_in, v_lb_param, v_a_norm_w, v_gmlp_ln_w, v_gmlp_ln_b, v_w_spatial, v_b_spatial, v_w_out, v_post_mix_w, v_pre_ffn_w, v_w_gate, v_w_up, v_w_down, v_post_ffn_w, v_w_ple, v_w_ple_gate, v_post_ple_w):
    T, D = x.shape[1], x.shape[2]
    Wd = D // 2
    G = Wd // HEAD
    NCH = 4
    Ci = w_in.shape[2]
    Fs = w_gate.shape[2]
    Dq = w_out.shape[1]
    Pd = w_ple.shape[1]
    xs, ps, tgt = x[0], p[0, 0], loss_target[0]
    c_idx = lax.axis_index("c").astype(jnp.int32).reshape((1,))
    place = ((2 * lax.axis_index("x") + lax.axis_index("y")).astype(jnp.int32).reshape((1,)), c_idx)

    swap = lambda a: jnp.swapaxes(a, 1, 2)
    transposed = ("w_gate", "w_up")
    weights = dict(w_in=w_in, w_out=w_out, w_gate=swap(w_gate), w_up=swap(w_up), w_down=w_down,
                   w_ple_gate=w_ple_gate, w_ple=w_ple)
    _Order.last = None
    gathering = {}
    for tag, w in weights.items():
        buf = _cast_slot("cast_" + tag, w, place)
        gathering[tag] = _split_start("gather_start_" + tag, [buf], 3, _gather_copies(buf.shape[1] // 2))

    forwarding = {}

    def forward_start(tag):
        Rh = weights[tag].shape[1] // 2
        (buf,) = _split_wait("gather_wait_" + tag, gathering.pop(tag), _gather_copies(Rh))
        forwarding[tag] = _split_start("forward_start_" + tag, [buf], 3, _forward_copies(Rh))

    def gathered(tag):
        (buf,) = _split_wait("forward_wait_" + tag, forwarding.pop(tag), _forward_copies(weights[tag].shape[1] // 2))
        return buf

    res = {}
    moments = dict(w_in=(m_w_in, v_w_in), w_out=(m_w_out, v_w_out), w_gate=(swap(m_w_gate), swap(v_w_gate)),
                   w_up=(swap(m_w_up), swap(v_w_up)), w_down=(m_w_down, v_w_down), w_ple=(m_w_ple, v_w_ple),
                   w_ple_gate=(m_w_ple_gate, v_w_ple_gate))
    reducing = {}

    def reduce_start(tag, g4):
        reducing[tag] = _Reduction(tag, g4, place)

    def reduce_exchange(tag):
        reducing[tag].exchange()

    def reduce_finish(tag):
        reducing[tag].finish()

    def reduce_update(tag):
        outs = reducing.pop(tag).update(weights[tag], *moments[tag])
        res[tag] = tuple(swap(a) for a in outs) if tag in transposed else outs

    tm, tn, tk = _tile(T, 1024), _tile(D, 1024), _tile(Wd, 2048)
    ts = _tile(math.gcd(Wd, Ci), 1024)
    tkt = _tile(T, 2048)
    n_sec, n_shd = Wd // ts, Ci // ts
    b_st = jnp.pad(b_spatial[0].T, ((0, 0), (0, LANES - G)))

    forward_start("w_in")
    h1 = _norm_in(xs, pre_mix_w)
    gw_in = gathered("w_in")
    proj6 = _mm("proj", [h1, gw_in],
                [pl.BlockSpec((tm, tk), lambda i, j, k: (i, k)),
                 pl.BlockSpec((None, tk, ts), lambda i, j, k: (j // n_shd, k, j % n_shd))],
                jax.ShapeDtypeStruct((6, T, Wd), F32),
                pl.BlockSpec((None, tm, ts), lambda i, j, k: (j // n_sec, i, j % n_sec)),
                (tm, ts), (T // tm, 6 * n_sec, D // tk), NN)
    forward_start("w_out")
    cat, o_a, states = _hgrn2_fwd(proj6, lb_param, a_norm_w)
    forward_start("w_gate")
    cat = _gmlp_fwd(proj6, cat, gmlp_ln_w, gmlp_ln_b, w_spatial[0], b_st)
    nkc = Wd // tk
    gw_out = gathered("w_out").reshape(D, D)
    mix = _mm("mix", [cat, gw_out],
              [pl.BlockSpec((None, tm, tk), lambda i, j, k: (k // nkc, i, k % nkc)),
               pl.BlockSpec((tk, tn), lambda i, j, k: (k, j))],
              jax.ShapeDtypeStruct((T, D), F32), pl.BlockSpec((tm, tn), lambda i, j, k: (i, j)),
              (tm, tn), (T // tm, D // tn, D // tk), NN)
    forward_start("w_up")
    x1, h2 = _mid1(xs, mix, post_mix_w, pre_ffn_w)
    gw_gate = gathered("w_gate")
    gw_up = gathered("w_up")
    gt, up, act = _ffn_up(h2, gw_gate, gw_up)
    tnf = _tile(D, 512)
    forward_start("w_down")
    forward_start("w_ple_gate")
    forward_start("w_ple")
    gw_down = gathered("w_down")
    ff = _mm("ffn_down", [act, gw_down],
             [pl.BlockSpec((None, tm, Fs), lambda i, j, k: (k, i, 0)),
              pl.BlockSpec((None, Fs, tnf), lambda i, j, k: (k, 0, j))],
             jax.ShapeDtypeStruct((T, D), F32), pl.BlockSpec((tm, tnf), lambda i, j, k: (i, j)),
             (tm, tnf), (T // tm, D // tnf, NCH), NN)
    x2, x2b = _mid2(x1, ff, post_ffn_w)
    gw_pg = gathered("w_ple_gate").reshape(D, D)
    gl = _mm("ple_gate", [x2b, gw_pg],
             [pl.BlockSpec((tm, tk), lambda i, j, k: (i, k)), pl.BlockSpec((tk, tn), lambda i, j, k: (k, j))],
             jax.ShapeDtypeStruct((T, D), F32), pl.BlockSpec((tm, tn), lambda i, j, k: (i, j)),
             (tm, tn), (T // tm, D // tn, D // tk), NN)
    tq = _tile(Dq, 1024)
    nq = Dq // tq
    gw_ple = gathered("w_ple")
    e = _mm("ple_embed", [ps, gw_ple],
            [pl.BlockSpec((tm, Pd), lambda i, j, k: (i, 0)),
             pl.BlockSpec((None, Pd, tq), lambda i, j, k: (j // nq, 0, j % nq))],
            jax.ShapeDtypeStruct((T, D), F32), pl.BlockSpec((tm, tq), lambda i, j, k: (i, j)),
            (tm, tq), (T // tm, D // tq, 1), NN)

    d3, de, dgl, dw_pp, loss_vec = _head(x2, gl, e, post_ple_w, tgt)
    g_ple = _mm("d_w_ple", [ps, de],
                [pl.BlockSpec((tkt, Pd), lambda i, j, k: (k, 0)), pl.BlockSpec((tkt, tq), lambda i, j, k: (k, j))],
                jax.ShapeDtypeStruct((NCH, Pd, Dq), PARTIAL),
                pl.BlockSpec((None, Pd, tq), lambda i, j, k: (j // nq, 0, j % nq)),
                (Pd, tq), (1, D // tq, T // tkt), TN)
    reduce_start("w_ple", g_ple)
    tmw = _tile(D, 1024)
    g_pg = _mm("d_w_ple_gate", [x2b, dgl],
               [pl.BlockSpec((tkt, tq), lambda i, j, k: (k, i)), pl.BlockSpec((tkt, tn), lambda i, j, k: (k, j))],
               jax.ShapeDtypeStruct((NCH, Dq, D), PARTIAL), pl.BlockSpec((None, tq, tn), lambda i, j, k: (i // nq, i % nq, j)),
               (tq, tn), (D // tq, D // tn, T // tkt), TN)
    reduce_start("w_ple_gate", g_pg)
    reduce_exchange("w_ple")
    dx2 = _mm("d_x2", [dgl, gw_pg],
              [pl.BlockSpec((tm, tk), lambda i, j, k: (i, k)), pl.BlockSpec((tn, tk), lambda i, j, k: (j, k))],
              jax.ShapeDtypeStruct((T, D), F32), pl.BlockSpec((tm, tn), lambda i, j, k: (i, j)),
              (tm, tn), (T // tm, D // tn, D // tk), NT,
              add=d3, add_spec=pl.BlockSpec((tm, tn), lambda i, j, k: (i, j)))
    reduce_exchange("w_ple_gate")
    dff, dw_pff = _bwd_ffn_norm(dx2, ff, post_ffn_w)
    g_down = _mm("d_w_down", [act, dff],
                 [pl.BlockSpec((None, tkt, Fs), lambda i, j, k: (i, k, 0)),
                  pl.BlockSpec((tkt, tnf), lambda i, j, k: (k, j))],
                 jax.ShapeDtypeStruct((NCH, Fs, D), PARTIAL), pl.BlockSpec((None, Fs, tnf), lambda i, j, k: (i, 0, j)),
                 (Fs, tnf), (NCH, D // tnf, T // tkt), TN)
    reduce_start("w_down", g_down)
    reduce_finish("w_ple")
    dgt, dup = _ffn_dact(dff, gw_down, gt, up)
    reduce_exchange("w_down")
    reduce_finish("w_ple_gate")
    reduce_update("w_ple")
    wgrad_specs = [pl.BlockSpec((None, tkt, Fs), lambda i, j, k: (i, k, 0)),
                   pl.BlockSpec((tkt, tnf), lambda i, j, k: (k, j))]
    wgrad_out = pl.BlockSpec((None, Fs, tnf), lambda i, j, k: (i, 0, j))
    g_gate = _mm("d_w_gate", [dgt, h2], wgrad_specs, jax.ShapeDtypeStruct((NCH, Fs, D), PARTIAL), wgrad_out,
                 (Fs, tnf), (NCH, D // tnf, T // tkt), TN)
    reduce_start("w_gate", g_gate)
    reduce_update("w_ple_gate")
    g_up = _mm("d_w_up", [dup, h2], wgrad_specs, jax.ShapeDtypeStruct((NCH, Fs, D), PARTIAL), wgrad_out,
               (Fs, tnf), (NCH, D // tnf, T // tkt), TN)
    reduce_start("w_up", g_up)
    reduce_exchange("w_gate")
    tmh = _tile(T, 1024)
    a_sp = pl.BlockSpec((None, tmh, Fs), lambda i, j, k: (k, i, 0))
    b_sp = pl.BlockSpec((None, Fs, tnf), lambda i, j, k: (k, 0, j))
    dh2 = _mm("d_h2", [dgt, gw_gate, dup, gw_up], [a_sp, b_sp, a_sp, b_sp],
              jax.ShapeDtypeStruct((T, D), F32), pl.BlockSpec((tmh, tnf), lambda i, j, k: (i, j)),
              (tmh, tnf), (T // tmh, D // tnf, NCH), NN)
    reduce_exchange("w_up")
    reduce_finish("w_down")
    dx1, dmix, dw_pf, dw_pm = _bwd_mid(dx2, dh2, x1, pre_ffn_w, mix, post_mix_w)
    reduce_update("w_down")
    tw = _tile(Wd, 1024)
    nw = Wd // tw
    nwq = Wd // tq
    g_out = _mm("d_w_out", [cat, dmix],
                [pl.BlockSpec((None, tkt, tq), lambda i, j, k: (i // nwq, k, i % nwq)),
                 pl.BlockSpec((tkt, tn), lambda i, j, k: (k, j))],
                jax.ShapeDtypeStruct((NCH, Dq, D), PARTIAL), pl.BlockSpec((None, tq, tn), lambda i, j, k: (i // nq, i % nq, j)),
                (tq, tn), (D // tq, D // tn, T // tkt), TN)
    reduce_start("w_out", g_out)
    reduce_finish("w_gate")
    dcat = _mm("d_cat", [dmix, gw_out],
               [pl.BlockSpec((tm, tk), lambda i, j, k: (i, k)), pl.BlockSpec((tw, tk), lambda i, j, k: (j, k))],
               jax.ShapeDtypeStruct((2, T, Wd), F32), pl.BlockSpec((None, tm, tw), lambda i, j, k: (j // nw, i, j % nw)),
               (tm, tw), (T // tm, D // tw, D // tk), NT)
    reduce_exchange("w_out")
    reduce_update("w_gate")
    dproj6, dw_an, dlb = _hgrn2_bwd(proj6, o_a, states, dcat, lb_param, a_norm_w)
    reduce_finish("w_up")
    dproj6, dw_lnw, dw_lnb, dw_sp, dw_bst = _gmlp_bwd(proj6, dproj6, dcat, gmlp_ln_w, gmlp_ln_b, w_spatial[0], b_st)
    reduce_update("w_up")
    g_in = _mm("d_w_in", [h1, dproj6],
               [pl.BlockSpec((tkt, tmw), lambda i, j, k: (k, i)),
                pl.BlockSpec((None, tkt, ts), lambda i, j, k: (j // n_sec, k, j % n_sec))],
               jax.ShapeDtypeStruct((NCH, D, Ci), PARTIAL),
               pl.BlockSpec((None, tmw, ts), lambda i, j, k: (j // n_shd, i, j % n_shd)),
               (tmw, ts), (D // tmw, 6 * n_sec, T // tkt), TN)
    reduce_start("w_in", g_in)
    reduce_finish("w_out")
    reduce_exchange("w_in")
    dh1 = _mm("d_h1", [dproj6, gw_in],
              [pl.BlockSpec((None, tm, ts), lambda i, j, k: (k // n_sec, i, k % n_sec)),
               pl.BlockSpec((None, tn, ts), lambda i, j, k: (k // n_shd, j, k % n_shd))],
              jax.ShapeDtypeStruct((T, D), F32), pl.BlockSpec((tm, tn), lambda i, j, k: (i, j)),
              (tm, tn), (T // tm, D // tn, 6 * n_sec), NT)
    reduce_update("w_out")
    grad_x, dw_pre = _bwd_in(dx1, dh1, xs, pre_mix_w)

    small = [("pre_mix_w", dw_pre), ("lb", dlb), ("a_norm_w", dw_an), ("gmlp_ln_w", dw_lnw), ("gmlp_ln_b", dw_lnb),
             ("w_spatial", dw_sp), ("b_spatial_t", dw_bst), ("post_mix_w", dw_pm), ("pre_ffn_w", dw_pf),
             ("post_ffn_w", dw_pff), ("post_ple_w", dw_pp), ("loss", loss_vec)]
    rows = [a.size // LANES for _, a in small]
    total = sum(rows)
    pad = (-total) % 8
    packed = jnp.concatenate([a.reshape(-1, LANES) for _, a in small] + [jnp.zeros((pad, LANES), F32)], axis=0)
    summed = _sum_all_devices(packed)
    off, piece = 0, {}
    for (nm, a), r in zip(small, rows):
        piece[nm] = summed[off:off + r].reshape(a.shape)
        off += r
    loss = jnp.sum(piece["loss"])
    g_small = {
        "pre_mix_w": piece["pre_mix_w"], "lb_param": _lb_grad(piece["lb"], lb_param), "a_norm_w": piece["a_norm_w"],
        "gmlp_ln_w": piece["gmlp_ln_w"], "gmlp_ln_b": piece["gmlp_ln_b"], "w_spatial": piece["w_spatial"][None],
        "b_spatial": piece["b_spatial_t"][:, :G].T[None], "post_mix_w": piece["post_mix_w"],
        "pre_ffn_w": piece["pre_ffn_w"], "post_ffn_w": piece["post_ffn_w"], "post_ple_w": piece["post_ple_w"],
    }
    w_small = dict(pre_mix_w=(pre_mix_w, m_pre_mix_w, v_pre_mix_w), lb_param=(lb_param, m_lb_param, v_lb_param),
                   a_norm_w=(a_norm_w, m_a_norm_w, v_a_norm_w), gmlp_ln_w=(gmlp_ln_w, m_gmlp_ln_w, v_gmlp_ln_w),
                   gmlp_ln_b=(gmlp_ln_b, m_gmlp_ln_b, v_gmlp_ln_b), w_spatial=(w_spatial, m_w_spatial, v_w_spatial),
                   b_spatial=(b_spatial, m_b_spatial, v_b_spatial), post_mix_w=(post_mix_w, m_post_mix_w, v_post_mix_w),
                   pre_ffn_w=(pre_ffn_w, m_pre_ffn_w, v_pre_ffn_w), post_ffn_w=(post_ffn_w, m_post_ffn_w, v_post_ffn_w),
                   post_ple_w=(post_ple_w, m_post_ple_w, v_post_ple_w))
    names_small = list(w_small)

    def pack(arrs):
        flat = jnp.concatenate([a.reshape(-1, LANES) for a in arrs], axis=0)
        return jnp.pad(flat, ((0, (-flat.shape[0]) % 16), (0, 0)))

    pk = [pack([w_small[n][i] for n in names_small]) for i in range(3)]
    pg = pack([g_small[n] for n in names_small])
    _, sd, sm, sv = _adamw("adamw_small", pk[0], pg, pk[1], pk[2])
    off = 0
    for n in names_small:
        shp = w_small[n][0].shape
        r = w_small[n][0].size // LANES
        res[n] = (g_small[n].reshape(shp), sd[off:off + r].reshape(shp), sm[off:off + r].reshape(shp),
                  sv[off:off + r].reshape(shp))
        off += r

    reduce_finish("w_in")
    reduce_update("w_in")

    order =["pre_mix_w", "w_in", "lb_param", "a_norm_w", "gmlp_ln_w", "gmlp_ln_b", "w_spatial", "b_spatial", "w_out",
             "post_mix_w", "pre_ffn_w", "w_gate", "w_up", "w_down", "post_ffn_w", "w_ple", "w_ple_gate", "post_ple_w"]
    return (loss, grad_x[None], *[res[n][0] for n in order], *[res[n][1] for n in order],
            *[res[n][2] for n in order], *[res[n][3] for n in order])
```
